```python
import math
import jax, jax.numpy as jnp
from jax import lax
import numpy as np

D_MODEL = 1024
BATCH = 8
SEQ = 4096
DEPTH = 2
DEC_BATCH = 32
DEC_SEQ = 1
PAST_LEN = 16384
PAGE_SIZE = 128

N_EVEN = (DEPTH + 1) // 2
N_ODD = DEPTH // 2

N_HEADS = 8
HEAD_DIM = 64
KV_HEADS = 2
GROUP = N_HEADS // KV_HEADS
ATTN_WIDTH = N_HEADS * HEAD_DIM
CMP_LEN = 32
CMP_STRIDE = 16
N_SUB = CMP_LEN // CMP_STRIDE
SEL_BLOCK = 64
SEL_RATIO = SEL_BLOCK // CMP_STRIDE
N_SELECT = 16
WINDOW = 512
Q_BLOCK = 64
N_BRANCH = 3
KV_COLS = 2 * KV_HEADS * HEAD_DIM
SCALE = HEAD_DIM ** -0.5
CONV_CH = D_MODEL // 2
CONV_WIDTH = 31
POOL_WINDOWS = (2, 4, 8, 16)
POOL_GROUP = D_MODEL // len(POOL_WINDOWS)
POOL_BUF = max(POOL_WINDOWS) - 1
D_FF = -(-8 * D_MODEL // (3 * 256)) * 256

IN_A = ATTN_WIDTH + N_BRANCH * KV_COLS + N_HEADS * N_BRANCH + 2 * CONV_CH
MIX_A = ATTN_WIDTH + CONV_CH
EPS = 1e-6
BIG = 1e9

kernel_name = 'nsa_conformer_pool_hybrid_step'


def rmsnorm(x, g):
    x32 = x.astype(jnp.float32)
    y = x32 * lax.rsqrt(jnp.mean(x32 * x32, axis=-1, keepdims=True) + EPS)
    return (y * g.astype(jnp.float32)).astype(x.dtype)


def masked_softmax(s, mask):
    s = jnp.where(mask, s, -jnp.inf)
    m = jnp.max(s, axis=-1, keepdims=True)
    m = jnp.where(jnp.isfinite(m), m, 0.0)
    e = jnp.where(mask, jnp.exp(s - m), 0.0)
    return e / jnp.maximum(jnp.sum(e, axis=-1, keepdims=True), jnp.finfo(jnp.float32).tiny)


def swiglu(h, w_gate, w_up, w_down):
    return (jax.nn.silu(h @ w_gate) * (h @ w_up)) @ w_down


def split_even_proj(h, w_in):
    b, t = h.shape[0], h.shape[1]
    z = h @ w_in
    q = z[..., :ATTN_WIDTH].reshape(b, t, KV_HEADS, GROUP, HEAD_DIM)
    off = ATTN_WIDTH
    kv = []
    for _ in range(N_BRANCH):
        kv.append(z[..., off:off + KV_COLS].reshape(b, t, 2, KV_HEADS, HEAD_DIM))
        off += KV_COLS
    gates = z[..., off:off + N_HEADS * N_BRANCH].reshape(b, t, KV_HEADS, GROUP, N_BRANCH)
    off += N_HEADS * N_BRANCH
    u = z[..., off:]
    return q, kv[0], kv[1], kv[2], gates, u


def compress_kv(k, pos, w1, w2):
    b, L = k.shape[0], k.shape[1]
    n_ch = L // CMP_STRIDE
    nb = n_ch - N_SUB + 1
    ch = k[:, :n_ch * CMP_STRIDE].reshape(b, n_ch, CMP_STRIDE, KV_HEADS, HEAD_DIM)
    w1s = w1.reshape(N_SUB, CMP_STRIDE, HEAD_DIM, HEAD_DIM)
    hid = jnp.einsum('ld,lde->e', pos, w1)
    for c in range(N_SUB):
        hid = hid + jnp.einsum('bnlgd,lde->bnge', ch[:, c:c + nb], w1s[c])
    return jnp.einsum('bnge,ef->bngf', jax.nn.gelu(hid), w2)


def compressed_attention(q, t_pos, kcmp, vcmp):
    nb = kcmp.shape[1]
    s = jnp.einsum('bqghd,bngd->bghqn', q, kcmp).astype(jnp.float32) * SCALE
    end = jnp.arange(nb) * CMP_STRIDE + (CMP_LEN - 1)
    p = masked_softmax(s, end[None, :] <= t_pos[:, None])
    o = jnp.einsum('bghqn,bngd->bqghd', p.astype(vcmp.dtype), vcmp)
    return o, p


def select_blocks(p_cmp, t_pos, ns):
    imp = jnp.sum(p_cmp, axis=2)
    nb = imp.shape[-1]
    P = jnp.pad(imp, ((0, 0), (0, 0), (0, 0), (N_SUB - 1, SEL_RATIO * ns - nb)))
    score = 0.0
    for m in range(SEL_RATIO):
        for n in range(N_SUB):
            st = N_SUB - 1 + m - n
            score = score + P[..., st:st + SEL_RATIO * ns:SEL_RATIO]
    blk = jnp.arange(ns)[None, :]
    cur = (t_pos // SEL_BLOCK)[:, None]
    valid = blk * SEL_BLOCK <= t_pos[:, None]
    forced = valid & ((blk == 0) | (blk == cur) | (blk == cur - 1))
    score = jnp.where(forced, BIG, jnp.where(valid, score, -BIG))
    _, idx = lax.top_k(score, min(N_SELECT, ns))
    return idx


def selected_attention(q, t_pos, idx, sel_kv):
    key_pos = idx[..., None] * SEL_BLOCK + jnp.arange(SEL_BLOCK)
    b, g, nq, k, sb = key_pos.shape
    mask = (key_pos <= t_pos[None, None, :, None, None]).reshape(b, g, 1, nq, k * sb)
    s = jnp.einsum('bqghd,bgqkld->bghqkl', q, sel_kv[..., 0, :]).astype(jnp.float32) * SCALE
    p = masked_softmax(s.reshape(b, g, GROUP, nq, k * sb), mask)
    p = p.reshape(b, g, GROUP, nq, k, sb).astype(sel_kv.dtype)
    return jnp.einsum('bghqkl,bgqkld->bqghd', p, sel_kv[..., 1, :])


def window_attention(q, t_pos, slab, key_pos):
    s = jnp.einsum('bqghd,bkgd->bghqk', q, slab[:, :, 0]).astype(jnp.float32) * SCALE
    diff = t_pos[:, None] - key_pos[None, :]
    mask = (diff >= 0) & (diff < WINDOW) & (key_pos[None, :] >= 0)
    p = masked_softmax(s, mask)
    return jnp.einsum('bghqk,bkgd->bqghd', p.astype(slab.dtype), slab[:, :, 1])


def gate_mix(gates, o_c, o_s, o_w):
    g = jax.nn.sigmoid(gates.astype(jnp.float32)).astype(o_c.dtype)
    return g[..., 0:1] * o_c + g[..., 1:2] * o_s + g[..., 2:3] * o_w


def nsa_prompt(q, kv_c, kv_s, kv_w, gates, ck):
    b, t = q.shape[0], q.shape[1]
    kcmp = compress_kv(kv_c[:, :, 0], ck[0], ck[1], ck[2])
    vcmp = compress_kv(kv_c[:, :, 1], ck[3], ck[4], ck[5])
    ns = -(-t // SEL_BLOCK)
    kvs = jnp.pad(kv_s, ((0, 0), (0, ns * SEL_BLOCK - t), (0, 0), (0, 0), (0, 0)))
    kvs = kvs.reshape(b, ns, SEL_BLOCK, 2, KV_HEADS, HEAD_DIM).transpose(0, 4, 1, 2, 3, 5)
    kvw = jnp.pad(kv_w, ((0, 0), (WINDOW, 0), (0, 0), (0, 0), (0, 0)))
    bi = jnp.arange(b)[:, None, None, None]
    gi = jnp.arange(KV_HEADS)[None, :, None, None]

    def block(i):
        q0 = i * Q_BLOCK
        t_pos = q0 + jnp.arange(Q_BLOCK)
        qb = lax.dynamic_slice_in_dim(q, q0, Q_BLOCK, axis=1)
        gb = lax.dynamic_slice_in_dim(gates, q0, Q_BLOCK, axis=1)
        o_c, p_c = compressed_attention(qb, t_pos, kcmp, vcmp)
        idx = select_blocks(p_c, t_pos, ns)
        o_s = selected_attention(qb, t_pos, idx, kvs[bi, gi, idx])
        slab = lax.dynamic_slice_in_dim(kvw, q0, WINDOW + Q_BLOCK, axis=1)
        o_w = window_attention(qb, t_pos, slab, q0 - WINDOW + jnp.arange(WINDOW + Q_BLOCK))
        return gate_mix(gb, o_c, o_s, o_w)

    out = lax.map(block, jnp.arange(t // Q_BLOCK))
    return jnp.moveaxis(out, 0, 1).reshape(b, t, ATTN_WIDTH)


def nsa_sample(q, kv_c, kv_s, kv_w, gates, ck, li, pool_c, pool_s, win_buf, page_table):
    b, s_new = q.shape[0], q.shape[1]
    n_pages = page_table.shape[1]
    past = n_pages * PAGE_SIZE
    past_c = pool_c[li, page_table].reshape(b, past, 2, KV_HEADS, HEAD_DIM)
    full_c = jnp.concatenate([past_c.astype(kv_c.dtype), kv_c], axis=1)
    kcmp = compress_kv(full_c[:, :, 0], ck[0], ck[1], ck[2])
    vcmp = compress_kv(full_c[:, :, 1], ck[3], ck[4], ck[5])
    t_pos = past + jnp.arange(s_new)
    o_c, p_c = compressed_attention(q, t_pos, kcmp, vcmp)
    ns = -(-(past + s_new) // SEL_BLOCK)
    idx = select_blocks(p_c, t_pos, ns)
    key_pos = idx[..., None] * SEL_BLOCK + jnp.arange(SEL_BLOCK)
    bi = jnp.arange(b)[:, None, None, None, None]
    gi = jnp.arange(KV_HEADS)[None, :, None, None, None]
    phys = page_table[bi, jnp.clip(key_pos // PAGE_SIZE, 0, n_pages - 1)]
    from_pool = pool_s[li, phys, key_pos % PAGE_SIZE, :, gi]
    from_new = kv_s[bi, jnp.clip(key_pos - past, 0, s_new - 1), :, gi]
    sel = jnp.where((key_pos < past)[..., None, None], from_pool.astype(kv_s.dtype), from_new)
    o_s = selected_attention(q, t_pos, idx, sel)
    w_buf = win_buf.shape[1]
    slab = jnp.concatenate([win_buf.astype(kv_w.dtype), kv_w], axis=1)
    o_w = window_attention(q, t_pos, slab, past - w_buf + jnp.arange(w_buf + s_new))
    out = gate_mix(gates, o_c, o_s, o_w).reshape(b, s_new, ATTN_WIDTH)
    return out, slab[:, -w_buf:]


def conv_module(u, prev, conv_w, conv_b, ln_g, ln_b):
    a = u[..., :CONV_CH] * jax.nn.sigmoid(u[..., CONV_CH:])
    ext = jnp.concatenate([prev.astype(a.dtype), a], axis=1)
    y = lax.conv_general_dilated(ext, conv_w[:, None, :].astype(a.dtype), window_strides=(1,), padding='VALID',
                                 dimension_numbers=('NWC', 'WIO', 'NWC'), feature_group_count=CONV_CH) + conv_b
    y32 = y.astype(jnp.float32)
    mu = jnp.mean(y32, axis=-1, keepdims=True)
    var = jnp.mean(jnp.square(y32 - mu), axis=-1, keepdims=True)
    yn = (y32 - mu) * lax.rsqrt(var + EPS) * ln_g + ln_b
    return jax.nn.silu(yn).astype(a.dtype), ext[:, -(CONV_WIDTH - 1):]


def pool_module(h, prev, first_pos, pool_w, pool_scale):
    n = h.shape[1]
    ext = jnp.concatenate([prev.astype(h.dtype), h], axis=1)
    cs = jnp.pad(jnp.cumsum(ext.astype(jnp.float32), axis=1), ((0, 0), (1, 0), (0, 0)))
    pos = first_pos + jnp.arange(n)
    hf = h.astype(jnp.float32)
    outs = []
    for gidx, w in enumerate(POOL_WINDOWS):
        c0, c1 = gidx * POOL_GROUP, (gidx + 1) * POOL_GROUP
        win_sum = cs[:, POOL_BUF + 1:, c0:c1] - cs[:, POOL_BUF + 1 - w:POOL_BUF + 1 - w + n, c0:c1]
        cnt = jnp.minimum(w, pos + 1).astype(jnp.float32)
        outs.append(win_sum / cnt[None, :, None] - hf[..., c0:c1])
    z = jnp.stack(outs, axis=2).astype(h.dtype)
    y = jnp.einsum('bngc,gce->bnge', z, pool_w).reshape(h.shape) * pool_scale
    return y, ext[:, -POOL_BUF:]


def setup_inputs(seed: int = 0) -> dict:
    key = jax.random.key(seed)
    ks = jax.random.split(key, 32)
    n_pages = PAST_LEN // PAGE_SIZE
    n_used = DEC_BATCH * n_pages
    n_pool = n_used + n_used // 4
    w_buf = min(WINDOW, PAST_LEN)
    f32 = jnp.float32

    def nrm(k, shape, scale=1.0):
        return jax.random.normal(k, shape, f32) * scale

    page_table = jax.random.permutation(ks[7], n_pool)[:n_used].reshape(DEC_BATCH, n_pages).astype(jnp.int32)
    return {
        'x_prompt': nrm(ks[0], (BATCH, SEQ, D_MODEL)),
        'x_sample': nrm(ks[1], (DEC_BATCH, DEC_SEQ, D_MODEL)),
        'cache_cmp_kv': nrm(ks[2], (N_EVEN, n_pool, PAGE_SIZE, 2, KV_HEADS, HEAD_DIM)),
        'cache_slc_kv': nrm(ks[3], (N_EVEN, n_pool, PAGE_SIZE, 2, KV_HEADS, HEAD_DIM)),
        'state_win_kv': nrm(ks[4], (N_EVEN, DEC_BATCH, w_buf, 2, KV_HEADS, HEAD_DIM)),
        'state_conv': nrm(ks[5], (N_EVEN, DEC_BATCH, CONV_WIDTH - 1, CONV_CH), 0.5),
        'state_pool': nrm(ks[6], (N_ODD, DEC_BATCH, POOL_BUF, D_MODEL)),
        'page_table': page_table,
        'norm_mix': 1.0 + nrm(ks[8], (DEPTH, D_MODEL), 0.05),
        'norm_ffn': 1.0 + nrm(ks[9], (DEPTH, D_MODEL), 0.05),
        'norm_final': 1.0 + nrm(ks[10], (D_MODEL,), 0.05),
        'w_in_a': nrm(ks[11], (N_EVEN, D_MODEL, IN_A), D_MODEL ** -0.5),
        'w_out_a': nrm(ks[12], (N_EVEN, MIX_A, D_MODEL), MIX_A ** -0.5),
        'cmp_pos_k': nrm(ks[13], (N_EVEN, CMP_LEN, HEAD_DIM), 0.5),
        'cmp_w1_k': nrm(ks[14], (N_EVEN, CMP_LEN, HEAD_DIM, HEAD_DIM), (CMP_LEN * HEAD_DIM) ** -0.5),
        'cmp_w2_k': nrm(ks[15], (N_EVEN, HEAD_DIM, HEAD_DIM), HEAD_DIM ** -0.5),
        'cmp_pos_v': nrm(ks[16], (N_EVEN, CMP_LEN, HEAD_DIM), 0.5),
        'cmp_w1_v': nrm(ks[17], (N_EVEN, CMP_LEN, HEAD_DIM, HEAD_DIM), (CMP_LEN * HEAD_DIM) ** -0.5),
        'cmp_w2_v': nrm(ks[18], (N_EVEN, HEAD_DIM, HEAD_DIM), HEAD_DIM ** -0.5),
        'conv_w': nrm(ks[19], (N_EVEN, CONV_WIDTH, CONV_CH), CONV_WIDTH ** -0.5),
        'conv_b': nrm(ks[20], (N_EVEN, CONV_CH), 0.02),
        'conv_ln_g': 1.0 + nrm(ks[21], (N_EVEN, CONV_CH), 0.05),
        'conv_ln_b': nrm(ks[22], (N_EVEN, CONV_CH), 0.02),
        'pool_w': nrm(ks[23], (N_ODD, len(POOL_WINDOWS), POOL_GROUP, POOL_GROUP), POOL_GROUP ** -0.5),
        'pool_scale': 0.5 + nrm(ks[24], (N_ODD, D_MODEL), 0.1),
        'w_ffn_gate': nrm(ks[25], (DEPTH, D_MODEL, D_FF), D_MODEL ** -0.5),
        'w_ffn_up': nrm(ks[26], (DEPTH, D_MODEL, D_FF), D_MODEL ** -0.5),
        'w_ffn_down': nrm(ks[27], (DEPTH, D_FF, D_MODEL), D_FF ** -0.5),
    }


def reference(x_prompt, x_sample, cache_cmp_kv, cache_slc_kv, state_win_kv, state_conv, state_pool, page_table,
              norm_mix, norm_ffn, norm_final, w_in_a, w_out_a, cmp_pos_k, cmp_w1_k, cmp_w2_k, cmp_pos_v, cmp_w1_v,
              cmp_w2_v, conv_w, conv_b, conv_ln_g, conv_ln_b, pool_w, pool_scale, w_ffn_gate, w_ffn_up, w_ffn_down):
    xp, xs = x_prompt, x_sample
    bp, tp = xp.shape[0], xp.shape[1]
    past = page_table.shape[1] * PAGE_SIZE
    cmp_p, cmp_s, slc_p, slc_s, win_p, win_s, conv_p, conv_s, pool_p, pool_s = [], [], [], [], [], [], [], [], [], []
    for l in range(DEPTH):
        hp = rmsnorm(xp, norm_mix[l])
        hs = rmsnorm(xs, norm_mix[l])
        if l % 2 == 0:
            i = l // 2
            ck = (cmp_pos_k[i], cmp_w1_k[i], cmp_w2_k[i], cmp_pos_v[i], cmp_w1_v[i], cmp_w2_v[i])
            cw = (conv_w[i], conv_b[i], conv_ln_g[i], conv_ln_b[i])
            q, kvc, kvs, kvw, g, u = split_even_proj(hp, w_in_a[i])
            a_out = nsa_prompt(q, kvc, kvs, kvw, g, ck)
            c_out, c_st = conv_module(u, jnp.zeros((bp, CONV_WIDTH - 1, CONV_CH), hp.dtype), *cw)
            mp = jnp.concatenate([a_out, c_out], axis=-1) @ w_out_a[i]
            cmp_p.append(kvc)
            slc_p.append(kvs)
            win_p.append(kvw[:, -min(WINDOW, tp):])
            conv_p.append(c_st)
            q, kvc, kvs, kvw, g, u = split_even_proj(hs, w_in_a[i])
            a_out, w_st = nsa_sample(q, kvc, kvs, kvw, g, ck, i, cache_cmp_kv, cache_slc_kv, state_win_kv[i], page_table)
            c_out, c_st = conv_module(u, state_conv[i], *cw)
            ms = jnp.concatenate([a_out, c_out], axis=-1) @ w_out_a[i]
            cmp_s.append(kvc)
            slc_s.append(kvs)
            win_s.append(w_st)
            conv_s.append(c_st)
        else:
            j = l // 2
            mp, p_st = pool_module(hp, jnp.zeros((bp, POOL_BUF, D_MODEL), hp.dtype), 0, pool_w[j], pool_scale[j])
            ms, s_st = pool_module(hs, state_pool[j], past, pool_w[j], pool_scale[j])
            pool_p.append(p_st)
            pool_s.append(s_st)
        xp = xp + mp
        xs = xs + ms
        xp = xp + swiglu(rmsnorm(xp, norm_ffn[l]), w_ffn_gate[l], w_ffn_up[l], w_ffn_down[l])
        xs = xs + swiglu(rmsnorm(xs, norm_ffn[l]), w_ffn_gate[l], w_ffn_up[l], w_ffn_down[l])
    y_prompt = rmsnorm(xp, norm_final)
    y_sample = rmsnorm(xs, norm_final)
    return (y_prompt, y_sample, jnp.stack(cmp_p), jnp.stack(cmp_s), jnp.stack(slc_p), jnp.stack(slc_s),
            jnp.stack(win_p), jnp.stack(win_s), jnp.stack(conv_p), jnp.stack(conv_s),
            jnp.stack(pool_p), jnp.stack(pool_s))
```

```python
import functools

import jax
import jax.numpy as jnp
from jax import lax
from jax.experimental import pallas as pl
from jax.experimental.pallas import tpu as pltpu

F32 = jnp.float32
BF16 = jnp.bfloat16

D_MODEL = 1024
N_HEADS = 8
HEAD_DIM = 64
KV_HEADS = 2
GROUP = N_HEADS // KV_HEADS
ATTN_WIDTH = N_HEADS * HEAD_DIM
KV_COLS = 2 * KV_HEADS * HEAD_DIM
CMP_LEN = 32
CMP_STRIDE = 16
N_SUB = CMP_LEN // CMP_STRIDE
SEL_BLOCK = 64
SEL_RATIO = SEL_BLOCK // CMP_STRIDE
N_SELECT = 16
WINDOW = 512
PAGE_SIZE = 128
N_BRANCH = 3
SCALE = HEAD_DIM ** -0.5
CONV_CH = D_MODEL // 2
CONV_WIDTH = 31
POOL_WINDOWS = (2, 4, 8, 16)
POOL_GROUP = D_MODEL // len(POOL_WINDOWS)
POOL_BUF = max(POOL_WINDOWS) - 1
EPS = 1e-6
BIG = 1e9
NEG = -1e30
TINY = float(jnp.finfo(jnp.float32).tiny)

LANE = 128
CHUNK_COLS = CMP_STRIDE * KV_COLS
GATE_PAD = LANE
IN_COLS = ATTN_WIDTH + N_BRANCH * KV_COLS + GATE_PAD + 2 * CONV_CH
VMEM_LIMIT = 56 * 1024 * 1024

Q_TILE = 128
SEL_CHUNK = 256
WIN_KEYS = WINDOW + Q_TILE
CONV_HALO = 32
CONV_ROWS = 32
POOL_HALO = 16


def _dot(a, b):
    return jnp.dot(a, b, preferred_element_type=F32)


def _dot_nt(a, b):
    return lax.dot_general(a, b, (((1,), (1,)), ((), ())), preferred_element_type=F32)


def _dot_exact_lhs(a, b):
    hi = a.astype(BF16)
    r1 = a - hi.astype(F32)
    mid = r1.astype(BF16)
    lo = (r1 - mid.astype(F32)).astype(BF16)
    return _dot(hi, b) + _dot(mid, b) + _dot(lo, b)


def _rms(x, g):
    return x * lax.rsqrt(jnp.mean(x * x, axis=-1, keepdims=True) + EPS) * g


def _softmax_rows(s, mask):
    m = jnp.max(s, axis=-1, keepdims=True)
    e = jnp.where(mask, jnp.exp(s - m), 0.0)
    return e / jnp.maximum(jnp.sum(e, axis=-1, keepdims=True), TINY)


def _params(*sem):
    return pltpu.CompilerParams(dimension_semantics=sem, vmem_limit_bytes=VMEM_LIMIT)


def _const_spec(shape):
    nd = len(shape)
    return pl.BlockSpec(shape, lambda *_: (0,) * nd, pipeline_mode=pl.Buffered(1))


def _inproj_kernel(x_ref, g_ref, w_ref, q_ref, kvc_ref, kvs_ref, kvw_ref, gate_ref, a_ref,
                   ksa_ref, vs_ref, kw_ref, vw_ref, *, tm, seq_len):
    h = _rms(x_ref[...], g_ref[...])
    z = _dot(h.astype(BF16), w_ref[...])
    off = ATTN_WIDTH
    q_ref[...] = (z[:, :off] * SCALE).astype(BF16)
    kvc_ref[...] = z[:, off:off + KV_COLS]
    kvs = z[:, off + KV_COLS:off + 2 * KV_COLS]
    kvw = z[:, off + 2 * KV_COLS:off + 3 * KV_COLS]
    kvs_ref[...] = kvs
    kvw_ref[...] = kvw
    off += 3 * KV_COLS
    gate_ref[...] = jax.nn.sigmoid(z[:, off:off + GATE_PAD])
    off += GATE_PAD
    a_ref[...] = z[:, off:off + CONV_CH] * jax.nn.sigmoid(z[:, off + CONV_CH:])
    pos = (pl.program_id(0) * tm + lax.broadcasted_iota(jnp.int32, (tm, HEAD_DIM), 0)) % seq_len
    onehot = (pos // SEL_BLOCK == lax.broadcasted_iota(jnp.int32, (tm, HEAD_DIM), 1)).astype(BF16)
    for g in range(KV_HEADS):
        k0, v0 = g * HEAD_DIM, (KV_HEADS + g) * HEAD_DIM
        ksa_ref[g] = jnp.concatenate([kvs[:, k0:k0 + HEAD_DIM].astype(BF16), onehot], axis=1)
        vs_ref[g] = kvs[:, v0:v0 + HEAD_DIM].astype(BF16)
        kw_ref[g] = kvw[:, k0:k0 + HEAD_DIM].astype(BF16)
        vw_ref[g] = kvw[:, v0:v0 + HEAD_DIM].astype(BF16)


def _inproj(x, g, w, *, tm, seq_len):
    m = x.shape[0]
    row = lambda c: pl.BlockSpec((tm, c), lambda i: (i, 0))
    grp = lambda c: pl.BlockSpec((KV_HEADS, tm, c), lambda i: (0, i, 0))
    out_shape = (
        jax.ShapeDtypeStruct((m, ATTN_WIDTH), BF16),
        jax.ShapeDtypeStruct((m, KV_COLS), F32), jax.ShapeDtypeStruct((m, KV_COLS), F32),
        jax.ShapeDtypeStruct((m, KV_COLS), F32),
        jax.ShapeDtypeStruct((m, GATE_PAD), F32), jax.ShapeDtypeStruct((m, CONV_CH), F32),
        jax.ShapeDtypeStruct((KV_HEADS, m, 2 * HEAD_DIM), BF16),
        jax.ShapeDtypeStruct((KV_HEADS, m, HEAD_DIM), BF16),
        jax.ShapeDtypeStruct((KV_HEADS, m, HEAD_DIM), BF16),
        jax.ShapeDtypeStruct((KV_HEADS, m, HEAD_DIM), BF16),
    )
    out_specs = (row(ATTN_WIDTH), row(KV_COLS), row(KV_COLS), row(KV_COLS), row(GATE_PAD), row(CONV_CH),
                 grp(2 * HEAD_DIM), grp(HEAD_DIM), grp(HEAD_DIM), grp(HEAD_DIM))
    return pl.pallas_call(
        functools.partial(_inproj_kernel, tm=tm, seq_len=seq_len),
        grid=(m // tm,),
        in_specs=[row(D_MODEL), _const_spec((1, D_MODEL)), _const_spec((D_MODEL, IN_COLS))],
        out_specs=out_specs, out_shape=out_shape,
        compiler_params=_params("parallel"), name="inproj",
    )(x, g, w)


def _conv_ln_silu(y, b, lg, lb):
    y = y + b
    mu = jnp.mean(y, axis=-1, keepdims=True)
    var = jnp.mean(jnp.square(y - mu), axis=-1, keepdims=True)
    return jax.nn.silu((y - mu) * lax.rsqrt(var + EPS) * lg + lb)


def _conv_kernel(prev_ref, a_ref, w_ref, b_ref, lg_ref, lb_ref, o_ref, ext_ref, *, tc):
    first = pl.program_id(1) == 0
    ext_ref[0:CONV_HALO, :] = jnp.where(first, 0.0, prev_ref[0])
    ext_ref[CONV_HALO:CONV_HALO + tc, :] = a_ref[0]
    lead = CONV_HALO - (CONV_WIDTH - 1)

    for r0 in range(0, tc, CONV_ROWS):
        acc = jnp.zeros((CONV_ROWS, CONV_CH), F32)
        for k in range(CONV_WIDTH):
            acc = acc + w_ref[k:k + 1, :] * ext_ref[r0 + lead + k:r0 + lead + k + CONV_ROWS, :]
        o_ref[0, r0:r0 + CONV_ROWS, :] = _conv_ln_silu(acc, b_ref[...], lg_ref[...], lb_ref[...]).astype(BF16)


def _conv_prompt(a, w, b, lg, lb, *, tc):
    bsz, t, _ = a.shape
    hb = tc // CONV_HALO
    return pl.pallas_call(
        functools.partial(_conv_kernel, tc=tc),
        grid=(bsz, t // tc),
        in_specs=[pl.BlockSpec((1, CONV_HALO, CONV_CH), lambda bi, i: (bi, jnp.maximum(i * hb - 1, 0), 0)),
                  pl.BlockSpec((1, tc, CONV_CH), lambda bi, i: (bi, i, 0)),
                  _const_spec((CONV_WIDTH, CONV_CH)), _const_spec((1, CONV_CH)),
                  _const_spec((1, CONV_CH)), _const_spec((1, CONV_CH))],
        out_specs=pl.BlockSpec((1, tc, CONV_CH), lambda bi, i: (bi, i, 0)),
        out_shape=jax.ShapeDtypeStruct((bsz, t, CONV_CH), BF16),
        scratch_shapes=[pltpu.VMEM((CONV_HALO + tc, CONV_CH), F32)],
        compiler_params=_params("parallel", "parallel"), name="conv_prompt",
    )(a, a, w, b, lg, lb)


def _compress_rows(h0, h1_next, pos_term, w2):
    hid = pos_term + h0 + h1_next
    return _dot(jax.nn.gelu(hid).astype(BF16), w2)


def _pos_term(posx_ref, wc_ref):
    hp = _dot(posx_ref[...], wc_ref[...])
    return hp[0:1, :KV_COLS] + hp[1:2, KV_COLS:]


def _compress_kernel(x_ref, wc_ref, posx_ref, w2_ref, kc_ref, vc_ref):
    hh = _dot(x_ref[0].astype(BF16), wc_ref[...])
    h1 = hh[:, KV_COLS:]
    h1_next = jnp.concatenate([h1[1:], jnp.zeros((1, KV_COLS), F32)], axis=0)
    cmp = _compress_rows(hh[:, :KV_COLS], h1_next, _pos_term(posx_ref, wc_ref), w2_ref[...])
    for g in range(KV_HEADS):
        kc_ref[0, g] = cmp[:, g * HEAD_DIM:(g + 1) * HEAD_DIM].astype(BF16)
        vc_ref[0, g] = cmp[:, (KV_HEADS + g) * HEAD_DIM:(KV_HEADS + g + 1) * HEAD_DIM].astype(BF16)


def _compress_prompt(kvc_chunks, wc, posx, w2):
    bsz, n_ch, _ = kvc_chunks.shape
    tok = jax.ShapeDtypeStruct((bsz, KV_HEADS, n_ch, HEAD_DIM), BF16)
    tok_spec = pl.BlockSpec((1, KV_HEADS, n_ch, HEAD_DIM), lambda bi: (bi, 0, 0, 0))
    return pl.pallas_call(
        _compress_kernel,
        grid=(bsz,),
        in_specs=[pl.BlockSpec((1, n_ch, CHUNK_COLS), lambda bi: (bi, 0, 0)),
                  _const_spec(wc.shape), _const_spec(posx.shape), _const_spec(w2.shape)],
        out_specs=(tok_spec, tok_spec), out_shape=(tok, tok),
        compiler_params=_params("parallel"), name="compress_prompt",
    )(kvc_chunks, wc, posx, w2)


def _block_scores(imp, ssel, t_col, ns):
    s = _dot_exact_lhs(imp, ssel)
    j = lax.broadcasted_iota(jnp.int32, s.shape, 1)
    cur = t_col // SEL_BLOCK
    valid = j * SEL_BLOCK <= t_col
    forced = valid & ((j == 0) | (j == cur) | (j == cur - 1))
    s = jnp.where(forced, BIG, jnp.where(valid, s, -BIG))
    return jnp.where(j < ns, s, -3.0 * BIG)


def _nsa_prompt_kernel(q_ref, gate_ref, ksa_ref, vs_ref, kw_ref, vw_ref, kc_ref, vc_ref, ssel_ref, o_ref,
                       *, tq, ns):
    i = pl.program_id(1)
    q0 = i * tq
    t_col = q0 + lax.broadcasted_iota(jnp.int32, (tq, 1), 0)
    t4 = jnp.concatenate([t_col] * GROUP, axis=0)
    gates = gate_ref[...]
    n_blk = kc_ref.shape[2]
    for g in range(KV_HEADS):
        q4 = jnp.concatenate(
            [q_ref[:, (GROUP * g + h) * HEAD_DIM:(GROUP * g + h + 1) * HEAD_DIM] for h in range(GROUP)], axis=0)

        sc = _dot_nt(q4, kc_ref[0, g])
        end = lax.broadcasted_iota(jnp.int32, (1, n_blk), 1) * CMP_STRIDE + (CMP_LEN - 1)
        cmask = end <= t4
        p_c = _softmax_rows(jnp.where(cmask, sc, NEG), cmask)
        o_c = _dot(p_c.astype(BF16), vc_ref[0, g])
        imp = p_c[0:tq] + p_c[tq:2 * tq] + p_c[2 * tq:3 * tq] + p_c[3 * tq:4 * tq]

        s_t = _block_scores(imp, ssel_ref[...], t_col, ns).T
        jrow = lax.broadcasted_iota(jnp.int32, (LANE, tq), 0)
        rank = jnp.zeros((LANE, tq), F32)
        for i2 in range(ns):
            row = s_t[i2:i2 + 1, :]
            rank = rank + jnp.where((row > s_t) | ((row == s_t) & (jrow > i2)), 1.0, 0.0)
        sel = jnp.where(rank < N_SELECT, 0.0, NEG).T
        selbias = jnp.concatenate([sel[:, :HEAD_DIM].astype(BF16)] * GROUP, axis=0)
        q_aug = jnp.concatenate([q4, selbias], axis=1)

        def chunk(c, carry, diagonal):
            m, l, acc = carry
            k0 = pl.multiple_of(c * SEL_CHUNK, SEL_CHUNK)
            s = _dot_nt(q_aug, ksa_ref[g, pl.ds(k0, SEL_CHUNK), :])
            if diagonal:
                key = k0 + lax.broadcasted_iota(jnp.int32, (1, SEL_CHUNK), 1)
                s = jnp.where(key <= t4, s, NEG)
            m_new = jnp.maximum(m, jnp.max(s, axis=-1, keepdims=True))
            alpha = jnp.exp(m - m_new)
            p = jnp.exp(s - m_new)
            l = alpha * l + jnp.sum(p, axis=-1, keepdims=True)
            acc = alpha * acc + _dot(p.astype(BF16), vs_ref[g, pl.ds(k0, SEL_CHUNK), :])
            return m_new, l, acc

        init = (jnp.full((GROUP * tq, 1), NEG, F32), jnp.zeros((GROUP * tq, 1), F32),
                jnp.zeros((GROUP * tq, HEAD_DIM), F32))
        last = q0 // SEL_CHUNK
        carry = lax.fori_loop(0, last, functools.partial(chunk, diagonal=False), init)
        _, l_s, acc_s = chunk(last, carry, True)
        o_s = acc_s / l_s

        w0 = pl.multiple_of(jnp.maximum(q0 - WINDOW, 0), Q_TILE)
        sw = _dot_nt(q4, kw_ref[g, pl.ds(w0, WIN_KEYS), :])
        diff = t4 - (w0 + lax.broadcasted_iota(jnp.int32, (1, WIN_KEYS), 1))
        wmask = (diff >= 0) & (diff < WINDOW)
        p_w = _softmax_rows(jnp.where(wmask, sw, NEG), wmask)
        o_w = _dot(p_w.astype(BF16), vw_ref[g, pl.ds(w0, WIN_KEYS), :])

        for h in range(GROUP):
            hh = GROUP * g + h
            r = slice(h * tq, (h + 1) * tq)
            c = N_BRANCH * hh
            o = gates[:, c:c + 1] * o_c[r] + gates[:, c + 1:c + 2] * o_s[r] + gates[:, c + 2:c + 3] * o_w[r]
            o_ref[:, hh * HEAD_DIM:(hh + 1) * HEAD_DIM] = o.astype(BF16)


def _nsa_prompt(q, gates, ksa, vs, kw, vw, kc, vc, ssel, *, bsz, seq_len):
    tq = Q_TILE
    nt = seq_len // tq
    ns = seq_len // SEL_BLOCK
    n_blk = kc.shape[2]
    row = lambda c: pl.BlockSpec((tq, c), lambda bi, i: (bi * nt + i, 0))
    seq = lambda c: pl.BlockSpec((KV_HEADS, seq_len, c), lambda bi, i: (0, bi, 0))
    tok = pl.BlockSpec((1, KV_HEADS, n_blk, HEAD_DIM), lambda bi, i: (bi, 0, 0, 0))
    return pl.pallas_call(
        functools.partial(_nsa_prompt_kernel, tq=tq, ns=ns),
        grid=(bsz, nt),
        in_specs=[row(ATTN_WIDTH), row(GATE_PAD), seq(2 * HEAD_DIM), seq(HEAD_DIM), seq(HEAD_DIM), seq(HEAD_DIM),
                  tok, tok, _const_spec(ssel.shape)],
        out_specs=row(ATTN_WIDTH),
        out_shape=jax.ShapeDtypeStruct((bsz * seq_len, ATTN_WIDTH), BF16),
        compiler_params=_params("parallel", "parallel"), name="nsa_prompt",
    )(q, gates, ksa, vs, kw, vw, kc, vc, ssel)


def _swiglu_residual(x1, gf, wg_ref, wu_ref, wd_ref):
    h = _rms(x1, gf).astype(BF16)
    act = jax.nn.silu(_dot(h, wg_ref[...])) * _dot(h, wu_ref[...])
    return x1 + _dot(act.astype(BF16), wd_ref[...])


def _ffn0_kernel(x_ref, a_ref, c_ref, woa_ref, woc_ref, gf_ref, wg_ref, wu_ref, wd_ref, o_ref):
    x1 = x_ref[...] + (_dot(a_ref[...], woa_ref[...]) + _dot(c_ref[...], woc_ref[...]))
    o_ref[...] = _swiglu_residual(x1, gf_ref[...], wg_ref, wu_ref, wd_ref)


def _ffn0(x, a, c, woa, woc, gf, wg, wu, wd, *, tm):
    m = x.shape[0]
    row = lambda cols: pl.BlockSpec((tm, cols), lambda i: (i, 0))
    return pl.pallas_call(
        _ffn0_kernel,
        grid=(m // tm,),
        in_specs=[row(D_MODEL), row(ATTN_WIDTH), row(CONV_CH), _const_spec(woa.shape), _const_spec(woc.shape),
                  _const_spec(gf.shape), _const_spec(wg.shape), _const_spec(wu.shape), _const_spec(wd.shape)],
        out_specs=row(D_MODEL), out_shape=jax.ShapeDtypeStruct((m, D_MODEL), F32),
        compiler_params=_params("parallel"), name="outproj_ffn",
    )(x, a, c, woa, woc, gf, wg, wu, wd)


def _pool_mix(x, h, win_sums, cnts, pw_ref, ps):
    ys = []
    for g in range(len(POOL_WINDOWS)):
        z = win_sums[g] / cnts[g] - h[:, g * POOL_GROUP:(g + 1) * POOL_GROUP]
        ys.append(_dot(z.astype(BF16), pw_ref[g]))
    return x + jnp.concatenate(ys, axis=1) * ps


def _layer1_prompt_kernel(xprev_ref, x_ref, gm_ref, pw_ref, ps_ref, gf_ref, wg_ref, wu_ref, wd_ref, gfin_ref,
                          y_ref, hst_ref, *, tm):
    i = pl.program_id(1)
    x = x_ref[0]
    h = _rms(x, gm_ref[...])
    hprev = jnp.where(i == 0, 0.0, _rms(xprev_ref[0], gm_ref[...]))
    hst_ref[0] = h[tm - POOL_HALO:, :]
    pos = i * tm + lax.broadcasted_iota(jnp.int32, (tm, 1), 0)
    sums, cnts = [], []
    for g, w in enumerate(POOL_WINDOWS):
        e = jnp.concatenate([hprev[:, g * POOL_GROUP:(g + 1) * POOL_GROUP],
                             h[:, g * POOL_GROUP:(g + 1) * POOL_GROUP]], axis=0)
        span = 1
        while span < w:
            e = e[span:] + e[:-span]
            span *= 2
        first = POOL_HALO - (w - 1)
        sums.append(e[first:first + tm])
        cnts.append(jnp.minimum(w, pos + 1).astype(F32))
    x1 = _pool_mix(x, h, sums, cnts, pw_ref, ps_ref[...])
    x2 = _swiglu_residual(x1, gf_ref[...], wg_ref, wu_ref, wd_ref)
    y_ref[0] = _rms(x2, gfin_ref[...])


def _layer1_prompt(x, gm, pw, ps, gf, wg, wu, wd, gfin, *, tm):
    bsz, t, _ = x.shape
    hb = tm // POOL_HALO
    return pl.pallas_call(
        functools.partial(_layer1_prompt_kernel, tm=tm),
        grid=(bsz, t // tm),
        in_specs=[pl.BlockSpec((1, POOL_HALO, D_MODEL), lambda bi, i: (bi, jnp.maximum(i * hb - 1, 0), 0)),
                  pl.BlockSpec((1, tm, D_MODEL), lambda bi, i: (bi, i, 0)),
                  _const_spec(gm.shape), _const_spec(pw.shape), _const_spec(ps.shape), _const_spec(gf.shape),
                  _const_spec(wg.shape), _const_spec(wu.shape), _const_spec(wd.shape), _const_spec(gfin.shape)],
        out_specs=(pl.BlockSpec((1, tm, D_MODEL), lambda bi, i: (bi, i, 0)),
                   pl.BlockSpec((1, POOL_HALO, D_MODEL), lambda bi, i: (bi, 0, 0))),
        out_shape=(jax.ShapeDtypeStruct((bsz, t, D_MODEL), F32),
                   jax.ShapeDtypeStruct((bsz, POOL_HALO, D_MODEL), F32)),
        compiler_params=_params("parallel", "arbitrary"), name="layer1_prompt",
    )(x, x, gm, pw, ps, gf, wg, wu, wd, gfin)


def _layer1_sample_kernel(x_ref, hist_ref, gm_ref, pw_ref, ps_ref, gf_ref, wg_ref, wu_ref, wd_ref, gfin_ref,
                          y_ref, h_ref, *, first_pos):
    x = x_ref[...]
    h = _rms(x, gm_ref[...])
    h_ref[...] = h
    sums, cnts = [], []
    for g, w in enumerate(POOL_WINDOWS):
        c = slice(g * POOL_GROUP, (g + 1) * POOL_GROUP)
        s = h[:, c]
        for k in range(1, w):
            s = s + hist_ref[POOL_BUF - k][:, c]
        sums.append(s)
        cnts.append(float(min(w, first_pos + 1)))
    x1 = _pool_mix(x, h, sums, cnts, pw_ref, ps_ref[...])
    x2 = _swiglu_residual(x1, gf_ref[...], wg_ref, wu_ref, wd_ref)
    y_ref[...] = _rms(x2, gfin_ref[...])


def _layer1_sample(x, hist, gm, pw, ps, gf, wg, wu, wd, gfin, *, first_pos):
    m = x.shape[0]
    args = (x, hist, gm, pw, ps, gf, wg, wu, wd, gfin)
    out = jax.ShapeDtypeStruct((m, D_MODEL), F32)
    return pl.pallas_call(
        functools.partial(_layer1_sample_kernel, first_pos=first_pos),
        grid=(1,),
        in_specs=[_const_spec(a.shape) for a in args],
        out_specs=(_const_spec((m, D_MODEL)), _const_spec((m, D_MODEL))), out_shape=(out, out),
        compiler_params=_params("arbitrary"), name="layer1_sample",
    )(*args)


def _page_copy(cache_ref, xbuf_ref, sem, phys, p):
    rows = PAGE_SIZE // CMP_STRIDE
    return pltpu.make_async_copy(cache_ref.at[phys], xbuf_ref.at[pl.ds(p * rows, rows)], sem)


def _cmp_sample_kernel(pt_ref, q_ref, cache_ref, wc_ref, posx_ref, w2_ref, ssel_ref, oc_ref, idx_ref,
                       xbuf_ref, hbuf_ref, sem, *, n_pages, past, ns, nsp):
    b = pl.program_id(0)
    n_ch = n_pages * (PAGE_SIZE // CMP_STRIDE)

    def start(p, carry):
        _page_copy(cache_ref, xbuf_ref, sem, pt_ref[b * n_pages + p], p).start()
        return carry

    def wait(p, carry):
        _page_copy(cache_ref, xbuf_ref, sem, 0, p).wait()
        return carry

    lax.fori_loop(0, n_pages, start, 0)
    lax.fori_loop(0, n_pages, wait, 0)

    rows = PAGE_SIZE

    def first_layer(r, carry):
        r0 = pl.multiple_of(r * rows, rows)
        hbuf_ref[pl.ds(r0, rows), :] = _dot(xbuf_ref[pl.ds(r0, rows), :].astype(BF16), wc_ref[...])
        return carry

    lax.fori_loop(0, n_ch // rows, first_layer, 0)
    hbuf_ref[n_ch:n_ch + 8, :] = jnp.zeros((8, 2 * KV_COLS), F32)
    cmp = _compress_rows(hbuf_ref[0:n_ch, :KV_COLS], hbuf_ref[pl.ds(1, n_ch), KV_COLS:],
                         _pos_term(posx_ref, wc_ref), w2_ref[...])

    q8 = q_ref[0]
    head = lax.broadcasted_iota(jnp.int32, (N_HEADS, 1), 0)
    end = lax.broadcasted_iota(jnp.int32, (1, n_ch), 1) * CMP_STRIDE + (CMP_LEN - 1)
    cmask = jnp.broadcast_to(end <= past, (N_HEADS, n_ch))
    o_c = jnp.zeros((N_HEADS, HEAD_DIM), F32)
    jl = lax.broadcasted_iota(jnp.int32, (1, nsp), 1)
    ii = lax.broadcasted_iota(jnp.int32, (nsp, nsp), 0)
    jj = lax.broadcasted_iota(jnp.int32, (nsp, nsp), 1)
    cur = past // SEL_BLOCK
    t_col = jnp.full((N_HEADS, 1), past, jnp.int32)
    for g in range(KV_HEADS):
        in_group = head // GROUP == g
        kc = cmp[:, g * HEAD_DIM:(g + 1) * HEAD_DIM].astype(BF16)
        vc = cmp[:, (KV_HEADS + g) * HEAD_DIM:(KV_HEADS + g + 1) * HEAD_DIM].astype(BF16)
        sc = _dot_nt(q8, kc)
        p_c = _softmax_rows(jnp.where(cmask, sc, NEG), cmask)
        o_c = jnp.where(in_group, _dot(p_c.astype(BF16), vc), o_c)
        imp = jnp.sum(jnp.where(in_group, p_c, 0.0), axis=0, keepdims=True)
        s_b = _block_scores(jnp.broadcast_to(imp, (N_HEADS, n_ch)), ssel_ref[...], t_col, ns)
        s_b = jnp.broadcast_to(s_b[0:1], (nsp, nsp))
        s_t = s_b.T
        beats = (s_t > s_b) | ((s_t == s_b) & (ii < jj))
        rank = jnp.sum(jnp.where(beats, 1.0, 0.0), axis=0, keepdims=True)
        sel = (rank < N_SELECT) & (jl != cur)
        sel_f = jnp.where(sel, 1.0, 0.0)
        before = _dot(jnp.broadcast_to(sel_f, (N_HEADS, nsp)).astype(BF16),
                      jnp.where(ii < jj, 1.0, 0.0).astype(BF16))[0:1]
        slot = lax.broadcasted_iota(jnp.int32, (N_SELECT, nsp), 0)
        pick = jnp.broadcast_to(sel, (N_SELECT, nsp)) & (jnp.broadcast_to(before, (N_SELECT, nsp)) == slot.astype(F32))
        blk = jnp.sum(jnp.where(pick, lax.broadcasted_iota(jnp.int32, (N_SELECT, nsp), 1), 0), axis=1, keepdims=True)
        idx_ref[0, g] = blk
    oc_ref[0] = o_c


def _cmp_sample(page_table, q8, cache_chunks, wc, posx, w2, ssel, *, past, ns):
    bsz, n_pages = page_table.shape
    n_ch = n_pages * (PAGE_SIZE // CMP_STRIDE)
    nsp = ssel.shape[1]
    grid_spec = pltpu.PrefetchScalarGridSpec(
        num_scalar_prefetch=1, grid=(bsz,),
        in_specs=[pl.BlockSpec((1, N_HEADS, HEAD_DIM), lambda b, pt: (b, 0, 0)),
                  pl.BlockSpec(memory_space=pl.ANY),
                  pl.BlockSpec(wc.shape, lambda b, pt: (0, 0)), pl.BlockSpec(posx.shape, lambda b, pt: (0, 0)),
                  pl.BlockSpec(w2.shape, lambda b, pt: (0, 0)), pl.BlockSpec(ssel.shape, lambda b, pt: (0, 0))],
        out_specs=(pl.BlockSpec((1, N_HEADS, HEAD_DIM), lambda b, pt: (b, 0, 0)),
                   pl.BlockSpec((1, KV_HEADS, N_SELECT, 1), lambda b, pt: (b, 0, 0, 0))),
        scratch_shapes=[pltpu.VMEM((n_ch, CHUNK_COLS), F32), pltpu.VMEM((n_ch + 8, 2 * KV_COLS), F32),
                        pltpu.SemaphoreType.DMA(())],
    )
    return pl.pallas_call(
        functools.partial(_cmp_sample_kernel, n_pages=n_pages, past=past, ns=ns, nsp=nsp),
        grid_spec=grid_spec,
        out_shape=(jax.ShapeDtypeStruct((bsz, N_HEADS, HEAD_DIM), F32),
                   jax.ShapeDtypeStruct((bsz, KV_HEADS, N_SELECT, 1), jnp.int32)),
        compiler_params=_params("arbitrary"), name="cmp_sample",
    )(page_table.reshape(-1), q8, cache_chunks, wc, posx, w2, ssel)


def _slab_copy(cache_ref, sbuf_ref, sem, src, slot):
    return pltpu.make_async_copy(cache_ref.at[src], sbuf_ref.at[slot], sem)


def _mix_sample_kernel(idx_ref, pt_ref, q_ref, gate_ref, oc_ref, kvs_ref, kvw_ref, win_ref, cst_ref, a_ref,
                       cw_ref, cb_ref, lg_ref, lb_ref, cache_ref, att_ref, conv_ref, sbuf_ref, sem,
                       *, n_pages, n_gather, past, w_buf):
    b = pl.program_id(0)
    per_page = PAGE_SIZE // SEL_BLOCK
    n_slab = KV_HEADS * n_gather

    def start(s, carry):
        g, k = s // n_gather, s % n_gather
        blk = idx_ref[(b * KV_HEADS + g) * N_SELECT + k]
        phys = pt_ref[b * n_pages + blk // per_page]
        _slab_copy(cache_ref, sbuf_ref, sem, phys * per_page + blk % per_page, s).start()
        return carry

    def wait(s, carry):
        _slab_copy(cache_ref, sbuf_ref, sem, 0, s).wait()
        return carry

    lax.fori_loop(0, n_slab, start, 0)

    y = (jnp.sum(cw_ref[0:CONV_WIDTH - 1, :] * cst_ref[0], axis=0, keepdims=True)
         + cw_ref[CONV_WIDTH - 1:CONV_WIDTH, :] * a_ref[0])
    conv_ref[0] = _conv_ln_silu(y, cb_ref[...], lg_ref[...], lb_ref[...])

    lax.fori_loop(0, n_slab, wait, 0)

    q8 = q_ref[0]
    q8f = q8.astype(F32)
    head = lax.broadcasted_iota(jnp.int32, (N_HEADS, 1), 0)
    gates = gate_ref[0]
    kvs_new, kvw_new = kvs_ref[0], kvw_ref[0]
    jw = lax.broadcasted_iota(jnp.int32, (1, w_buf), 1)
    wdiff = w_buf - jw
    wmask = jnp.broadcast_to((wdiff < WINDOW) & (past - wdiff >= 0), (N_HEADS, w_buf))
    out = jnp.zeros((N_HEADS, HEAD_DIM), F32)

    def with_new_row(s, mask, vals, k_new, v_new):
        s_new = jnp.sum(q8f * k_new.astype(BF16).astype(F32), axis=-1, keepdims=True)
        s = jnp.where(mask, s, NEG)
        m = jnp.maximum(jnp.max(s, axis=-1, keepdims=True), s_new)
        e = jnp.where(mask, jnp.exp(s - m), 0.0)
        e_new = jnp.exp(s_new - m)
        num = _dot(e.astype(BF16), vals.astype(BF16)) + e_new * v_new
        return num / (jnp.sum(e, axis=-1, keepdims=True) + e_new)

    for g in range(KV_HEADS):
        kc, vc = slice(g * HEAD_DIM, (g + 1) * HEAD_DIM), slice((KV_HEADS + g) * HEAD_DIM, (KV_HEADS + g + 1) * HEAD_DIM)
        slab = sbuf_ref[g * n_gather:(g + 1) * n_gather].reshape(n_gather * SEL_BLOCK, KV_COLS)
        s_s = _dot_nt(q8, slab[:, kc].astype(BF16))
        o_s = with_new_row(s_s, jnp.full(s_s.shape, True), slab[:, vc], kvs_new[:, kc], kvs_new[:, vc])
        win = win_ref[0]
        s_w = _dot_nt(q8, win[:, kc].astype(BF16))
        o_w = with_new_row(s_w, wmask, win[:, vc], kvw_new[:, kc], kvw_new[:, vc])
        mixed = gates[:, 0:1] * oc_ref[0] + gates[:, 1:2] * o_s + gates[:, 2:3] * o_w
        out = jnp.where(head // GROUP == g, mixed, out)
    att_ref[0] = out


def _mix_sample(idx, page_table, q8, gates8, o_c, kvs, kvw, win, conv_state, a, cw, cb, lg, lb, cache_slabs,
                *, past, n_gather):
    bsz, n_pages = page_table.shape
    w_buf = win.shape[1]
    one = lambda *shape: pl.BlockSpec((1,) + shape, lambda b, *_: (b,) + (0,) * len(shape))
    const = lambda shape: pl.BlockSpec(shape, lambda b, *_: (0,) * len(shape))
    grid_spec = pltpu.PrefetchScalarGridSpec(
        num_scalar_prefetch=2, grid=(bsz,),
        in_specs=[one(N_HEADS, HEAD_DIM), one(N_HEADS, N_BRANCH), one(N_HEADS, HEAD_DIM), one(1, KV_COLS),
                  one(1, KV_COLS), one(w_buf, KV_COLS), one(CONV_WIDTH - 1, CONV_CH), one(1, CONV_CH),
                  const(cw.shape), const(cb.shape), const(lg.shape), const(lb.shape),
                  pl.BlockSpec(memory_space=pl.ANY)],
        out_specs=(one(N_HEADS, HEAD_DIM), one(1, CONV_CH)),
        scratch_shapes=[pltpu.VMEM((KV_HEADS * n_gather, SEL_BLOCK, KV_COLS), F32), pltpu.SemaphoreType.DMA(())],
    )
    return pl.pallas_call(
        functools.partial(_mix_sample_kernel, n_pages=n_pages, n_gather=n_gather, past=past, w_buf=w_buf),
        grid_spec=grid_spec,
        out_shape=(jax.ShapeDtypeStruct((bsz, N_HEADS, HEAD_DIM), F32),
                   jax.ShapeDtypeStruct((bsz, 1, CONV_CH), F32)),
        compiler_params=_params("arbitrary"), name="mix_sample",
    )(idx.reshape(-1), page_table.reshape(-1), q8, gates8, o_c, kvs, kvw, win, conv_state, a, cw, cb, lg, lb,
      cache_slabs)


def _inproj_weight(w_in):
    off = ATTN_WIDTH + N_BRANCH * KV_COLS
    n_gate = N_HEADS * N_BRANCH
    gate = jnp.pad(w_in[:, off:off + n_gate], ((0, 0), (0, GATE_PAD - n_gate)))
    return jnp.concatenate([w_in[:, :off], gate, w_in[:, off + n_gate:]], axis=1).astype(BF16)


def _compress_weights(pos_k, w1_k, w2_k, pos_v, w1_v, w2_v):
    n_slot = 2 * KV_HEADS
    eye = jnp.eye(n_slot, dtype=F32)

    def place(per_slot):
        w = jnp.stack(per_slot, axis=0)
        full = jnp.einsum('jclde,jk->ljdcke', w, eye)
        return full.reshape(CHUNK_COLS, N_SUB * KV_COLS)

    w1k = w1_k.reshape(N_SUB, CMP_STRIDE, HEAD_DIM, HEAD_DIM)
    w1v = w1_v.reshape(N_SUB, CMP_STRIDE, HEAD_DIM, HEAD_DIM)
    wc = place([w1k, w1k, w1v, w1v]).astype(BF16)
    pk = pos_k.reshape(N_SUB, CMP_STRIDE, 1, HEAD_DIM)
    pv = pos_v.reshape(N_SUB, CMP_STRIDE, 1, HEAD_DIM)
    posx = jnp.concatenate([pk, pk, pv, pv], axis=2).reshape(N_SUB, CHUNK_COLS)
    posx = jnp.pad(posx, ((0, 8 - N_SUB), (0, 0))).astype(BF16)
    w2 = jnp.einsum('jef,jk->jekf', jnp.stack([w2_k, w2_k, w2_v, w2_v]), eye).reshape(KV_COLS, KV_COLS).astype(BF16)
    return wc, posx, w2


def _selection_matrix(n_rows, n_cols):
    n = jnp.arange(n_rows)[:, None]
    j = jnp.arange(n_cols)[None, :]
    cnt = jnp.zeros((n_rows, n_cols), F32)
    for m in range(SEL_RATIO):
        for sub in range(N_SUB):
            cnt = cnt + (SEL_RATIO * j + m - sub == n).astype(F32)
    return cnt.astype(BF16)


def kernel(x_prompt, x_sample, cache_cmp_kv, cache_slc_kv, state_win_kv, state_conv, state_pool, page_table,
           norm_mix, norm_ffn, norm_final, w_in_a, w_out_a, cmp_pos_k, cmp_w1_k, cmp_w2_k, cmp_pos_v, cmp_w1_v,
           cmp_w2_v, conv_w, conv_b, conv_ln_g, conv_ln_b, pool_w, pool_scale, w_ffn_gate, w_ffn_up, w_ffn_down):
    bp, tp, _ = x_prompt.shape
    bs, s_new, _ = x_sample.shape
    n_pages = page_table.shape[1]
    past = n_pages * PAGE_SIZE
    w_buf = state_win_kv.shape[2]
    n_pool = cache_cmp_kv.shape[1]
    ns_p = tp // SEL_BLOCK
    ns_s = -(-(past + s_new) // SEL_BLOCK)
    assert s_new == 1 and tp % SEL_CHUNK == 0 and tp >= WIN_KEYS and N_SELECT <= ns_p <= HEAD_DIM
    assert ns_s > N_SELECT and norm_mix.shape[0] == 2

    row = lambda v: v.reshape(1, -1)
    w_in = _inproj_weight(w_in_a[0])
    wc, posx, w2 = _compress_weights(cmp_pos_k[0], cmp_w1_k[0], cmp_w2_k[0], cmp_pos_v[0], cmp_w1_v[0], cmp_w2_v[0])
    woa, woc = w_out_a[0, :ATTN_WIDTH].astype(BF16), w_out_a[0, ATTN_WIDTH:].astype(BF16)
    wg, wu, wd = w_ffn_gate.astype(BF16), w_ffn_up.astype(BF16), w_ffn_down.astype(BF16)
    cw, cb, lg, lb = conv_w[0], row(conv_b[0]), row(conv_ln_g[0]), row(conv_ln_b[0])
    pw, ps = pool_w[0].astype(BF16), row(pool_scale[0])

    m = bp * tp
    xp = x_prompt.reshape(m, D_MODEL)
    q, kvc, kvs, kvw, gates, a, ksa, vs, kw, vw = _inproj(xp, row(norm_mix[0]), w_in, tm=512, seq_len=tp)
    c_out = _conv_prompt(a.reshape(bp, tp, CONV_CH), cw, cb, lg, lb, tc=512)
    kc, vc = _compress_prompt(kvc.reshape(bp, tp // CMP_STRIDE, CHUNK_COLS), wc, posx, w2)
    a_out = _nsa_prompt(q, gates, ksa, vs, kw, vw, kc, vc, _selection_matrix(tp // CMP_STRIDE, LANE),
                        bsz=bp, seq_len=tp)
    xp = _ffn0(xp, a_out, c_out.reshape(m, CONV_CH), woa, woc, row(norm_ffn[0]), wg[0], wu[0], wd[0], tm=512)
    y_prompt, pool_tail = _layer1_prompt(xp.reshape(bp, tp, D_MODEL), row(norm_mix[1]), pw, ps, row(norm_ffn[1]),
                                         wg[1], wu[1], wd[1], row(norm_final), tm=512)
    kv6 = lambda z, b: z.reshape(1, b, -1, 2, KV_HEADS, HEAD_DIM)
    new_cmp_p, new_slc_p = kv6(kvc, bp), kv6(kvs, bp)
    new_win_p = kv6(kvw, bp)[:, :, -min(WINDOW, tp):]
    new_conv_p = a.reshape(1, bp, tp, CONV_CH)[:, :, -(CONV_WIDTH - 1):]
    new_pool_p = pool_tail[None, :, -POOL_BUF:]

    xs = x_sample.reshape(bs, D_MODEL)
    q, kvc, kvs, kvw, gates, a, _, _, _, _ = _inproj(xs, row(norm_mix[0]), w_in, tm=bs, seq_len=1)
    q8 = q.reshape(bs, N_HEADS, HEAD_DIM)
    nsp = -(-ns_s // LANE) * LANE
    o_c, idx = _cmp_sample(page_table, q8, cache_cmp_kv[0].reshape(n_pool, PAGE_SIZE // CMP_STRIDE, CHUNK_COLS),
                           wc, posx, w2, _selection_matrix(n_pages * (PAGE_SIZE // CMP_STRIDE), nsp),
                           past=past, ns=ns_s)
    gates8 = gates[:, :N_HEADS * N_BRANCH].reshape(bs, N_HEADS, N_BRANCH)
    att, c_s = _mix_sample(idx, page_table, q8, gates8, o_c, kvs.reshape(bs, 1, KV_COLS), kvw.reshape(bs, 1, KV_COLS),
                           state_win_kv[0].reshape(bs, w_buf, KV_COLS), state_conv[0], a.reshape(bs, 1, CONV_CH),
                           cw, cb, lg, lb,
                           cache_slc_kv[0].reshape(n_pool * (PAGE_SIZE // SEL_BLOCK), SEL_BLOCK, KV_COLS),
                           past=past, n_gather=N_SELECT - 1)
    xs = _ffn0(xs, att.reshape(bs, ATTN_WIDTH).astype(BF16), c_s.reshape(bs, CONV_CH).astype(BF16), woa, woc,
               row(norm_ffn[0]), wg[0], wu[0], wd[0], tm=bs)
    y_sample, h_s = _layer1_sample(xs, jnp.swapaxes(state_pool[0], 0, 1), row(norm_mix[1]), pw, ps, row(norm_ffn[1]),
                                   wg[1], wu[1], wd[1], row(norm_final), first_pos=past)
    new_win_s = jnp.concatenate([state_win_kv[0], kv6(kvw, bs)[0]], axis=1)[None, :, -w_buf:]
    new_conv_s = jnp.concatenate([state_conv[0], a.reshape(bs, 1, CONV_CH)], axis=1)[None, :, -(CONV_WIDTH - 1):]
    new_pool_s = jnp.concatenate([state_pool[0], h_s[:, None, :]], axis=1)[None, :, -POOL_BUF:]

    return (y_prompt, y_sample.reshape(bs, s_new, D_MODEL), new_cmp_p, kv6(kvc, bs), new_slc_p, kv6(kvs, bs),
            new_win_p, new_win_s, new_conv_p, new_conv_s, new_pool_p, new_pool_s)
```

```python
import functools

import jax
import jax.numpy as jnp
from jax import lax
from jax.experimental import pallas as pl
from jax.experimental.pallas import tpu as pltpu

F32 = jnp.float32
BF16 = jnp.bfloat16

D_MODEL = 1024
N_HEADS = 8
HEAD_DIM = 64
KV_HEADS = 2
GROUP = N_HEADS // KV_HEADS
ATTN_WIDTH = N_HEADS * HEAD_DIM
KV_COLS = 2 * KV_HEADS * HEAD_DIM
CMP_LEN = 32
CMP_STRIDE = 16
N_SUB = CMP_LEN // CMP_STRIDE
SEL_BLOCK = 64
SEL_RATIO = SEL_BLOCK // CMP_STRIDE
N_SELECT = 16
WINDOW = 512
PAGE_SIZE = 128
N_BRANCH = 3
SCALE = HEAD_DIM ** -0.5
CONV_CH = D_MODEL // 2
CONV_WIDTH = 31
POOL_WINDOWS = (2, 4, 8, 16)
POOL_GROUP = D_MODEL // len(POOL_WINDOWS)
POOL_BUF = max(POOL_WINDOWS) - 1
EPS = 1e-6
BIG = 1e9
NEG = -1e30
TINY = float(jnp.finfo(jnp.float32).tiny)

LANE = 128
CHUNK_COLS = CMP_STRIDE * KV_COLS
GATE_PAD = LANE
IN_COLS = ATTN_WIDTH + N_BRANCH * KV_COLS + GATE_PAD + 2 * CONV_CH
VMEM_LIMIT = 56 * 1024 * 1024

Q_TILE = 128
SEL_CHUNK = 256
WIN_KEYS = WINDOW + Q_TILE
CONV_HALO = 32
CONV_ROWS = 32
POOL_HALO = 16


def _dot(a, b):
    return jnp.dot(a, b, preferred_element_type=F32)


def _dot_nt(a, b):
    return lax.dot_general(a, b, (((1,), (1,)), ((), ())), preferred_element_type=F32)


def _dot_exact_lhs(a, b):
    hi = a.astype(BF16)
    r1 = a - hi.astype(F32)
    mid = r1.astype(BF16)
    lo = (r1 - mid.astype(F32)).astype(BF16)
    return _dot(hi, b) + _dot(mid, b) + _dot(lo, b)


def _rms(x, g):
    return x * lax.rsqrt(jnp.mean(x * x, axis=-1, keepdims=True) + EPS) * g


def _softmax_rows(s, mask):
    m = jnp.max(s, axis=-1, keepdims=True)
    e = jnp.where(mask, jnp.exp(s - m), 0.0)
    return e / jnp.maximum(jnp.sum(e, axis=-1, keepdims=True), TINY)


def _params(*sem):
    return pltpu.CompilerParams(dimension_semantics=sem, vmem_limit_bytes=VMEM_LIMIT)


def _const_spec(shape):
    nd = len(shape)
    return pl.BlockSpec(shape, lambda *_: (0,) * nd, pipeline_mode=pl.Buffered(1))


def _inproj_kernel(x_ref, g_ref, w_ref, wkvt_ref, q_ref, kvc_ref, kvs_ref, kvw_ref, gate_ref, a_ref,
                   ksa_ref, vs_ref, kw_ref, vw_ref, kvct_ref, kvst_ref, kvwt_ref, *, tm, seq_len):
    h = _rms(x_ref[...], g_ref[...])
    hb = h.astype(BF16)
    z = _dot(hb, w_ref[...])
    zt = _dot_nt(wkvt_ref[...], hb)
    kvct_ref[0] = zt[:KV_COLS]
    kvst_ref[0] = zt[KV_COLS:2 * KV_COLS]
    kvwt_ref[0] = zt[2 * KV_COLS:]
    off = ATTN_WIDTH
    q_ref[...] = (z[:, :off] * SCALE).astype(BF16)
    kvc_ref[...] = z[:, off:off + KV_COLS]
    kvs = z[:, off + KV_COLS:off + 2 * KV_COLS]
    kvw = z[:, off + 2 * KV_COLS:off + 3 * KV_COLS]
    kvs_ref[...] = kvs
    kvw_ref[...] = kvw
    off += 3 * KV_COLS
    gate_ref[...] = jax.nn.sigmoid(z[:, off:off + GATE_PAD])
    off += GATE_PAD
    a_ref[...] = z[:, off:off + CONV_CH] * jax.nn.sigmoid(z[:, off + CONV_CH:])
    pos = (pl.program_id(0) * tm + lax.broadcasted_iota(jnp.int32, (tm, HEAD_DIM), 0)) % seq_len
    onehot = (pos // SEL_BLOCK == lax.broadcasted_iota(jnp.int32, (tm, HEAD_DIM), 1)).astype(BF16)
    for g in range(KV_HEADS):
        k0, v0 = g * HEAD_DIM, (KV_HEADS + g) * HEAD_DIM
        ksa_ref[g] = jnp.concatenate([kvs[:, k0:k0 + HEAD_DIM].astype(BF16), onehot], axis=1)
        vs_ref[g] = kvs[:, v0:v0 + HEAD_DIM].astype(BF16)
        kw_ref[g] = kvw[:, k0:k0 + HEAD_DIM].astype(BF16)
        vw_ref[g] = kvw[:, v0:v0 + HEAD_DIM].astype(BF16)


def _inproj(x, g, w, wkvt, *, tm, seq_len, t_len):
    m = x.shape[0]
    per_row = t_len // tm
    row = lambda c: pl.BlockSpec((tm, c), lambda i: (i, 0))
    grp = lambda c: pl.BlockSpec((KV_HEADS, tm, c), lambda i: (0, i, 0))
    tr = pl.BlockSpec((1, KV_COLS, tm), lambda i: (i // per_row, 0, i % per_row))
    kvt = jax.ShapeDtypeStruct((m // t_len, KV_COLS, t_len), F32)
    out_shape = (
        jax.ShapeDtypeStruct((m, ATTN_WIDTH), BF16),
        jax.ShapeDtypeStruct((m, KV_COLS), F32), jax.ShapeDtypeStruct((m, KV_COLS), F32),
        jax.ShapeDtypeStruct((m, KV_COLS), F32),
        jax.ShapeDtypeStruct((m, GATE_PAD), F32), jax.ShapeDtypeStruct((m, CONV_CH), F32),
        jax.ShapeDtypeStruct((KV_HEADS, m, 2 * HEAD_DIM), BF16),
        jax.ShapeDtypeStruct((KV_HEADS, m, HEAD_DIM), BF16),
        jax.ShapeDtypeStruct((KV_HEADS, m, HEAD_DIM), BF16),
        jax.ShapeDtypeStruct((KV_HEADS, m, HEAD_DIM), BF16),
        kvt, kvt, kvt,
    )
    out_specs = (row(ATTN_WIDTH), row(KV_COLS), row(KV_COLS), row(KV_COLS), row(GATE_PAD), row(CONV_CH),
                 grp(2 * HEAD_DIM), grp(HEAD_DIM), grp(HEAD_DIM), grp(HEAD_DIM), tr, tr, tr)
    return pl.pallas_call(
        functools.partial(_inproj_kernel, tm=tm, seq_len=seq_len),
        grid=(m // tm,),
        in_specs=[row(D_MODEL), _const_spec((1, D_MODEL)), _const_spec((D_MODEL, IN_COLS)),
                  _const_spec((N_BRANCH * KV_COLS, D_MODEL))],
        out_specs=out_specs, out_shape=out_shape,
        compiler_params=_params("parallel"), name="inproj",
    )(x, g, w, wkvt)


def _conv_ln_silu(y, b, lg, lb):
    y = y + b
    mu = jnp.mean(y, axis=-1, keepdims=True)
    var = jnp.mean(jnp.square(y - mu), axis=-1, keepdims=True)
    return jax.nn.silu((y - mu) * lax.rsqrt(var + EPS) * lg + lb)


def _conv_kernel(prev_ref, a_ref, w_ref, b_ref, lg_ref, lb_ref, o_ref, ext_ref, *, tc):
    first = pl.program_id(1) == 0
    ext_ref[0:CONV_HALO, :] = jnp.where(first, 0.0, prev_ref[0])
    ext_ref[CONV_HALO:CONV_HALO + tc, :] = a_ref[0]
    lead = CONV_HALO - (CONV_WIDTH - 1)

    for r0 in range(0, tc, CONV_ROWS):
        acc = jnp.zeros((CONV_ROWS, CONV_CH), F32)
        for k in range(CONV_WIDTH):
            acc = acc + w_ref[k:k + 1, :] * ext_ref[r0 + lead + k:r0 + lead + k + CONV_ROWS, :]
        o_ref[0, r0:r0 + CONV_ROWS, :] = _conv_ln_silu(acc, b_ref[...], lg_ref[...], lb_ref[...]).astype(BF16)


def _conv_prompt(a, w, b, lg, lb, *, tc):
    bsz, t, _ = a.shape
    hb = tc // CONV_HALO
    return pl.pallas_call(
        functools.partial(_conv_kernel, tc=tc),
        grid=(bsz, t // tc),
        in_specs=[pl.BlockSpec((1, CONV_HALO, CONV_CH), lambda bi, i: (bi, jnp.maximum(i * hb - 1, 0), 0)),
                  pl.BlockSpec((1, tc, CONV_CH), lambda bi, i: (bi, i, 0)),
                  _const_spec((CONV_WIDTH, CONV_CH)), _const_spec((1, CONV_CH)),
                  _const_spec((1, CONV_CH)), _const_spec((1, CONV_CH))],
        out_specs=pl.BlockSpec((1, tc, CONV_CH), lambda bi, i: (bi, i, 0)),
        out_shape=jax.ShapeDtypeStruct((bsz, t, CONV_CH), BF16),
        scratch_shapes=[pltpu.VMEM((CONV_HALO + tc, CONV_CH), F32)],
        compiler_params=_params("parallel", "parallel"), name="conv_prompt",
    )(a, a, w, b, lg, lb)


def _compress_rows(h0, h1_next, pos_term, w2):
    hid = pos_term + h0 + h1_next
    return _dot(jax.nn.gelu(hid).astype(BF16), w2)


def _pos_term(posx_ref, wc_ref):
    hp = _dot(posx_ref[...], wc_ref[...])
    return hp[0:1, :KV_COLS] + hp[1:2, KV_COLS:]


def _compress_kernel(x_ref, wc_ref, posx_ref, w2_ref, kc_ref, vc_ref, pos_ref):
    hh = _dot(x_ref[0].astype(BF16), wc_ref[...])
    h1 = hh[:, KV_COLS:]
    h1_next = jnp.concatenate([h1[1:], jnp.zeros((1, KV_COLS), F32)], axis=0)
    pos_term = _pos_term(posx_ref, wc_ref)
    pos_ref[...] = jnp.broadcast_to(pos_term, pos_ref.shape)
    cmp = _compress_rows(hh[:, :KV_COLS], h1_next, pos_term, w2_ref[...])
    for g in range(KV_HEADS):
        kc_ref[0, g] = cmp[:, g * HEAD_DIM:(g + 1) * HEAD_DIM].astype(BF16)
        vc_ref[0, g] = cmp[:, (KV_HEADS + g) * HEAD_DIM:(KV_HEADS + g + 1) * HEAD_DIM].astype(BF16)


def _compress_prompt(kvc_chunks, wc, posx, w2):
    bsz, n_ch, _ = kvc_chunks.shape
    tok = jax.ShapeDtypeStruct((bsz, KV_HEADS, n_ch, HEAD_DIM), BF16)
    tok_spec = pl.BlockSpec((1, KV_HEADS, n_ch, HEAD_DIM), lambda bi: (bi, 0, 0, 0))
    return pl.pallas_call(
        _compress_kernel,
        grid=(bsz,),
        in_specs=[pl.BlockSpec((1, n_ch, CHUNK_COLS), lambda bi: (bi, 0, 0)),
                  _const_spec(wc.shape), _const_spec(posx.shape), _const_spec(w2.shape)],
        out_specs=(tok_spec, tok_spec, pl.BlockSpec((8, KV_COLS), lambda bi: (0, 0))),
        out_shape=(tok, tok, jax.ShapeDtypeStruct((8, KV_COLS), F32)),
        compiler_params=_params("arbitrary"), name="compress_prompt",
    )(kvc_chunks, wc, posx, w2)


def _block_scores(imp, ssel, t_col, ns):
    s = _dot_exact_lhs(imp, ssel)
    j = lax.broadcasted_iota(jnp.int32, s.shape, 1)
    cur = t_col // SEL_BLOCK
    valid = j * SEL_BLOCK <= t_col
    forced = valid & ((j == 0) | (j == cur) | (j == cur - 1))
    s = jnp.where(forced, BIG, jnp.where(valid, s, -BIG))
    return jnp.where(j < ns, s, -3.0 * BIG)


def _nsa_prompt_kernel(q_ref, gate_ref, ksa_ref, vs_ref, kw_ref, vw_ref, kc_ref, vc_ref, ssel_ref, o_ref,
                       *, tq, ns):
    i = pl.program_id(1)
    q0 = i * tq
    t_col = q0 + lax.broadcasted_iota(jnp.int32, (tq, 1), 0)
    t4 = jnp.concatenate([t_col] * GROUP, axis=0)
    gates = gate_ref[...]
    n_blk = kc_ref.shape[2]
    for g in range(KV_HEADS):
        q4 = jnp.concatenate(
            [q_ref[:, (GROUP * g + h) * HEAD_DIM:(GROUP * g + h + 1) * HEAD_DIM] for h in range(GROUP)], axis=0)

        sc = _dot_nt(q4, kc_ref[0, g])
        end = lax.broadcasted_iota(jnp.int32, (1, n_blk), 1) * CMP_STRIDE + (CMP_LEN - 1)
        cmask = end <= t4
        p_c = _softmax_rows(jnp.where(cmask, sc, NEG), cmask)
        o_c = _dot(p_c.astype(BF16), vc_ref[0, g])
        imp = p_c[0:tq] + p_c[tq:2 * tq] + p_c[2 * tq:3 * tq] + p_c[3 * tq:4 * tq]

        s_t = _block_scores(imp, ssel_ref[...], t_col, ns).T
        jrow = lax.broadcasted_iota(jnp.int32, (LANE, tq), 0)
        rank = jnp.zeros((LANE, tq), F32)
        for i2 in range(ns):
            row = s_t[i2:i2 + 1, :]
            rank = rank + jnp.where((row > s_t) | ((row == s_t) & (jrow > i2)), 1.0, 0.0)
        sel = jnp.where(rank < N_SELECT, 0.0, NEG).T
        selbias = jnp.concatenate([sel[:, :HEAD_DIM].astype(BF16)] * GROUP, axis=0)
        q_aug = jnp.concatenate([q4, selbias], axis=1)

        def chunk(c, carry, diagonal):
            m, l, acc = carry
            k0 = pl.multiple_of(c * SEL_CHUNK, SEL_CHUNK)
            s = _dot_nt(q_aug, ksa_ref[g, pl.ds(k0, SEL_CHUNK), :])
            if diagonal:
                key = k0 + lax.broadcasted_iota(jnp.int32, (1, SEL_CHUNK), 1)
                s = jnp.where(key <= t4, s, NEG)
            m_new = jnp.maximum(m, jnp.max(s, axis=-1, keepdims=True))
            alpha = jnp.exp(m - m_new)
            p = jnp.exp(s - m_new)
            l = alpha * l + jnp.sum(p, axis=-1, keepdims=True)
            acc = alpha * acc + _dot(p.astype(BF16), vs_ref[g, pl.ds(k0, SEL_CHUNK), :])
            return m_new, l, acc

        init = (jnp.full((GROUP * tq, 1), NEG, F32), jnp.zeros((GROUP * tq, 1), F32),
                jnp.zeros((GROUP * tq, HEAD_DIM), F32))
        last = q0 // SEL_CHUNK
        carry = lax.fori_loop(0, last, functools.partial(chunk, diagonal=False), init)
        _, l_s, acc_s = chunk(last, carry, True)
        o_s = acc_s / l_s

        w0 = pl.multiple_of(jnp.maximum(q0 - WINDOW, 0), Q_TILE)
        sw = _dot_nt(q4, kw_ref[g, pl.ds(w0, WIN_KEYS), :])
        diff = t4 - (w0 + lax.broadcasted_iota(jnp.int32, (1, WIN_KEYS), 1))
        wmask = (diff >= 0) & (diff < WINDOW)
        p_w = _softmax_rows(jnp.where(wmask, sw, NEG), wmask)
        o_w = _dot(p_w.astype(BF16), vw_ref[g, pl.ds(w0, WIN_KEYS), :])

        for h in range(GROUP):
            hh = GROUP * g + h
            r = slice(h * tq, (h + 1) * tq)
            c = N_BRANCH * hh
            o = gates[:, c:c + 1] * o_c[r] + gates[:, c + 1:c + 2] * o_s[r] + gates[:, c + 2:c + 3] * o_w[r]
            o_ref[:, hh * HEAD_DIM:(hh + 1) * HEAD_DIM] = o.astype(BF16)


def _nsa_prompt(q, gates, ksa, vs, kw, vw, kc, vc, ssel, *, bsz, seq_len):
    tq = Q_TILE
    nt = seq_len // tq
    ns = seq_len // SEL_BLOCK
    n_blk = kc.shape[2]
    row = lambda c: pl.BlockSpec((tq, c), lambda bi, i: (bi * nt + i, 0))
    seq = lambda c: pl.BlockSpec((KV_HEADS, seq_len, c), lambda bi, i: (0, bi, 0))
    tok = pl.BlockSpec((1, KV_HEADS, n_blk, HEAD_DIM), lambda bi, i: (bi, 0, 0, 0))
    return pl.pallas_call(
        functools.partial(_nsa_prompt_kernel, tq=tq, ns=ns),
        grid=(bsz, nt),
        in_specs=[row(ATTN_WIDTH), row(GATE_PAD), seq(2 * HEAD_DIM), seq(HEAD_DIM), seq(HEAD_DIM), seq(HEAD_DIM),
                  tok, tok, _const_spec(ssel.shape)],
        out_specs=row(ATTN_WIDTH),
        out_shape=jax.ShapeDtypeStruct((bsz * seq_len, ATTN_WIDTH), BF16),
        compiler_params=_params("parallel", "parallel"), name="nsa_prompt",
    )(q, gates, ksa, vs, kw, vw, kc, vc, ssel)


def _swiglu_residual(x1, gf, wg_ref, wu_ref, wd_ref):
    h = _rms(x1, gf).astype(BF16)
    act = jax.nn.silu(_dot(h, wg_ref[...])) * _dot(h, wu_ref[...])
    return x1 + _dot(act.astype(BF16), wd_ref[...])


def _ffn0_kernel(x_ref, a_ref, c_ref, woa_ref, woc_ref, gf_ref, wg_ref, wu_ref, wd_ref, o_ref):
    x1 = x_ref[...] + (_dot(a_ref[...], woa_ref[...]) + _dot(c_ref[...], woc_ref[...]))
    o_ref[...] = _swiglu_residual(x1, gf_ref[...], wg_ref, wu_ref, wd_ref)


def _ffn0(x, a, c, woa, woc, gf, wg, wu, wd, *, tm):
    m = x.shape[0]
    row = lambda cols: pl.BlockSpec((tm, cols), lambda i: (i, 0))
    return pl.pallas_call(
        _ffn0_kernel,
        grid=(m // tm,),
        in_specs=[row(D_MODEL), row(ATTN_WIDTH), row(CONV_CH), _const_spec(woa.shape), _const_spec(woc.shape),
                  _const_spec(gf.shape), _const_spec(wg.shape), _const_spec(wu.shape), _const_spec(wd.shape)],
        out_specs=row(D_MODEL), out_shape=jax.ShapeDtypeStruct((m, D_MODEL), F32),
        compiler_params=_params("parallel"), name="outproj_ffn",
    )(x, a, c, woa, woc, gf, wg, wu, wd)


def _pool_mix(x, h, win_sums, cnts, pw_ref, ps):
    ys = []
    for g in range(len(POOL_WINDOWS)):
        z = win_sums[g] / cnts[g] - h[:, g * POOL_GROUP:(g + 1) * POOL_GROUP]
        ys.append(_dot(z.astype(BF16), pw_ref[g]))
    return x + jnp.concatenate(ys, axis=1) * ps


def _layer1_prompt_kernel(xprev_ref, x_ref, gm_ref, pw_ref, ps_ref, gf_ref, wg_ref, wu_ref, wd_ref, gfin_ref,
                          y_ref, hst_ref, *, tm):
    i = pl.program_id(1)
    x = x_ref[0]
    h = _rms(x, gm_ref[...])
    hprev = jnp.where(i == 0, 0.0, _rms(xprev_ref[0], gm_ref[...]))
    hst_ref[0] = h[tm - POOL_HALO:, :]
    pos = i * tm + lax.broadcasted_iota(jnp.int32, (tm, 1), 0)
    sums, cnts = [], []
    for g, w in enumerate(POOL_WINDOWS):
        e = jnp.concatenate([hprev[:, g * POOL_GROUP:(g + 1) * POOL_GROUP],
                             h[:, g * POOL_GROUP:(g + 1) * POOL_GROUP]], axis=0)
        span = 1
        while span < w:
            e = e[span:] + e[:-span]
            span *= 2
        first = POOL_HALO - (w - 1)
        sums.append(e[first:first + tm])
        cnts.append(jnp.minimum(w, pos + 1).astype(F32))
    x1 = _pool_mix(x, h, sums, cnts, pw_ref, ps_ref[...])
    x2 = _swiglu_residual(x1, gf_ref[...], wg_ref, wu_ref, wd_ref)
    y_ref[0] = _rms(x2, gfin_ref[...])


def _layer1_prompt(x, gm, pw, ps, gf, wg, wu, wd, gfin, *, tm):
    bsz, t, _ = x.shape
    hb = tm // POOL_HALO
    return pl.pallas_call(
        functools.partial(_layer1_prompt_kernel, tm=tm),
        grid=(bsz, t // tm),
        in_specs=[pl.BlockSpec((1, POOL_HALO, D_MODEL), lambda bi, i: (bi, jnp.maximum(i * hb - 1, 0), 0)),
                  pl.BlockSpec((1, tm, D_MODEL), lambda bi, i: (bi, i, 0)),
                  _const_spec(gm.shape), _const_spec(pw.shape), _const_spec(ps.shape), _const_spec(gf.shape),
                  _const_spec(wg.shape), _const_spec(wu.shape), _const_spec(wd.shape), _const_spec(gfin.shape)],
        out_specs=(pl.BlockSpec((1, tm, D_MODEL), lambda bi, i: (bi, i, 0)),
                   pl.BlockSpec((1, POOL_HALO, D_MODEL), lambda bi, i: (bi, 0, 0))),
        out_shape=(jax.ShapeDtypeStruct((bsz, t, D_MODEL), F32),
                   jax.ShapeDtypeStruct((bsz, POOL_HALO, D_MODEL), F32)),
        compiler_params=_params("parallel", "arbitrary"), name="layer1_prompt",
    )(x, x, gm, pw, ps, gf, wg, wu, wd, gfin)


def _layer1_sample_kernel(x_ref, hist_ref, gm_ref, pw_ref, ps_ref, gf_ref, wg_ref, wu_ref, wd_ref, gfin_ref,
                          y_ref, h_ref, *, first_pos):
    x = x_ref[...]
    h = _rms(x, gm_ref[...])
    h_ref[...] = h
    sums, cnts = [], []
    for g, w in enumerate(POOL_WINDOWS):
        c = slice(g * POOL_GROUP, (g + 1) * POOL_GROUP)
        s = h[:, c]
        for k in range(1, w):
            s = s + hist_ref[POOL_BUF - k][:, c]
        sums.append(s)
        cnts.append(float(min(w, first_pos + 1)))
    x1 = _pool_mix(x, h, sums, cnts, pw_ref, ps_ref[...])
    x2 = _swiglu_residual(x1, gf_ref[...], wg_ref, wu_ref, wd_ref)
    y_ref[...] = _rms(x2, gfin_ref[...])


def _layer1_sample(x, hist, gm, pw, ps, gf, wg, wu, wd, gfin, *, first_pos):
    m = x.shape[0]
    args = (x, hist, gm, pw, ps, gf, wg, wu, wd, gfin)
    out = jax.ShapeDtypeStruct((m, D_MODEL), F32)
    return pl.pallas_call(
        functools.partial(_layer1_sample_kernel, first_pos=first_pos),
        grid=(1,),
        in_specs=[_const_spec(a.shape) for a in args],
        out_specs=(_const_spec((m, D_MODEL)), _const_spec((m, D_MODEL))), out_shape=(out, out),
        compiler_params=_params("arbitrary"), name="layer1_sample",
    )(*args)


def _page_copy(cache_ref, xt_ref, sem, phys, p):
    return pltpu.make_async_copy(cache_ref.at[phys], xt_ref.at[p], sem)


def _cmp_sample_kernel(pt_ref, q_ref, cache_ref, wl_ref, pos_ref, w2_ref, ssel_ref, oc_ref, idx_ref,
                       xt_ref, xk_ref, xv_ref, hbuf_ref, sem, *, n_pages, past, ns, nsp):
    b = pl.program_id(0)
    n_ch = n_pages * (PAGE_SIZE // CMP_STRIDE)

    def start(p, carry):
        _page_copy(cache_ref, xt_ref, sem, pt_ref[b * n_pages + p], p).start()
        return carry

    def wait(p, carry):
        _page_copy(cache_ref, xt_ref, sem, 0, p).wait()
        return carry

    def to_rows(p, carry):
        r0 = pl.multiple_of(p * PAGE_SIZE, PAGE_SIZE)
        xk_ref[pl.ds(r0, PAGE_SIZE), :] = xt_ref[p, 0:LANE, :].T
        xv_ref[pl.ds(r0, PAGE_SIZE), :] = xt_ref[p, LANE:KV_COLS, :].T
        return carry

    lax.fori_loop(0, n_pages, start, 0)
    lax.fori_loop(0, n_pages, wait, 0)
    lax.fori_loop(0, n_pages, to_rows, 0)

    acc = None
    for l in range(CMP_STRIDE):
        rows_l = jnp.concatenate([xk_ref[pl.ds(l, n_ch, stride=CMP_STRIDE), :],
                                  xv_ref[pl.ds(l, n_ch, stride=CMP_STRIDE), :]], axis=1)
        part = _dot(rows_l.astype(BF16), wl_ref[l])
        acc = part if acc is None else acc + part
    hbuf_ref[0:n_ch, :] = acc
    hbuf_ref[n_ch:n_ch + 8, :] = jnp.zeros((8, 2 * KV_COLS), F32)
    cmp = _compress_rows(hbuf_ref[0:n_ch, :KV_COLS], hbuf_ref[pl.ds(1, n_ch), KV_COLS:],
                         pos_ref[0:1, :], w2_ref[...])

    q8 = q_ref[0]
    head = lax.broadcasted_iota(jnp.int32, (N_HEADS, 1), 0)
    end = lax.broadcasted_iota(jnp.int32, (1, n_ch), 1) * CMP_STRIDE + (CMP_LEN - 1)
    cmask = jnp.broadcast_to(end <= past, (N_HEADS, n_ch))
    o_c = jnp.zeros((N_HEADS, HEAD_DIM), F32)
    jl = lax.broadcasted_iota(jnp.int32, (1, nsp), 1)
    ii = lax.broadcasted_iota(jnp.int32, (nsp, nsp), 0)
    jj = lax.broadcasted_iota(jnp.int32, (nsp, nsp), 1)
    cur = past // SEL_BLOCK
    t_col = jnp.full((N_HEADS, 1), past, jnp.int32)
    for g in range(KV_HEADS):
        in_group = head // GROUP == g
        kc = cmp[:, g * HEAD_DIM:(g + 1) * HEAD_DIM].astype(BF16)
        vc = cmp[:, (KV_HEADS + g) * HEAD_DIM:(KV_HEADS + g + 1) * HEAD_DIM].astype(BF16)
        sc = _dot_nt(q8, kc)
        p_c = _softmax_rows(jnp.where(cmask, sc, NEG), cmask)
        o_c = jnp.where(in_group, _dot(p_c.astype(BF16), vc), o_c)
        imp = jnp.sum(jnp.where(in_group, p_c, 0.0), axis=0, keepdims=True)
        s_b = _block_scores(jnp.broadcast_to(imp, (N_HEADS, n_ch)), ssel_ref[...], t_col, ns)
        s_b = jnp.broadcast_to(s_b[0:1], (nsp, nsp))
        s_t = s_b.T
        beats = (s_t > s_b) | ((s_t == s_b) & (ii < jj))
        rank = jnp.sum(jnp.where(beats, 1.0, 0.0), axis=0, keepdims=True)
        sel = (rank < N_SELECT) & (jl != cur)
        sel_f = jnp.where(sel, 1.0, 0.0)
        before = _dot(jnp.broadcast_to(sel_f, (N_HEADS, nsp)).astype(BF16),
                      jnp.where(ii < jj, 1.0, 0.0).astype(BF16))[0:1]
        slot = lax.broadcasted_iota(jnp.int32, (N_SELECT, nsp), 0)
        pick = jnp.broadcast_to(sel, (N_SELECT, nsp)) & (jnp.broadcast_to(before, (N_SELECT, nsp)) == slot.astype(F32))
        blk = jnp.sum(jnp.where(pick, lax.broadcasted_iota(jnp.int32, (N_SELECT, nsp), 1), 0), axis=1, keepdims=True)
        idx_ref[0, g] = blk
    oc_ref[0] = o_c


def _cmp_sample(page_table, q8, cache_pages, wl, pos_term, w2, ssel, *, past, ns):
    bsz, n_pages = page_table.shape
    n_ch = n_pages * (PAGE_SIZE // CMP_STRIDE)
    nsp = ssel.shape[1]
    grid_spec = pltpu.PrefetchScalarGridSpec(
        num_scalar_prefetch=1, grid=(bsz,),
        in_specs=[pl.BlockSpec((1, N_HEADS, HEAD_DIM), lambda b, pt: (b, 0, 0)),
                  pl.BlockSpec(memory_space=pl.ANY),
                  _const_spec(wl.shape), _const_spec(pos_term.shape), _const_spec(w2.shape), _const_spec(ssel.shape)],
        out_specs=(pl.BlockSpec((1, N_HEADS, HEAD_DIM), lambda b, pt: (b, 0, 0)),
                   pl.BlockSpec((1, KV_HEADS, N_SELECT, 1), lambda b, pt: (b, 0, 0, 0))),
        scratch_shapes=[pltpu.VMEM((n_pages, KV_COLS, PAGE_SIZE), F32), pltpu.VMEM((n_pages * PAGE_SIZE, LANE), F32),
                        pltpu.VMEM((n_pages * PAGE_SIZE, LANE), F32),
                        pltpu.VMEM((n_ch + 8, 2 * KV_COLS), F32), pltpu.SemaphoreType.DMA(())],
    )
    return pl.pallas_call(
        functools.partial(_cmp_sample_kernel, n_pages=n_pages, past=past, ns=ns, nsp=nsp),
        grid_spec=grid_spec,
        out_shape=(jax.ShapeDtypeStruct((bsz, N_HEADS, HEAD_DIM), F32),
                   jax.ShapeDtypeStruct((bsz, KV_HEADS, N_SELECT, 1), jnp.int32)),
        compiler_params=_params("arbitrary"), name="cmp_sample",
    )(page_table.reshape(-1), q8, cache_pages, wl, pos_term, w2, ssel)


def _slab_copy(cache_ref, dst_ref, sem, src, g, k):
    return pltpu.make_async_copy(cache_ref.at[src], dst_ref.at[g, :, pl.ds(k * PAGE_SIZE, PAGE_SIZE)], sem)


def _mix_sample_kernel(idx_ref, pt_ref, q_ref, gate_ref, oc_ref, kvs_ref, kvw_ref, win_ref, cst_ref, a_ref,
                       cw_ref, cb_ref, lg_ref, lb_ref, cache_ref, att_ref, conv_ref, kt_ref, vt_ref, sem,
                       *, n_pages, n_gather, past, w_buf):
    b = pl.program_id(0)
    per_page = PAGE_SIZE // SEL_BLOCK
    halves = []
    for g in range(KV_HEADS):
        for k in range(n_gather):
            blk = idx_ref[(b * KV_HEADS + g) * N_SELECT + k]
            slab = pt_ref[b * n_pages + blk // per_page] * (2 * KV_HEADS) + g
            _slab_copy(cache_ref, kt_ref, sem, slab, g, k).start()
            _slab_copy(cache_ref, vt_ref, sem, slab + KV_HEADS, g, k).start()
            halves.append(blk % per_page)

    y = (jnp.sum(cw_ref[0:CONV_WIDTH - 1, :] * cst_ref[0], axis=0, keepdims=True)
         + cw_ref[CONV_WIDTH - 1:CONV_WIDTH, :] * a_ref[0])
    conv_ref[0] = _conv_ln_silu(y, cb_ref[...], lg_ref[...], lb_ref[...])

    def wait(s, carry):
        _slab_copy(cache_ref, kt_ref, sem, 0, 0, 0).wait()
        return carry

    lax.fori_loop(0, 2 * KV_HEADS * n_gather, wait, 0)

    q8 = q_ref[0]
    q8f = q8.astype(F32)
    head = lax.broadcasted_iota(jnp.int32, (N_HEADS, 1), 0)
    gates = gate_ref[0]
    kvs_new, kvw_new = kvs_ref[0], kvw_ref[0]
    jw = lax.broadcasted_iota(jnp.int32, (1, w_buf), 1)
    wdiff = w_buf - jw
    wmask = jnp.broadcast_to((wdiff < WINDOW) & (past - wdiff >= 0), (N_HEADS, w_buf))
    half_of_lane = lax.broadcasted_iota(jnp.int32, (1, PAGE_SIZE), 1) // SEL_BLOCK
    out = jnp.zeros((N_HEADS, HEAD_DIM), F32)

    def with_new_row(kt, vt, mask, k_new, v_new):
        s = jnp.where(mask, _dot(q8, kt.astype(BF16)), NEG)
        s_new = jnp.sum(q8f * k_new.astype(BF16).astype(F32), axis=-1, keepdims=True)
        m = jnp.maximum(jnp.max(s, axis=-1, keepdims=True), s_new)
        e = jnp.where(mask, jnp.exp(s - m), 0.0)
        e_new = jnp.exp(s_new - m)
        num = _dot_nt(e.astype(BF16), vt.astype(BF16)) + e_new * v_new
        return num / (jnp.sum(e, axis=-1, keepdims=True) + e_new)

    for g in range(KV_HEADS):
        kc, vc = slice(g * HEAD_DIM, (g + 1) * HEAD_DIM), slice((KV_HEADS + g) * HEAD_DIM, (KV_HEADS + g + 1) * HEAD_DIM)
        smask = jnp.concatenate([half_of_lane == halves[g * n_gather + k] for k in range(n_gather)], axis=1)
        smask = jnp.broadcast_to(smask, (N_HEADS, n_gather * PAGE_SIZE))
        o_s = with_new_row(kt_ref[g], vt_ref[g], smask, kvs_new[:, kc], kvs_new[:, vc])
        o_w = with_new_row(win_ref[0, g], win_ref[0, KV_HEADS + g], wmask, kvw_new[:, kc], kvw_new[:, vc])
        mixed = gates[:, 0:1] * oc_ref[0] + gates[:, 1:2] * o_s + gates[:, 2:3] * o_w
        out = jnp.where(head // GROUP == g, mixed, out)
    att_ref[0] = out


def _mix_sample(idx, page_table, q8, gates8, o_c, kvs, kvw, win, conv_state, a, cw, cb, lg, lb, cache_slabs,
                *, past, n_gather):
    bsz, n_pages = page_table.shape
    w_buf = win.shape[-1]
    one = lambda *shape: pl.BlockSpec((1,) + shape, lambda b, *_: (b,) + (0,) * len(shape))
    const = lambda shape: pl.BlockSpec(shape, lambda b, *_: (0,) * len(shape))
    gathered = pltpu.VMEM((KV_HEADS, HEAD_DIM, n_gather * PAGE_SIZE), F32)
    grid_spec = pltpu.PrefetchScalarGridSpec(
        num_scalar_prefetch=2, grid=(bsz,),
        in_specs=[one(N_HEADS, HEAD_DIM), one(N_HEADS, N_BRANCH), one(N_HEADS, HEAD_DIM), one(1, KV_COLS),
                  one(1, KV_COLS), one(2 * KV_HEADS, HEAD_DIM, w_buf), one(CONV_WIDTH - 1, CONV_CH), one(1, CONV_CH),
                  const(cw.shape), const(cb.shape), const(lg.shape), const(lb.shape),
                  pl.BlockSpec(memory_space=pl.ANY)],
        out_specs=(one(N_HEADS, HEAD_DIM), one(1, CONV_CH)),
        scratch_shapes=[gathered, gathered, pltpu.SemaphoreType.DMA(())],
    )
    return pl.pallas_call(
        functools.partial(_mix_sample_kernel, n_pages=n_pages, n_gather=n_gather, past=past, w_buf=w_buf),
        grid_spec=grid_spec,
        out_shape=(jax.ShapeDtypeStruct((bsz, N_HEADS, HEAD_DIM), F32),
                   jax.ShapeDtypeStruct((bsz, 1, CONV_CH), F32)),
        compiler_params=_params("arbitrary"), name="mix_sample",
    )(idx.reshape(-1), page_table.reshape(-1), q8, gates8, o_c, kvs, kvw, win, conv_state, a, cw, cb, lg, lb,
      cache_slabs)


def _inproj_weight(w_in):
    off = ATTN_WIDTH + N_BRANCH * KV_COLS
    n_gate = N_HEADS * N_BRANCH
    gate = jnp.pad(w_in[:, off:off + n_gate], ((0, 0), (0, GATE_PAD - n_gate)))
    return jnp.concatenate([w_in[:, :off], gate, w_in[:, off + n_gate:]], axis=1).astype(BF16)


def _compress_weights(pos_k, w1_k, w2_k, pos_v, w1_v, w2_v):
    n_slot = 2 * KV_HEADS
    eye = jnp.eye(n_slot, dtype=F32)

    def place(per_slot):
        w = jnp.stack(per_slot, axis=0)
        full = jnp.einsum('jclde,jk->ljdcke', w, eye)
        return full.reshape(CHUNK_COLS, N_SUB * KV_COLS)

    w1k = w1_k.reshape(N_SUB, CMP_STRIDE, HEAD_DIM, HEAD_DIM)
    w1v = w1_v.reshape(N_SUB, CMP_STRIDE, HEAD_DIM, HEAD_DIM)
    wc = place([w1k, w1k, w1v, w1v]).astype(BF16)
    pk = pos_k.reshape(N_SUB, CMP_STRIDE, 1, HEAD_DIM)
    pv = pos_v.reshape(N_SUB, CMP_STRIDE, 1, HEAD_DIM)
    posx = jnp.concatenate([pk, pk, pv, pv], axis=2).reshape(N_SUB, CHUNK_COLS)
    posx = jnp.pad(posx, ((0, 8 - N_SUB), (0, 0))).astype(BF16)
    w2 = jnp.einsum('jef,jk->jekf', jnp.stack([w2_k, w2_k, w2_v, w2_v]), eye).reshape(KV_COLS, KV_COLS).astype(BF16)
    return wc, posx, w2


def _selection_matrix(n_rows, n_cols):
    n = jnp.arange(n_rows)[:, None]
    j = jnp.arange(n_cols)[None, :]
    cnt = jnp.zeros((n_rows, n_cols), F32)
    for m in range(SEL_RATIO):
        for sub in range(N_SUB):
            cnt = cnt + (SEL_RATIO * j + m - sub == n).astype(F32)
    return cnt.astype(BF16)


def kernel(x_prompt, x_sample, cache_cmp_kv, cache_slc_kv, state_win_kv, state_conv, state_pool, page_table,
           norm_mix, norm_ffn, norm_final, w_in_a, w_out_a, cmp_pos_k, cmp_w1_k, cmp_w2_k, cmp_pos_v, cmp_w1_v,
           cmp_w2_v, conv_w, conv_b, conv_ln_g, conv_ln_b, pool_w, pool_scale, w_ffn_gate, w_ffn_up, w_ffn_down):
    bp, tp, _ = x_prompt.shape
    bs, s_new, _ = x_sample.shape
    n_pages = page_table.shape[1]
    past = n_pages * PAGE_SIZE
    w_buf = state_win_kv.shape[2]
    n_pool = cache_cmp_kv.shape[1]
    ns_p = tp // SEL_BLOCK
    ns_s = -(-(past + s_new) // SEL_BLOCK)
    assert s_new == 1 and tp % SEL_CHUNK == 0 and tp >= WIN_KEYS and N_SELECT <= ns_p <= HEAD_DIM
    assert ns_s > N_SELECT and norm_mix.shape[0] == 2

    row = lambda v: v.reshape(1, -1)
    w_in = _inproj_weight(w_in_a[0])
    wkvt = w_in_a[0, :, ATTN_WIDTH:ATTN_WIDTH + N_BRANCH * KV_COLS].T.astype(BF16)
    wc, posx, w2 = _compress_weights(cmp_pos_k[0], cmp_w1_k[0], cmp_w2_k[0], cmp_pos_v[0], cmp_w1_v[0], cmp_w2_v[0])
    woa, woc = w_out_a[0, :ATTN_WIDTH].astype(BF16), w_out_a[0, ATTN_WIDTH:].astype(BF16)
    wg, wu, wd = w_ffn_gate.astype(BF16), w_ffn_up.astype(BF16), w_ffn_down.astype(BF16)
    cw, cb, lg, lb = conv_w[0], row(conv_b[0]), row(conv_ln_g[0]), row(conv_ln_b[0])
    pw, ps = pool_w[0].astype(BF16), row(pool_scale[0])

    m = bp * tp
    xp = x_prompt.reshape(m, D_MODEL)
    q, kvc, kvs, kvw, gates, a, ksa, vs, kw, vw, kvct, kvst, kvwt = _inproj(
        xp, row(norm_mix[0]), w_in, wkvt, tm=512, seq_len=tp, t_len=tp)
    c_out = _conv_prompt(a.reshape(bp, tp, CONV_CH), cw, cb, lg, lb, tc=512)
    kc, vc, pos_term = _compress_prompt(kvc.reshape(bp, tp // CMP_STRIDE, CHUNK_COLS), wc, posx, w2)
    a_out = _nsa_prompt(q, gates, ksa, vs, kw, vw, kc, vc, _selection_matrix(tp // CMP_STRIDE, LANE),
                        bsz=bp, seq_len=tp)
    xp = _ffn0(xp, a_out, c_out.reshape(m, CONV_CH), woa, woc, row(norm_ffn[0]), wg[0], wu[0], wd[0], tm=512)
    y_prompt, pool_tail = _layer1_prompt(xp.reshape(bp, tp, D_MODEL), row(norm_mix[1]), pw, ps, row(norm_ffn[1]),
                                         wg[1], wu[1], wd[1], row(norm_final), tm=512)
    kv6 = lambda z, b: z.reshape(1, b, -1, 2, KV_HEADS, HEAD_DIM)
    kv6_t = lambda zt: jnp.transpose(zt.reshape(bp, 2, KV_HEADS, HEAD_DIM, tp), (0, 4, 1, 2, 3))[None]
    new_cmp_p, new_slc_p = kv6_t(kvct), kv6_t(kvst)
    new_win_p = kv6_t(kvwt)[:, :, -min(WINDOW, tp):]
    new_conv_p = a.reshape(1, bp, tp, CONV_CH)[:, :, -(CONV_WIDTH - 1):]
    new_pool_p = pool_tail[None, :, -POOL_BUF:]

    xs = x_sample.reshape(bs, D_MODEL)
    q, kvc, kvs, kvw, gates, a = _inproj(xs, row(norm_mix[0]), w_in, wkvt, tm=bs, seq_len=1, t_len=bs)[:6]
    q8 = q.reshape(bs, N_HEADS, HEAD_DIM)
    nsp = -(-ns_s // LANE) * LANE
    rows_last = lambda c: jnp.transpose(c, (0, 2, 3, 4, 1))
    o_c, idx = _cmp_sample(page_table, q8, rows_last(cache_cmp_kv[0]).reshape(n_pool, KV_COLS, PAGE_SIZE),
                           wc.reshape(CMP_STRIDE, KV_COLS, N_SUB * KV_COLS), pos_term, w2,
                           _selection_matrix(n_pages * (PAGE_SIZE // CMP_STRIDE), nsp), past=past, ns=ns_s)
    gates8 = gates[:, :N_HEADS * N_BRANCH].reshape(bs, N_HEADS, N_BRANCH)
    att, c_s = _mix_sample(idx, page_table, q8, gates8, o_c, kvs.reshape(bs, 1, KV_COLS), kvw.reshape(bs, 1, KV_COLS),
                           rows_last(state_win_kv[0]).reshape(bs, 2 * KV_HEADS, HEAD_DIM, w_buf), state_conv[0],
                           a.reshape(bs, 1, CONV_CH), cw, cb, lg, lb,
                           rows_last(cache_slc_kv[0]).reshape(n_pool * 2 * KV_HEADS, HEAD_DIM, PAGE_SIZE),
                           past=past, n_gather=N_SELECT - 1)
    xs = _ffn0(xs, att.reshape(bs, ATTN_WIDTH).astype(BF16), c_s.reshape(bs, CONV_CH).astype(BF16), woa, woc,
               row(norm_ffn[0]), wg[0], wu[0], wd[0], tm=bs)
    y_sample, h_s = _layer1_sample(xs, jnp.swapaxes(state_pool[0], 0, 1), row(norm_mix[1]), pw, ps, row(norm_ffn[1]),
                                   wg[1], wu[1], wd[1], row(norm_final), first_pos=past)
    new_win_s = jnp.concatenate([state_win_kv[0], kv6(kvw, bs)[0]], axis=1)[None, :, -w_buf:]
    new_conv_s = jnp.concatenate([state_conv[0], a.reshape(bs, 1, CONV_CH)], axis=1)[None, :, -(CONV_WIDTH - 1):]
    new_pool_s = jnp.concatenate([state_pool[0], h_s[:, None, :]], axis=1)[None, :, -POOL_BUF:]

    return (y_prompt, y_sample.reshape(bs, s_new, D_MODEL), new_cmp_p, kv6(kvc, bs), new_slc_p, kv6(kvs, bs),
            new_win_p, new_win_s, new_conv_p, new_conv_s, new_pool_p, new_pool_s)
```

```python
import functools

import jax
import jax.numpy as jnp
from jax import lax
from jax.experimental import pallas as pl
from jax.experimental.pallas import tpu as pltpu

F32 = jnp.float32
BF16 = jnp.bfloat16

D_MODEL = 1024
N_HEADS = 8
HEAD_DIM = 64
KV_HEADS = 2
GROUP = N_HEADS // KV_HEADS
ATTN_WIDTH = N_HEADS * HEAD_DIM
KV_COLS = 2 * KV_HEADS * HEAD_DIM
CMP_LEN = 32
CMP_STRIDE = 16
N_SUB = CMP_LEN // CMP_STRIDE
SEL_BLOCK = 64
SEL_RATIO = SEL_BLOCK // CMP_STRIDE
N_SELECT = 16
WINDOW = 512
PAGE_SIZE = 128
N_BRANCH = 3
SCALE = HEAD_DIM ** -0.5
CONV_CH = D_MODEL // 2
CONV_WIDTH = 31
POOL_WINDOWS = (2, 4, 8, 16)
POOL_GROUP = D_MODEL // len(POOL_WINDOWS)
POOL_BUF = max(POOL_WINDOWS) - 1
EPS = 1e-6
BIG = 1e9
NEG = -1e30
TINY = float(jnp.finfo(jnp.float32).tiny)

LANE = 128
SUBLANE = 8
CHUNK_COLS = CMP_STRIDE * KV_COLS
GATE_PAD = LANE
IN_COLS = ATTN_WIDTH + N_BRANCH * KV_COLS + GATE_PAD + 2 * CONV_CH
VMEM_LIMIT = 56 * 1024 * 1024

Q_TILE = 128
SEL_CHUNK = 512
WIN_KEYS = WINDOW + Q_TILE
CONV_HALO = 32
CONV_ROWS = 32
POOL_HALO = 16


def _dot(a, b):
    return jnp.dot(a, b, preferred_element_type=F32)


def _dot_nt(a, b):
    return lax.dot_general(a, b, (((1,), (1,)), ((), ())), preferred_element_type=F32)


def _dot_exact_lhs(a, b):
    hi = a.astype(BF16)
    r1 = a - hi.astype(F32)
    mid = r1.astype(BF16)
    lo = (r1 - mid.astype(F32)).astype(BF16)
    return _dot(hi, b) + _dot(mid, b) + _dot(lo, b)


def _rms(x, g):
    return x * lax.rsqrt(jnp.mean(x * x, axis=-1, keepdims=True) + EPS) * g


def _softmax_rows(s, mask):
    m = jnp.max(s, axis=-1, keepdims=True)
    e = jnp.where(mask, jnp.exp(s - m), 0.0)
    return e / jnp.maximum(jnp.sum(e, axis=-1, keepdims=True), TINY)


def _softmax_parts(s, mask):
    s = jnp.where(mask, s, NEG)
    e = jnp.where(mask, jnp.exp(s - jnp.max(s, axis=-1, keepdims=True)), 0.0)
    return e, 1.0 / jnp.maximum(jnp.sum(e, axis=-1, keepdims=True), TINY)


def _params(*sem):
    return pltpu.CompilerParams(dimension_semantics=sem, vmem_limit_bytes=VMEM_LIMIT)


def _const_spec(shape):
    nd = len(shape)
    return pl.BlockSpec(shape, lambda *_: (0,) * nd, pipeline_mode=pl.Buffered(1))


def _inproj_kernel(x_ref, g_ref, w_ref, wkvt_ref, q_ref, kvc_ref, kvs_ref, kvw_ref, gate_ref, a_ref,
                   ksa_ref, vs_ref, kw_ref, vw_ref, kvct_ref, kvst_ref, kvwt_ref, *, tm, seq_len):
    h = _rms(x_ref[...], g_ref[...])
    hb = h.astype(BF16)
    z = _dot(hb, w_ref[...])
    zt = _dot_nt(wkvt_ref[...], hb)
    kvct_ref[0] = zt[:KV_COLS]
    kvst_ref[0] = zt[KV_COLS:2 * KV_COLS]
    kvwt_ref[0] = zt[2 * KV_COLS:]
    off = ATTN_WIDTH
    q_ref[...] = (z[:, :off] * SCALE).astype(BF16)
    kvc_ref[...] = z[:, off:off + KV_COLS]
    kvs = z[:, off + KV_COLS:off + 2 * KV_COLS]
    kvw = z[:, off + 2 * KV_COLS:off + 3 * KV_COLS]
    kvs_ref[...] = kvs
    kvw_ref[...] = kvw
    off += 3 * KV_COLS
    gate_ref[...] = jax.nn.sigmoid(z[:, off:off + GATE_PAD])
    off += GATE_PAD
    a_ref[...] = z[:, off:off + CONV_CH] * jax.nn.sigmoid(z[:, off + CONV_CH:])
    pos = (pl.program_id(0) * tm + lax.broadcasted_iota(jnp.int32, (tm, HEAD_DIM), 0)) % seq_len
    onehot = (pos // SEL_BLOCK == lax.broadcasted_iota(jnp.int32, (tm, HEAD_DIM), 1)).astype(BF16)
    for g in range(KV_HEADS):
        k0, v0 = g * HEAD_DIM, (KV_HEADS + g) * HEAD_DIM
        ksa_ref[g] = jnp.concatenate([kvs[:, k0:k0 + HEAD_DIM].astype(BF16), onehot], axis=1)
        vs_ref[g] = kvs[:, v0:v0 + HEAD_DIM].astype(BF16)
        kw_ref[g] = kvw[:, k0:k0 + HEAD_DIM].astype(BF16)
        vw_ref[g] = kvw[:, v0:v0 + HEAD_DIM].astype(BF16)


def _inproj(x, g, w, wkvt, *, tm, seq_len, t_len):
    m = x.shape[0]
    per_row = t_len // tm
    row = lambda c: pl.BlockSpec((tm, c), lambda i: (i, 0))
    grp = lambda c: pl.BlockSpec((KV_HEADS, tm, c), lambda i: (0, i, 0))
    tr = pl.BlockSpec((1, KV_COLS, tm), lambda i: (i // per_row, 0, i % per_row))
    kvt = jax.ShapeDtypeStruct((m // t_len, KV_COLS, t_len), F32)
    out_shape = (
        jax.ShapeDtypeStruct((m, ATTN_WIDTH), BF16),
        jax.ShapeDtypeStruct((m, KV_COLS), F32), jax.ShapeDtypeStruct((m, KV_COLS), F32),
        jax.ShapeDtypeStruct((m, KV_COLS), F32),
        jax.ShapeDtypeStruct((m, GATE_PAD), F32), jax.ShapeDtypeStruct((m, CONV_CH), F32),
        jax.ShapeDtypeStruct((KV_HEADS, m, 2 * HEAD_DIM), BF16),
        jax.ShapeDtypeStruct((KV_HEADS, m, HEAD_DIM), BF16),
        jax.ShapeDtypeStruct((KV_HEADS, m, HEAD_DIM), BF16),
        jax.ShapeDtypeStruct((KV_HEADS, m, HEAD_DIM), BF16),
        kvt, kvt, kvt,
    )
    out_specs = (row(ATTN_WIDTH), row(KV_COLS), row(KV_COLS), row(KV_COLS), row(GATE_PAD), row(CONV_CH),
                 grp(2 * HEAD_DIM), grp(HEAD_DIM), grp(HEAD_DIM), grp(HEAD_DIM), tr, tr, tr)
    return pl.pallas_call(
        functools.partial(_inproj_kernel, tm=tm, seq_len=seq_len),
        grid=(m // tm,),
        in_specs=[row(D_MODEL), _const_spec((1, D_MODEL)), _const_spec((D_MODEL, IN_COLS)),
                  _const_spec((N_BRANCH * KV_COLS, D_MODEL))],
        out_specs=out_specs, out_shape=out_shape,
        compiler_params=_params("parallel"), name="inproj",
    )(x, g, w, wkvt)


def _conv_ln_silu(y, b, lg, lb):
    y = y + b
    mu = jnp.mean(y, axis=-1, keepdims=True)
    var = jnp.mean(jnp.square(y - mu), axis=-1, keepdims=True)
    return jax.nn.silu((y - mu) * lax.rsqrt(var + EPS) * lg + lb)


def _conv_kernel(prev_ref, a_ref, w_ref, b_ref, lg_ref, lb_ref, o_ref, ext_ref, sh_ref, *, tc):
    first = pl.program_id(1) == 0
    ext_ref[0:CONV_HALO, :] = jnp.where(first, 0.0, prev_ref[0])
    ext_ref[CONV_HALO:CONV_HALO + tc, :] = a_ref[0]
    lead = CONV_HALO - (CONV_WIDTH - 1)
    span = sh_ref.shape[1]
    for r in range(1, SUBLANE):
        sh_ref[r - 1] = ext_ref[r:r + span, :]

    for r0 in range(0, tc, CONV_ROWS):
        acc = jnp.zeros((CONV_ROWS, CONV_CH), F32)
        for k in range(CONV_WIDTH):
            r = (lead + k) % SUBLANE
            i0 = r0 + lead + k - r
            rows = ext_ref[i0:i0 + CONV_ROWS, :] if r == 0 else sh_ref[r - 1, i0:i0 + CONV_ROWS, :]
            acc = acc + w_ref[k:k + 1, :] * rows
        o_ref[0, r0:r0 + CONV_ROWS, :] = _conv_ln_silu(acc, b_ref[...], lg_ref[...], lb_ref[...]).astype(BF16)


def _conv_prompt(a, w, b, lg, lb, *, tc):
    bsz, t, _ = a.shape
    hb = tc // CONV_HALO
    return pl.pallas_call(
        functools.partial(_conv_kernel, tc=tc),
        grid=(bsz, t // tc),
        in_specs=[pl.BlockSpec((1, CONV_HALO, CONV_CH), lambda bi, i: (bi, jnp.maximum(i * hb - 1, 0), 0)),
                  pl.BlockSpec((1, tc, CONV_CH), lambda bi, i: (bi, i, 0)),
                  _const_spec((CONV_WIDTH, CONV_CH)), _const_spec((1, CONV_CH)),
                  _const_spec((1, CONV_CH)), _const_spec((1, CONV_CH))],
        out_specs=pl.BlockSpec((1, tc, CONV_CH), lambda bi, i: (bi, i, 0)),
        out_shape=jax.ShapeDtypeStruct((bsz, t, CONV_CH), BF16),
        scratch_shapes=[pltpu.VMEM((CONV_HALO + tc, CONV_CH), F32),
                        pltpu.VMEM((SUBLANE - 1, CONV_HALO + tc - SUBLANE, CONV_CH), F32)],
        compiler_params=_params("parallel", "parallel"), name="conv_prompt",
    )(a, a, w, b, lg, lb)


def _compress_rows(h0, h1_next, pos_term, w2):
    hid = pos_term + h0 + h1_next
    return _dot(jax.nn.gelu(hid).astype(BF16), w2)


def _pos_term(posx_ref, wc_ref):
    hp = _dot(posx_ref[...], wc_ref[...])
    return hp[0:1, :KV_COLS] + hp[1:2, KV_COLS:]


def _compress_kernel(x_ref, wc_ref, posx_ref, w2_ref, kc_ref, vc_ref, pos_ref):
    hh = _dot(x_ref[0].astype(BF16), wc_ref[...])
    h1 = hh[:, KV_COLS:]
    h1_next = jnp.concatenate([h1[1:], jnp.zeros((1, KV_COLS), F32)], axis=0)
    pos_term = _pos_term(posx_ref, wc_ref)
    pos_ref[...] = jnp.broadcast_to(pos_term, pos_ref.shape)
    cmp = _compress_rows(hh[:, :KV_COLS], h1_next, pos_term, w2_ref[...])
    for g in range(KV_HEADS):
        kc_ref[0, g] = cmp[:, g * HEAD_DIM:(g + 1) * HEAD_DIM].astype(BF16)
        vc_ref[0, g] = cmp[:, (KV_HEADS + g) * HEAD_DIM:(KV_HEADS + g + 1) * HEAD_DIM].astype(BF16)


def _compress_prompt(kvc_chunks, wc, posx, w2):
    bsz, n_ch, _ = kvc_chunks.shape
    tok = jax.ShapeDtypeStruct((bsz, KV_HEADS, n_ch, HEAD_DIM), BF16)
    tok_spec = pl.BlockSpec((1, KV_HEADS, n_ch, HEAD_DIM), lambda bi: (bi, 0, 0, 0))
    return pl.pallas_call(
        _compress_kernel,
        grid=(bsz,),
        in_specs=[pl.BlockSpec((1, n_ch, CHUNK_COLS), lambda bi: (bi, 0, 0)),
                  _const_spec(wc.shape), _const_spec(posx.shape), _const_spec(w2.shape)],
        out_specs=(tok_spec, tok_spec, pl.BlockSpec((8, KV_COLS), lambda bi: (0, 0))),
        out_shape=(tok, tok, jax.ShapeDtypeStruct((8, KV_COLS), F32)),
        compiler_params=_params("arbitrary"), name="compress_prompt",
    )(kvc_chunks, wc, posx, w2)


def _block_scores(imp, ssel, t_col, ns):
    s = _dot_exact_lhs(imp, ssel)
    j = lax.broadcasted_iota(jnp.int32, s.shape, 1)
    cur = t_col // SEL_BLOCK
    valid = j * SEL_BLOCK <= t_col
    forced = valid & ((j == 0) | (j == cur) | (j == cur - 1))
    s = jnp.where(forced, BIG, jnp.where(valid, s, -BIG))
    return jnp.where(j < ns, s, -3.0 * BIG)


def _nsa_prompt_kernel(q_ref, gate_ref, ksa_ref, vs_ref, kw_ref, vw_ref, kc_ref, vc_ref, ssel_ref, o_ref,
                       *, tq, ns):
    i = pl.program_id(1)
    q0 = i * tq
    t_col = q0 + lax.broadcasted_iota(jnp.int32, (tq, 1), 0)
    t4 = jnp.concatenate([t_col] * GROUP, axis=0)
    gates = gate_ref[...]
    n_blk = kc_ref.shape[2]
    end = lax.broadcasted_iota(jnp.int32, (1, n_blk), 1) * CMP_STRIDE + (CMP_LEN - 1)
    cmask = end <= t4
    w0 = pl.multiple_of(jnp.maximum(q0 - WINDOW, 0), Q_TILE)
    diff = t4 - (w0 + lax.broadcasted_iota(jnp.int32, (1, WIN_KEYS), 1))
    wmask = (diff >= 0) & (diff < WINDOW)
    sub = lax.broadcasted_iota(jnp.int32, (SUBLANE, tq), 0)
    q_aug, o_c, o_w = [], [], []
    for g in range(KV_HEADS):
        q4 = jnp.concatenate(
            [q_ref[:, (GROUP * g + h) * HEAD_DIM:(GROUP * g + h + 1) * HEAD_DIM] for h in range(GROUP)], axis=0)

        e_c, r_c = _softmax_parts(_dot_nt(q4, kc_ref[0, g]), cmask)
        o_c.append(_dot(e_c.astype(BF16), vc_ref[0, g]) * r_c)
        p_c = e_c * r_c
        imp = p_c[0:tq] + p_c[tq:2 * tq] + p_c[2 * tq:3 * tq] + p_c[3 * tq:4 * tq]

        s_t = _block_scores(imp, ssel_ref[...], t_col, ns).T
        blocks = [s_t[v * SUBLANE:(v + 1) * SUBLANE] for v in range(ns // SUBLANE)]
        ranks = [jnp.zeros((SUBLANE, tq), F32) for _ in blocks]
        for i2 in range(ns):
            row = s_t[i2:i2 + 1, :]
            for v, blk in enumerate(blocks):
                if v > i2 // SUBLANE:
                    beats = row >= blk
                elif v < i2 // SUBLANE:
                    beats = row > blk
                else:
                    beats = (row > blk) | ((row == blk) & (sub > i2 % SUBLANE))
                ranks[v] = ranks[v] + jnp.where(beats, 1.0, 0.0)
        sel_t = jnp.concatenate([jnp.where(r < N_SELECT, 0.0, NEG) for r in ranks]
                                + [jnp.full((LANE - ns, tq), NEG, F32)], axis=0)
        sel = sel_t.T
        selbias = jnp.concatenate([sel[:, :HEAD_DIM].astype(BF16)] * GROUP, axis=0)
        q_aug.append(jnp.concatenate([q4, selbias], axis=1))

        e_w, r_w = _softmax_parts(_dot_nt(q4, kw_ref[g, pl.ds(w0, WIN_KEYS), :]), wmask)
        o_w.append(_dot(e_w.astype(BF16), vw_ref[g, pl.ds(w0, WIN_KEYS), :]) * r_w)

    def chunk(k0, carry, lo):
        out = []
        for g in range(KV_HEADS):
            m, l, acc = carry[g]
            s = _dot_nt(q_aug[g], ksa_ref[g, pl.ds(k0, SEL_CHUNK), :])
            if lo is not None:
                key = k0 + lax.broadcasted_iota(jnp.int32, (1, SEL_CHUNK), 1)
                s = jnp.where((key <= t4) & (key >= lo), s, NEG)
            m_new = jnp.maximum(m, jnp.max(s, axis=-1, keepdims=True))
            alpha = jnp.exp(m - m_new)
            p = jnp.exp(s - m_new)
            l = alpha * l + jnp.sum(p, axis=-1, keepdims=True)
            acc = alpha * acc + _dot(p.astype(BF16), vs_ref[g, pl.ds(k0, SEL_CHUNK), :])
            out.append((m_new, l, acc))
        return tuple(out)

    init = tuple((jnp.full((GROUP * tq, 1), NEG, F32), jnp.zeros((GROUP * tq, 1), F32),
                  jnp.zeros((GROUP * tq, HEAD_DIM), F32)) for _ in range(KV_HEADS))
    n_full = q0 // SEL_CHUNK
    carry = lax.fori_loop(0, n_full, lambda c, cr: chunk(pl.multiple_of(c * SEL_CHUNK, SEL_CHUNK), cr, None), init)
    tail0 = pl.multiple_of(jnp.maximum(q0 + tq - SEL_CHUNK, 0), Q_TILE)
    carry = chunk(tail0, carry, n_full * SEL_CHUNK)

    for g in range(KV_HEADS):
        _, l_s, acc_s = carry[g]
        o_s = acc_s * (1.0 / l_s)
        for h in range(GROUP):
            hh = GROUP * g + h
            r = slice(h * tq, (h + 1) * tq)
            c = N_BRANCH * hh
            o = gates[:, c:c + 1] * o_c[g][r] + gates[:, c + 1:c + 2] * o_s[r] + gates[:, c + 2:c + 3] * o_w[g][r]
            o_ref[:, hh * HEAD_DIM:(hh + 1) * HEAD_DIM] = o.astype(BF16)


def _nsa_prompt(q, gates, ksa, vs, kw, vw, kc, vc, ssel, *, bsz, seq_len):
    tq = Q_TILE
    nt = seq_len // tq
    ns = seq_len // SEL_BLOCK
    n_blk = kc.shape[2]
    row = lambda c: pl.BlockSpec((tq, c), lambda bi, i: (bi * nt + i, 0))
    seq = lambda c: pl.BlockSpec((KV_HEADS, seq_len, c), lambda bi, i: (0, bi, 0))
    tok = pl.BlockSpec((1, KV_HEADS, n_blk, HEAD_DIM), lambda bi, i: (bi, 0, 0, 0))
    return pl.pallas_call(
        functools.partial(_nsa_prompt_kernel, tq=tq, ns=ns),
        grid=(bsz, nt),
        in_specs=[row(ATTN_WIDTH), row(GATE_PAD), seq(2 * HEAD_DIM), seq(HEAD_DIM), seq(HEAD_DIM), seq(HEAD_DIM),
                  tok, tok, _const_spec(ssel.shape)],
        out_specs=row(ATTN_WIDTH),
        out_shape=jax.ShapeDtypeStruct((bsz * seq_len, ATTN_WIDTH), BF16),
        compiler_params=_params("parallel", "parallel"), name="nsa_prompt",
    )(q, gates, ksa, vs, kw, vw, kc, vc, ssel)


def _swiglu_residual(x1, gf, wg_ref, wu_ref, wd_ref):
    h = _rms(x1, gf).astype(BF16)
    act = jax.nn.silu(_dot(h, wg_ref[...])) * _dot(h, wu_ref[...])
    return x1 + _dot(act.astype(BF16), wd_ref[...])


def _ffn0_kernel(x_ref, a_ref, c_ref, woa_ref, woc_ref, gf_ref, wg_ref, wu_ref, wd_ref, o_ref):
    x1 = x_ref[...] + (_dot(a_ref[...], woa_ref[...]) + _dot(c_ref[...], woc_ref[...]))
    o_ref[...] = _swiglu_residual(x1, gf_ref[...], wg_ref, wu_ref, wd_ref)


def _ffn0(x, a, c, woa, woc, gf, wg, wu, wd, *, tm):
    m = x.shape[0]
    row = lambda cols: pl.BlockSpec((tm, cols), lambda i: (i, 0))
    return pl.pallas_call(
        _ffn0_kernel,
        grid=(m // tm,),
        in_specs=[row(D_MODEL), row(ATTN_WIDTH), row(CONV_CH), _const_spec(woa.shape), _const_spec(woc.shape),
                  _const_spec(gf.shape), _const_spec(wg.shape), _const_spec(wu.shape), _const_spec(wd.shape)],
        out_specs=row(D_MODEL), out_shape=jax.ShapeDtypeStruct((m, D_MODEL), F32),
        compiler_params=_params("parallel"), name="outproj_ffn",
    )(x, a, c, woa, woc, gf, wg, wu, wd)


def _pool_mix(x, h, win_sums, cnts, pw_ref, ps):
    ys = []
    for g in range(len(POOL_WINDOWS)):
        z = win_sums[g] / cnts[g] - h[:, g * POOL_GROUP:(g + 1) * POOL_GROUP]
        ys.append(_dot(z.astype(BF16), pw_ref[g]))
    return x + jnp.concatenate(ys, axis=1) * ps


def _layer1_prompt_kernel(xprev_ref, x_ref, gm_ref, pw_ref, ps_ref, gf_ref, wg_ref, wu_ref, wd_ref, gfin_ref,
                          y_ref, hst_ref, *, tm):
    i = pl.program_id(1)
    x = x_ref[0]
    h = _rms(x, gm_ref[...])
    hprev = jnp.where(i == 0, 0.0, _rms(xprev_ref[0], gm_ref[...]))
    hst_ref[0] = h[tm - POOL_HALO:, :]
    pos = i * tm + lax.broadcasted_iota(jnp.int32, (tm, 1), 0)
    sums, cnts = [], []
    for g, w in enumerate(POOL_WINDOWS):
        e = jnp.concatenate([hprev[:, g * POOL_GROUP:(g + 1) * POOL_GROUP],
                             h[:, g * POOL_GROUP:(g + 1) * POOL_GROUP]], axis=0)
        span = 1
        while span < w:
            e = e[span:] + e[:-span]
            span *= 2
        first = POOL_HALO - (w - 1)
        sums.append(e[first:first + tm])
        cnts.append(jnp.minimum(w, pos + 1).astype(F32))
    x1 = _pool_mix(x, h, sums, cnts, pw_ref, ps_ref[...])
    x2 = _swiglu_residual(x1, gf_ref[...], wg_ref, wu_ref, wd_ref)
    y_ref[0] = _rms(x2, gfin_ref[...])


def _layer1_prompt(x, gm, pw, ps, gf, wg, wu, wd, gfin, *, tm):
    bsz, t, _ = x.shape
    hb = tm // POOL_HALO
    return pl.pallas_call(
        functools.partial(_layer1_prompt_kernel, tm=tm),
        grid=(bsz, t // tm),
        in_specs=[pl.BlockSpec((1, POOL_HALO, D_MODEL), lambda bi, i: (bi, jnp.maximum(i * hb - 1, 0), 0)),
                  pl.BlockSpec((1, tm, D_MODEL), lambda bi, i: (bi, i, 0)),
                  _const_spec(gm.shape), _const_spec(pw.shape), _const_spec(ps.shape), _const_spec(gf.shape),
                  _const_spec(wg.shape), _const_spec(wu.shape), _const_spec(wd.shape), _const_spec(gfin.shape)],
        out_specs=(pl.BlockSpec((1, tm, D_MODEL), lambda bi, i: (bi, i, 0)),
                   pl.BlockSpec((1, POOL_HALO, D_MODEL), lambda bi, i: (bi, 0, 0))),
        out_shape=(jax.ShapeDtypeStruct((bsz, t, D_MODEL), F32),
                   jax.ShapeDtypeStruct((bsz, POOL_HALO, D_MODEL), F32)),
        compiler_params=_params("parallel", "arbitrary"), name="layer1_prompt",
    )(x, x, gm, pw, ps, gf, wg, wu, wd, gfin)


def _layer1_sample_kernel(x_ref, hist_ref, gm_ref, pw_ref, ps_ref, gf_ref, wg_ref, wu_ref, wd_ref, gfin_ref,
                          y_ref, h_ref, *, first_pos):
    x = x_ref[...]
    h = _rms(x, gm_ref[...])
    h_ref[...] = h
    sums, cnts = [], []
    for g, w in enumerate(POOL_WINDOWS):
        c = slice(g * POOL_GROUP, (g + 1) * POOL_GROUP)
        s = h[:, c]
        for k in range(1, w):
            s = s + hist_ref[POOL_BUF - k][:, c]
        sums.append(s)
        cnts.append(float(min(w, first_pos + 1)))
    x1 = _pool_mix(x, h, sums, cnts, pw_ref, ps_ref[...])
    x2 = _swiglu_residual(x1, gf_ref[...], wg_ref, wu_ref, wd_ref)
    y_ref[...] = _rms(x2, gfin_ref[...])


def _layer1_sample(x, hist, gm, pw, ps, gf, wg, wu, wd, gfin, *, first_pos):
    m = x.shape[0]
    args = (x, hist, gm, pw, ps, gf, wg, wu, wd, gfin)
    out = jax.ShapeDtypeStruct((m, D_MODEL), F32)
    return pl.pallas_call(
        functools.partial(_layer1_sample_kernel, first_pos=first_pos),
        grid=(1,),
        in_specs=[_const_spec(a.shape) for a in args],
        out_specs=(_const_spec((m, D_MODEL)), _const_spec((m, D_MODEL))), out_shape=(out, out),
        compiler_params=_params("arbitrary"), name="layer1_sample",
    )(*args)


def _page_copy(cache_ref, xt_ref, sem, phys, p):
    return pltpu.make_async_copy(cache_ref.at[phys], xt_ref.at[p], sem)


def _cmp_sample_kernel(pt_ref, q_ref, cache_ref, wc_ref, pos_ref, w2_ref, ssel_ref, oc_ref, idx_ref,
                       xt_ref, xl_ref, hbuf_ref, sem, *, n_pages, past, ns, nsp):
    b = pl.program_id(0)
    n_ch = n_pages * (PAGE_SIZE // CMP_STRIDE)

    def gather(sample):
        def start(p, carry):
            _page_copy(cache_ref, xt_ref, sem, pt_ref[sample * n_pages + p], p).start()
            return carry

        lax.fori_loop(0, n_pages, start, 0)

    def wait(p, carry):
        _page_copy(cache_ref, xt_ref, sem, 0, p).wait()
        return carry

    per_page = PAGE_SIZE // CMP_STRIDE
    r_out = lax.broadcasted_iota(jnp.int32, (PAGE_SIZE, PAGE_SIZE), 0)
    r_in = lax.broadcasted_iota(jnp.int32, (PAGE_SIZE, PAGE_SIZE), 1)
    perm = (r_in == (r_out % per_page) * CMP_STRIDE + r_out // per_page).astype(BF16)

    batch = 8

    def to_rows(i, carry):
        p0 = pl.multiple_of(i * batch, batch)
        pages = xt_ref[pl.ds(p0, batch)].reshape(batch * KV_COLS, PAGE_SIZE)
        xp = _dot_nt(perm, pages.astype(BF16))
        for k in range(batch):
            c0 = pl.multiple_of((p0 + k) * per_page, per_page)
            for l in range(CMP_STRIDE):
                xl_ref[l, pl.ds(c0, per_page), :] = xp[l * per_page:(l + 1) * per_page, k * KV_COLS:(k + 1) * KV_COLS]
        return carry

    @pl.when(b == 0)
    def _():
        gather(0)

    lax.fori_loop(0, n_pages, wait, 0)
    lax.fori_loop(0, n_pages // batch, to_rows, 0)

    @pl.when(b + 1 < pl.num_programs(0))
    def _():
        gather(b + 1)

    rows = min(256, n_ch)
    for r0 in range(0, n_ch, rows):
        chunk_rows = jnp.concatenate([xl_ref[l, r0:r0 + rows, :].astype(BF16) for l in range(CMP_STRIDE)], axis=1)
        hbuf_ref[r0:r0 + rows, :] = _dot(chunk_rows, wc_ref[...])
    hbuf_ref[n_ch:n_ch + 8, :] = jnp.zeros((8, 2 * KV_COLS), F32)
    cmp = _compress_rows(hbuf_ref[0:n_ch, :KV_COLS], hbuf_ref[pl.ds(1, n_ch), KV_COLS:],
                         pos_ref[0:1, :], w2_ref[...])

    q8 = q_ref[0]
    head = lax.broadcasted_iota(jnp.int32, (N_HEADS, 1), 0)
    end = lax.broadcasted_iota(jnp.int32, (1, n_ch), 1) * CMP_STRIDE + (CMP_LEN - 1)
    cmask = jnp.broadcast_to(end <= past, (N_HEADS, n_ch))
    o_c = jnp.zeros((N_HEADS, HEAD_DIM), F32)
    jl = lax.broadcasted_iota(jnp.int32, (1, nsp), 1)
    ii = lax.broadcasted_iota(jnp.int32, (nsp, nsp), 0)
    jj = lax.broadcasted_iota(jnp.int32, (nsp, nsp), 1)
    cur = past // SEL_BLOCK
    t_col = jnp.full((N_HEADS, 1), past, jnp.int32)
    for g in range(KV_HEADS):
        in_group = head // GROUP == g
        kc = cmp[:, g * HEAD_DIM:(g + 1) * HEAD_DIM].astype(BF16)
        vc = cmp[:, (KV_HEADS + g) * HEAD_DIM:(KV_HEADS + g + 1) * HEAD_DIM].astype(BF16)
        sc = _dot_nt(q8, kc)
        p_c = _softmax_rows(jnp.where(cmask, sc, NEG), cmask)
        o_c = jnp.where(in_group, _dot(p_c.astype(BF16), vc), o_c)
        imp = jnp.sum(jnp.where(in_group, p_c, 0.0), axis=0, keepdims=True)
        s_b = _block_scores(jnp.broadcast_to(imp, (N_HEADS, n_ch)), ssel_ref[...], t_col, ns)
        s_b = jnp.broadcast_to(s_b[0:1], (nsp, nsp))
        s_t = s_b.T
        beats = (s_t > s_b) | ((s_t == s_b) & (ii < jj))
        rank = jnp.sum(jnp.where(beats, 1.0, 0.0), axis=0, keepdims=True)
        sel = (rank < N_SELECT) & (jl != cur)
        sel_f = jnp.where(sel, 1.0, 0.0)
        before = _dot(jnp.broadcast_to(sel_f, (N_HEADS, nsp)).astype(BF16),
                      jnp.where(ii < jj, 1.0, 0.0).astype(BF16))[0:1]
        slot = lax.broadcasted_iota(jnp.int32, (N_SELECT, nsp), 0)
        pick = jnp.broadcast_to(sel, (N_SELECT, nsp)) & (jnp.broadcast_to(before, (N_SELECT, nsp)) == slot.astype(F32))
        blk = jnp.sum(jnp.where(pick, lax.broadcasted_iota(jnp.int32, (N_SELECT, nsp), 1), 0), axis=1, keepdims=True)
        idx_ref[0, g] = blk
    oc_ref[0] = o_c


def _cmp_sample(page_table, q8, cache_pages, wc, pos_term, w2, ssel, *, past, ns):
    bsz, n_pages = page_table.shape
    n_ch = n_pages * (PAGE_SIZE // CMP_STRIDE)
    nsp = ssel.shape[1]
    grid_spec = pltpu.PrefetchScalarGridSpec(
        num_scalar_prefetch=1, grid=(bsz,),
        in_specs=[pl.BlockSpec((1, N_HEADS, HEAD_DIM), lambda b, pt: (b, 0, 0)),
                  pl.BlockSpec(memory_space=pl.ANY),
                  _const_spec(wc.shape), _const_spec(pos_term.shape), _const_spec(w2.shape), _const_spec(ssel.shape)],
        out_specs=(pl.BlockSpec((1, N_HEADS, HEAD_DIM), lambda b, pt: (b, 0, 0)),
                   pl.BlockSpec((1, KV_HEADS, N_SELECT, 1), lambda b, pt: (b, 0, 0, 0))),
        scratch_shapes=[pltpu.VMEM((n_pages, KV_COLS, PAGE_SIZE), F32), pltpu.VMEM((CMP_STRIDE, n_ch, KV_COLS), F32),
                        pltpu.VMEM((n_ch + 8, 2 * KV_COLS), F32), pltpu.SemaphoreType.DMA(())],
    )
    return pl.pallas_call(
        functools.partial(_cmp_sample_kernel, n_pages=n_pages, past=past, ns=ns, nsp=nsp),
        grid_spec=grid_spec,
        out_shape=(jax.ShapeDtypeStruct((bsz, N_HEADS, HEAD_DIM), F32),
                   jax.ShapeDtypeStruct((bsz, KV_HEADS, N_SELECT, 1), jnp.int32)),
        compiler_params=_params("arbitrary"), name="cmp_sample",
    )(page_table.reshape(-1), q8, cache_pages, wc, pos_term, w2, ssel)


def _slab_copy(cache_ref, dst_ref, sem, src, g, k):
    return pltpu.make_async_copy(cache_ref.at[src], dst_ref.at[g, :, pl.ds(k * PAGE_SIZE, PAGE_SIZE)], sem)


def _mix_sample_kernel(idx_ref, pt_ref, q_ref, gate_ref, oc_ref, kvs_ref, kvw_ref, win_ref, cst_ref, a_ref,
                       cw_ref, cb_ref, lg_ref, lb_ref, cache_ref, att_ref, conv_ref, kt_ref, vt_ref, sem,
                       *, n_pages, n_gather, past, w_buf):
    b = pl.program_id(0)
    per_page = PAGE_SIZE // SEL_BLOCK
    halves = []
    for g in range(KV_HEADS):
        for k in range(n_gather):
            blk = idx_ref[(b * KV_HEADS + g) * N_SELECT + k]
            slab = pt_ref[b * n_pages + blk // per_page] * (2 * KV_HEADS) + g
            _slab_copy(cache_ref, kt_ref, sem, slab, g, k).start()
            _slab_copy(cache_ref, vt_ref, sem, slab + KV_HEADS, g, k).start()
            halves.append(blk % per_page)

    y = (jnp.sum(cw_ref[0:CONV_WIDTH - 1, :] * cst_ref[0], axis=0, keepdims=True)
         + cw_ref[CONV_WIDTH - 1:CONV_WIDTH, :] * a_ref[0])
    conv_ref[0] = _conv_ln_silu(y, cb_ref[...], lg_ref[...], lb_ref[...])

    def wait(s, carry):
        _slab_copy(cache_ref, kt_ref, sem, 0, 0, 0).wait()
        return carry

    lax.fori_loop(0, 2 * KV_HEADS * n_gather, wait, 0)

    q8 = q_ref[0]
    q8f = q8.astype(F32)
    head = lax.broadcasted_iota(jnp.int32, (N_HEADS, 1), 0)
    gates = gate_ref[0]
    kvs_new, kvw_new = kvs_ref[0], kvw_ref[0]
    jw = lax.broadcasted_iota(jnp.int32, (1, w_buf), 1)
    wdiff = w_buf - jw
    wmask = jnp.broadcast_to((wdiff < WINDOW) & (past - wdiff >= 0), (N_HEADS, w_buf))
    half_of_lane = lax.broadcasted_iota(jnp.int32, (1, PAGE_SIZE), 1) // SEL_BLOCK
    out = jnp.zeros((N_HEADS, HEAD_DIM), F32)

    def with_new_row(kt, vt, mask, k_new, v_new):
        s = jnp.where(mask, _dot(q8, kt.astype(BF16)), NEG)
        s_new = jnp.sum(q8f * k_new.astype(BF16).astype(F32), axis=-1, keepdims=True)
        m = jnp.maximum(jnp.max(s, axis=-1, keepdims=True), s_new)
        e = jnp.where(mask, jnp.exp(s - m), 0.0)
        e_new = jnp.exp(s_new - m)
        num = _dot_nt(e.astype(BF16), vt.astype(BF16)) + e_new * v_new
        return num / (jnp.sum(e, axis=-1, keepdims=True) + e_new)

    for g in range(KV_HEADS):
        kc, vc = slice(g * HEAD_DIM, (g + 1) * HEAD_DIM), slice((KV_HEADS + g) * HEAD_DIM, (KV_HEADS + g + 1) * HEAD_DIM)
        smask = jnp.concatenate([half_of_lane == halves[g * n_gather + k] for k in range(n_gather)], axis=1)
        smask = jnp.broadcast_to(smask, (N_HEADS, n_gather * PAGE_SIZE))
        o_s = with_new_row(kt_ref[g], vt_ref[g], smask, kvs_new[:, kc], kvs_new[:, vc])
        o_w = with_new_row(win_ref[0, g], win_ref[0, KV_HEADS + g], wmask, kvw_new[:, kc], kvw_new[:, vc])
        mixed = gates[:, 0:1] * oc_ref[0] + gates[:, 1:2] * o_s + gates[:, 2:3] * o_w
        out = jnp.where(head // GROUP == g, mixed, out)
    att_ref[0] = out


def _mix_sample(idx, page_table, q8, gates8, o_c, kvs, kvw, win, conv_state, a, cw, cb, lg, lb, cache_slabs,
                *, past, n_gather):
    bsz, n_pages = page_table.shape
    w_buf = win.shape[-1]
    one = lambda *shape: pl.BlockSpec((1,) + shape, lambda b, *_: (b,) + (0,) * len(shape))
    const = lambda shape: pl.BlockSpec(shape, lambda b, *_: (0,) * len(shape))
    gathered = pltpu.VMEM((KV_HEADS, HEAD_DIM, n_gather * PAGE_SIZE), F32)
    grid_spec = pltpu.PrefetchScalarGridSpec(
        num_scalar_prefetch=2, grid=(bsz,),
        in_specs=[one(N_HEADS, HEAD_DIM), one(N_HEADS, N_BRANCH), one(N_HEADS, HEAD_DIM), one(1, KV_COLS),
                  one(1, KV_COLS), one(2 * KV_HEADS, HEAD_DIM, w_buf), one(CONV_WIDTH - 1, CONV_CH), one(1, CONV_CH),
                  const(cw.shape), const(cb.shape), const(lg.shape), const(lb.shape),
                  pl.BlockSpec(memory_space=pl.ANY)],
        out_specs=(one(N_HEADS, HEAD_DIM), one(1, CONV_CH)),
        scratch_shapes=[gathered, gathered, pltpu.SemaphoreType.DMA(())],
    )
    return pl.pallas_call(
        functools.partial(_mix_sample_kernel, n_pages=n_pages, n_gather=n_gather, past=past, w_buf=w_buf),
        grid_spec=grid_spec,
        out_shape=(jax.ShapeDtypeStruct((bsz, N_HEADS, HEAD_DIM), F32),
                   jax.ShapeDtypeStruct((bsz, 1, CONV_CH), F32)),
        compiler_params=_params("arbitrary"), name="mix_sample",
    )(idx.reshape(-1), page_table.reshape(-1), q8, gates8, o_c, kvs, kvw, win, conv_state, a, cw, cb, lg, lb,
      cache_slabs)


def _inproj_weight(w_in):
    off = ATTN_WIDTH + N_BRANCH * KV_COLS
    n_gate = N_HEADS * N_BRANCH
    gate = jnp.pad(w_in[:, off:off + n_gate], ((0, 0), (0, GATE_PAD - n_gate)))
    return jnp.concatenate([w_in[:, :off], gate, w_in[:, off + n_gate:]], axis=1).astype(BF16)


def _compress_weights(pos_k, w1_k, w2_k, pos_v, w1_v, w2_v):
    n_slot = 2 * KV_HEADS
    eye = jnp.eye(n_slot, dtype=F32)

    def place(per_slot):
        w = jnp.stack(per_slot, axis=0)
        full = jnp.einsum('jclde,jk->ljdcke', w, eye)
        return full.reshape(CHUNK_COLS, N_SUB * KV_COLS)

    w1k = w1_k.reshape(N_SUB, CMP_STRIDE, HEAD_DIM, HEAD_DIM)
    w1v = w1_v.reshape(N_SUB, CMP_STRIDE, HEAD_DIM, HEAD_DIM)
    wc = place([w1k, w1k, w1v, w1v]).astype(BF16)
    pk = pos_k.reshape(N_SUB, CMP_STRIDE, 1, HEAD_DIM)
    pv = pos_v.reshape(N_SUB, CMP_STRIDE, 1, HEAD_DIM)
    posx = jnp.concatenate([pk, pk, pv, pv], axis=2).reshape(N_SUB, CHUNK_COLS)
    posx = jnp.pad(posx, ((0, 8 - N_SUB), (0, 0))).astype(BF16)
    w2 = jnp.einsum('jef,jk->jekf', jnp.stack([w2_k, w2_k, w2_v, w2_v]), eye).reshape(KV_COLS, KV_COLS).astype(BF16)
    return wc, posx, w2


def _selection_matrix(n_rows, n_cols):
    n = jnp.arange(n_rows)[:, None]
    j = jnp.arange(n_cols)[None, :]
    cnt = jnp.zeros((n_rows, n_cols), F32)
    for m in range(SEL_RATIO):
        for sub in range(N_SUB):
            cnt = cnt + (SEL_RATIO * j + m - sub == n).astype(F32)
    return cnt.astype(BF16)


def kernel(x_prompt, x_sample, cache_cmp_kv, cache_slc_kv, state_win_kv, state_conv, state_pool, page_table,
           norm_mix, norm_ffn, norm_final, w_in_a, w_out_a, cmp_pos_k, cmp_w1_k, cmp_w2_k, cmp_pos_v, cmp_w1_v,
           cmp_w2_v, conv_w, conv_b, conv_ln_g, conv_ln_b, pool_w, pool_scale, w_ffn_gate, w_ffn_up, w_ffn_down):
    bp, tp, _ = x_prompt.shape
    bs, s_new, _ = x_sample.shape
    n_pages = page_table.shape[1]
    past = n_pages * PAGE_SIZE
    w_buf = state_win_kv.shape[2]
    n_pool = cache_cmp_kv.shape[1]
    ns_p = tp // SEL_BLOCK
    ns_s = -(-(past + s_new) // SEL_BLOCK)
    assert s_new == 1 and tp % SEL_CHUNK == 0 and tp >= WIN_KEYS and N_SELECT <= ns_p <= HEAD_DIM
    assert ns_s > N_SELECT and norm_mix.shape[0] == 2 and ns_p % SUBLANE == 0

    row = lambda v: v.reshape(1, -1)
    w_in = _inproj_weight(w_in_a[0])
    wkvt = w_in_a[0, :, ATTN_WIDTH:ATTN_WIDTH + N_BRANCH * KV_COLS].T.astype(BF16)
    wc, posx, w2 = _compress_weights(cmp_pos_k[0], cmp_w1_k[0], cmp_w2_k[0], cmp_pos_v[0], cmp_w1_v[0], cmp_w2_v[0])
    woa, woc = w_out_a[0, :ATTN_WIDTH].astype(BF16), w_out_a[0, ATTN_WIDTH:].astype(BF16)
    wg, wu, wd = w_ffn_gate.astype(BF16), w_ffn_up.astype(BF16), w_ffn_down.astype(BF16)
    cw, cb, lg, lb = conv_w[0], row(conv_b[0]), row(conv_ln_g[0]), row(conv_ln_b[0])
    pw, ps = pool_w[0].astype(BF16), row(pool_scale[0])

    m = bp * tp
    xp = x_prompt.reshape(m, D_MODEL)
    q, kvc, kvs, kvw, gates, a, ksa, vs, kw, vw, kvct, kvst, kvwt = _inproj(
        xp, row(norm_mix[0]), w_in, wkvt, tm=512, seq_len=tp, t_len=tp)
    c_out = _conv_prompt(a.reshape(bp, tp, CONV_CH), cw, cb, lg, lb, tc=512)
    kc, vc, pos_term = _compress_prompt(kvc.reshape(bp, tp // CMP_STRIDE, CHUNK_COLS), wc, posx, w2)
    a_out = _nsa_prompt(q, gates, ksa, vs, kw, vw, kc, vc, _selection_matrix(tp // CMP_STRIDE, LANE),
                        bsz=bp, seq_len=tp)
    xp = _ffn0(xp, a_out, c_out.reshape(m, CONV_CH), woa, woc, row(norm_ffn[0]), wg[0], wu[0], wd[0], tm=512)
    y_prompt, pool_tail = _layer1_prompt(xp.reshape(bp, tp, D_MODEL), row(norm_mix[1]), pw, ps, row(norm_ffn[1]),
                                         wg[1], wu[1], wd[1], row(norm_final), tm=512)
    kv6 = lambda z, b: z.reshape(1, b, -1, 2, KV_HEADS, HEAD_DIM)
    kv6_t = lambda zt: jnp.transpose(zt.reshape(bp, 2, KV_HEADS, HEAD_DIM, tp), (0, 4, 1, 2, 3))[None]
    new_cmp_p, new_slc_p = kv6_t(kvct), kv6_t(kvst)
    new_win_p = kv6_t(kvwt)[:, :, -min(WINDOW, tp):]
    new_conv_p = a.reshape(1, bp, tp, CONV_CH)[:, :, -(CONV_WIDTH - 1):]
    new_pool_p = pool_tail[None, :, -POOL_BUF:]

    xs = x_sample.reshape(bs, D_MODEL)
    q, kvc, kvs, kvw, gates, a = _inproj(xs, row(norm_mix[0]), w_in, wkvt, tm=bs, seq_len=1, t_len=bs)[:6]
    q8 = q.reshape(bs, N_HEADS, HEAD_DIM)
    nsp = -(-ns_s // LANE) * LANE
    rows_last = lambda c: jnp.transpose(c, (0, 2, 3, 4, 1))
    o_c, idx = _cmp_sample(page_table, q8, rows_last(cache_cmp_kv[0]).reshape(n_pool, KV_COLS, PAGE_SIZE),
                           wc, pos_term, w2,
                           _selection_matrix(n_pages * (PAGE_SIZE // CMP_STRIDE), nsp), past=past, ns=ns_s)
    gates8 = gates[:, :N_HEADS * N_BRANCH].reshape(bs, N_HEADS, N_BRANCH)
    att, c_s = _mix_sample(idx, page_table, q8, gates8, o_c, kvs.reshape(bs, 1, KV_COLS), kvw.reshape(bs, 1, KV_COLS),
                           rows_last(state_win_kv[0]).reshape(bs, 2 * KV_HEADS, HEAD_DIM, w_buf), state_conv[0],
                           a.reshape(bs, 1, CONV_CH), cw, cb, lg, lb,
                           rows_last(cache_slc_kv[0]).reshape(n_pool * 2 * KV_HEADS, HEAD_DIM, PAGE_SIZE),
                           past=past, n_gather=N_SELECT - 1)
    xs = _ffn0(xs, att.reshape(bs, ATTN_WIDTH).astype(BF16), c_s.reshape(bs, CONV_CH).astype(BF16), woa, woc,
               row(norm_ffn[0]), wg[0], wu[0], wd[0], tm=bs)
    y_sample, h_s = _layer1_sample(xs, jnp.swapaxes(state_pool[0], 0, 1), row(norm_mix[1]), pw, ps, row(norm_ffn[1]),
                                   wg[1], wu[1], wd[1], row(norm_final), first_pos=past)
    new_win_s = jnp.concatenate([state_win_kv[0], kv6(kvw, bs)[0]], axis=1)[None, :, -w_buf:]
    new_conv_s = jnp.concatenate([state_conv[0], a.reshape(bs, 1, CONV_CH)], axis=1)[None, :, -(CONV_WIDTH - 1):]
    new_pool_s = jnp.concatenate([state_pool[0], h_s[:, None, :]], axis=1)[None, :, -POOL_BUF:]

    return (y_prompt, y_sample.reshape(bs, s_new, D_MODEL), new_cmp_p, kv6(kvc, bs), new_slc_p, kv6(kvs, bs),
            new_win_p, new_win_s, new_conv_p, new_conv_s, new_pool_p, new_pool_s)
```

```python
import functools

import jax
import jax.numpy as jnp
from jax import lax
from jax.experimental import pallas as pl
from jax.experimental.pallas import tpu as pltpu

F32 = jnp.float32
BF16 = jnp.bfloat16

D_MODEL = 1024
N_HEADS = 8
HEAD_DIM = 64
KV_HEADS = 2
GROUP = N_HEADS // KV_HEADS
ATTN_WIDTH = N_HEADS * HEAD_DIM
KV_COLS = 2 * KV_HEADS * HEAD_DIM
CMP_LEN = 32
CMP_STRIDE = 16
N_SUB = CMP_LEN // CMP_STRIDE
SEL_BLOCK = 64
SEL_RATIO = SEL_BLOCK // CMP_STRIDE
N_SELECT = 16
WINDOW = 512
PAGE_SIZE = 128
N_BRANCH = 3
SCALE = HEAD_DIM ** -0.5
CONV_CH = D_MODEL // 2
CONV_WIDTH = 31
POOL_WINDOWS = (2, 4, 8, 16)
POOL_GROUP = D_MODEL // len(POOL_WINDOWS)
POOL_BUF = max(POOL_WINDOWS) - 1
EPS = 1e-6
BIG = 1e9
NEG = -1e30
TINY = float(jnp.finfo(jnp.float32).tiny)

LANE = 128
SUBLANE = 8
CHUNK_COLS = CMP_STRIDE * KV_COLS
GATE_PAD = LANE
IN_COLS = ATTN_WIDTH + N_BRANCH * KV_COLS + GATE_PAD + 2 * CONV_CH
VMEM_LIMIT = 56 * 1024 * 1024

Q_TILE = 128
SEL_CHUNK = 512
WIN_KEYS = WINDOW + Q_TILE
ATT_ROWS = 64
CONV_HALO = 32
CONV_ROWS = 32
POOL_HALO = 16


def _dot(a, b):
    return jnp.dot(a, b, preferred_element_type=F32)


def _dot_nt(a, b):
    return lax.dot_general(a, b, (((1,), (1,)), ((), ())), preferred_element_type=F32)


def _dot_exact_lhs(a, b):
    hi = a.astype(BF16)
    r1 = a - hi.astype(F32)
    mid = r1.astype(BF16)
    lo = (r1 - mid.astype(F32)).astype(BF16)
    return _dot(hi, b) + _dot(mid, b) + _dot(lo, b)


def _rms(x, g):
    return x * lax.rsqrt(jnp.mean(x * x, axis=-1, keepdims=True) + EPS) * g


def _softmax_rows(s, mask):
    m = jnp.max(s, axis=-1, keepdims=True)
    e = jnp.where(mask, jnp.exp(s - m), 0.0)
    return e / jnp.maximum(jnp.sum(e, axis=-1, keepdims=True), TINY)


def _softmax_parts(s, mask):
    s = jnp.where(mask, s, NEG)
    e = jnp.where(mask, jnp.exp(s - jnp.max(s, axis=-1, keepdims=True)), 0.0)
    return e, 1.0 / jnp.maximum(jnp.sum(e, axis=-1, keepdims=True), TINY)


def _params(*sem):
    return pltpu.CompilerParams(dimension_semantics=sem, vmem_limit_bytes=VMEM_LIMIT)


def _const_spec(shape):
    nd = len(shape)
    return pl.BlockSpec(shape, lambda *_: (0,) * nd, pipeline_mode=pl.Buffered(1))


def _inproj_kernel(x_ref, g_ref, w_ref, wkvt_ref, q_ref, kvc_ref, kvs_ref, kvw_ref, gate_ref, a_ref,
                   ksa_ref, vs_ref, kw_ref, vw_ref, kvct_ref, kvst_ref, kvwt_ref, *, tm, seq_len):
    h = _rms(x_ref[...], g_ref[...])
    hb = h.astype(BF16)
    z = _dot(hb, w_ref[...])
    zt = _dot_nt(wkvt_ref[...], hb)
    kvct_ref[0] = zt[:KV_COLS]
    kvst_ref[0] = zt[KV_COLS:2 * KV_COLS]
    kvwt_ref[0] = zt[2 * KV_COLS:]
    off = ATTN_WIDTH
    q_ref[...] = (z[:, :off] * SCALE).astype(BF16)
    kvc_ref[...] = z[:, off:off + KV_COLS]
    kvs = z[:, off + KV_COLS:off + 2 * KV_COLS]
    kvw = z[:, off + 2 * KV_COLS:off + 3 * KV_COLS]
    kvs_ref[...] = kvs
    kvw_ref[...] = kvw
    off += 3 * KV_COLS
    gate_ref[...] = jax.nn.sigmoid(z[:, off:off + GATE_PAD])
    off += GATE_PAD
    a_ref[...] = z[:, off:off + CONV_CH] * jax.nn.sigmoid(z[:, off + CONV_CH:])
    pos = (pl.program_id(0) * tm + lax.broadcasted_iota(jnp.int32, (tm, HEAD_DIM), 0)) % seq_len
    lane = lax.broadcasted_iota(jnp.int32, (tm, HEAD_DIM), 1)
    onehot = (pos // SEL_BLOCK == lane).astype(BF16)
    ones_col = (lane == 0).astype(BF16)
    for g in range(KV_HEADS):
        k0, v0 = g * HEAD_DIM, (KV_HEADS + g) * HEAD_DIM
        ksa_ref[g] = jnp.concatenate([kvs[:, k0:k0 + HEAD_DIM].astype(BF16), onehot], axis=1)
        vs_ref[g] = jnp.concatenate([kvs[:, v0:v0 + HEAD_DIM].astype(BF16), ones_col], axis=1)
        kw_ref[g] = kvw[:, k0:k0 + HEAD_DIM].astype(BF16)
        vw_ref[g] = jnp.concatenate([kvw[:, v0:v0 + HEAD_DIM].astype(BF16), ones_col], axis=1)


def _inproj(x, g, w, wkvt, *, tm, seq_len, t_len):
    m = x.shape[0]
    per_row = t_len // tm
    row = lambda c: pl.BlockSpec((tm, c), lambda i: (i, 0))
    grp = lambda c: pl.BlockSpec((KV_HEADS, tm, c), lambda i: (0, i, 0))
    tr = pl.BlockSpec((1, KV_COLS, tm), lambda i: (i // per_row, 0, i % per_row))
    kvt = jax.ShapeDtypeStruct((m // t_len, KV_COLS, t_len), F32)
    out_shape = (
        jax.ShapeDtypeStruct((m, ATTN_WIDTH), BF16),
        jax.ShapeDtypeStruct((m, KV_COLS), F32), jax.ShapeDtypeStruct((m, KV_COLS), F32),
        jax.ShapeDtypeStruct((m, KV_COLS), F32),
        jax.ShapeDtypeStruct((m, GATE_PAD), F32), jax.ShapeDtypeStruct((m, CONV_CH), F32),
        jax.ShapeDtypeStruct((KV_HEADS, m, 2 * HEAD_DIM), BF16),
        jax.ShapeDtypeStruct((KV_HEADS, m, 2 * HEAD_DIM), BF16),
        jax.ShapeDtypeStruct((KV_HEADS, m, HEAD_DIM), BF16),
        jax.ShapeDtypeStruct((KV_HEADS, m, 2 * HEAD_DIM), BF16),
        kvt, kvt, kvt,
    )
    out_specs = (row(ATTN_WIDTH), row(KV_COLS), row(KV_COLS), row(KV_COLS), row(GATE_PAD), row(CONV_CH),
                 grp(2 * HEAD_DIM), grp(2 * HEAD_DIM), grp(HEAD_DIM), grp(2 * HEAD_DIM), tr, tr, tr)
    return pl.pallas_call(
        functools.partial(_inproj_kernel, tm=tm, seq_len=seq_len),
        grid=(m // tm,),
        in_specs=[row(D_MODEL), _const_spec((1, D_MODEL)), _const_spec((D_MODEL, IN_COLS)),
                  _const_spec((N_BRANCH * KV_COLS, D_MODEL))],
        out_specs=out_specs, out_shape=out_shape,
        compiler_params=_params("parallel"), name="inproj",
    )(x, g, w, wkvt)


def _conv_ln_silu(y, b, lg, lb):
    y = y + b
    mu = jnp.mean(y, axis=-1, keepdims=True)
    var = jnp.mean(jnp.square(y - mu), axis=-1, keepdims=True)
    return jax.nn.silu((y - mu) * lax.rsqrt(var + EPS) * lg + lb)


def _conv_kernel(prev_ref, a_ref, w_ref, b_ref, lg_ref, lb_ref, o_ref, ext_ref, sh_ref, *, tc):
    first = pl.program_id(1) == 0
    ext_ref[0:CONV_HALO, :] = jnp.where(first, 0.0, prev_ref[0])
    ext_ref[CONV_HALO:CONV_HALO + tc, :] = a_ref[0]
    lead = CONV_HALO - (CONV_WIDTH - 1)
    span = sh_ref.shape[1]
    for r in range(1, SUBLANE):
        sh_ref[r - 1] = ext_ref[r:r + span, :]

    for r0 in range(0, tc, CONV_ROWS):
        acc = jnp.zeros((CONV_ROWS, CONV_CH), F32)
        for k in range(CONV_WIDTH):
            r = (lead + k) % SUBLANE
            i0 = r0 + lead + k - r
            rows = ext_ref[i0:i0 + CONV_ROWS, :] if r == 0 else sh_ref[r - 1, i0:i0 + CONV_ROWS, :]
            acc = acc + w_ref[k:k + 1, :] * rows
        o_ref[0, r0:r0 + CONV_ROWS, :] = _conv_ln_silu(acc, b_ref[...], lg_ref[...], lb_ref[...]).astype(BF16)


def _conv_prompt(a, w, b, lg, lb, *, tc):
    bsz, t, _ = a.shape
    hb = tc // CONV_HALO
    return pl.pallas_call(
        functools.partial(_conv_kernel, tc=tc),
        grid=(bsz, t // tc),
        in_specs=[pl.BlockSpec((1, CONV_HALO, CONV_CH), lambda bi, i: (bi, jnp.maximum(i * hb - 1, 0), 0)),
                  pl.BlockSpec((1, tc, CONV_CH), lambda bi, i: (bi, i, 0)),
                  _const_spec((CONV_WIDTH, CONV_CH)), _const_spec((1, CONV_CH)),
                  _const_spec((1, CONV_CH)), _const_spec((1, CONV_CH))],
        out_specs=pl.BlockSpec((1, tc, CONV_CH), lambda bi, i: (bi, i, 0)),
        out_shape=jax.ShapeDtypeStruct((bsz, t, CONV_CH), BF16),
        scratch_shapes=[pltpu.VMEM((CONV_HALO + tc, CONV_CH), F32),
                        pltpu.VMEM((SUBLANE - 1, CONV_HALO + tc - SUBLANE, CONV_CH), F32)],
        compiler_params=_params("parallel", "parallel"), name="conv_prompt",
    )(a, a, w, b, lg, lb)


def _compress_rows(h0, h1_next, pos_term, w2):
    hid = pos_term + h0 + h1_next
    return _dot(jax.nn.gelu(hid).astype(BF16), w2)


def _pos_term(posx_ref, wc_ref):
    hp = _dot(posx_ref[...], wc_ref[...])
    return hp[0:1, :KV_COLS] + hp[1:2, KV_COLS:]


def _compress_kernel(x_ref, wc_ref, posx_ref, w2_ref, kc_ref, vc_ref, pos_ref):
    hh = _dot(x_ref[0].astype(BF16), wc_ref[...])
    h1 = hh[:, KV_COLS:]
    h1_next = jnp.concatenate([h1[1:], jnp.zeros((1, KV_COLS), F32)], axis=0)
    pos_term = _pos_term(posx_ref, wc_ref)
    pos_ref[...] = jnp.broadcast_to(pos_term, pos_ref.shape)
    cmp = _compress_rows(hh[:, :KV_COLS], h1_next, pos_term, w2_ref[...])
    for g in range(KV_HEADS):
        kc_ref[0, g] = cmp[:, g * HEAD_DIM:(g + 1) * HEAD_DIM].astype(BF16)
        vc_ref[0, g] = cmp[:, (KV_HEADS + g) * HEAD_DIM:(KV_HEADS + g + 1) * HEAD_DIM].astype(BF16)


def _compress_prompt(kvc_chunks, wc, posx, w2):
    bsz, n_ch, _ = kvc_chunks.shape
    tok = jax.ShapeDtypeStruct((bsz, KV_HEADS, n_ch, HEAD_DIM), BF16)
    tok_spec = pl.BlockSpec((1, KV_HEADS, n_ch, HEAD_DIM), lambda bi: (bi, 0, 0, 0))
    return pl.pallas_call(
        _compress_kernel,
        grid=(bsz,),
        in_specs=[pl.BlockSpec((1, n_ch, CHUNK_COLS), lambda bi: (bi, 0, 0)),
                  _const_spec(wc.shape), _const_spec(posx.shape), _const_spec(w2.shape)],
        out_specs=(tok_spec, tok_spec, pl.BlockSpec((8, KV_COLS), lambda bi: (0, 0))),
        out_shape=(tok, tok, jax.ShapeDtypeStruct((8, KV_COLS), F32)),
        compiler_params=_params("arbitrary"), name="compress_prompt",
    )(kvc_chunks, wc, posx, w2)


def _block_scores(imp, ssel, t_col, ns):
    s = _dot_exact_lhs(imp, ssel)
    j = lax.broadcasted_iota(jnp.int32, s.shape, 1)
    cur = t_col // SEL_BLOCK
    valid = j * SEL_BLOCK <= t_col
    forced = valid & ((j == 0) | (j == cur) | (j == cur - 1))
    s = jnp.where(forced, BIG, jnp.where(valid, s, -BIG))
    return jnp.where(j < ns, s, -3.0 * BIG)


def _nsa_prompt_kernel(q_ref, gate_ref, ksa_ref, vs_ref, kw_ref, vw_ref, kc_ref, vc_ref, ssel_ref, o_ref,
                       qa_ref, s_ref, p_ref, part_ref, m_ref, acc_ref, *, tq, ns):
    i = pl.program_id(1)
    q0 = i * tq
    t_col = q0 + lax.broadcasted_iota(jnp.int32, (tq, 1), 0)
    t4 = jnp.concatenate([t_col] * GROUP, axis=0)
    gates = gate_ref[...]
    n_blk = kc_ref.shape[2]
    end = lax.broadcasted_iota(jnp.int32, (1, n_blk), 1) * CMP_STRIDE + (CMP_LEN - 1)
    cmask = end <= t4
    sub = lax.broadcasted_iota(jnp.int32, (SUBLANE, tq), 0)
    o_c = []
    for g in range(KV_HEADS):
        q4 = jnp.concatenate(
            [q_ref[:, (GROUP * g + h) * HEAD_DIM:(GROUP * g + h + 1) * HEAD_DIM] for h in range(GROUP)], axis=0)

        e_c, r_c = _softmax_parts(_dot_nt(q4, kc_ref[0, g]), cmask)
        o_c.append(_dot(e_c.astype(BF16), vc_ref[0, g]) * r_c)
        p_c = e_c * r_c
        imp = p_c[0:tq] + p_c[tq:2 * tq] + p_c[2 * tq:3 * tq] + p_c[3 * tq:4 * tq]

        s_t = _block_scores(imp, ssel_ref[...], t_col, ns).T
        blocks = [s_t[v * SUBLANE:(v + 1) * SUBLANE] for v in range(ns // SUBLANE)]
        ranks = [jnp.zeros((SUBLANE, tq), F32) for _ in blocks]
        for i2 in range(ns):
            row = s_t[i2:i2 + 1, :]
            for v, blk in enumerate(blocks):
                if v > i2 // SUBLANE:
                    beats = row >= blk
                elif v < i2 // SUBLANE:
                    beats = row > blk
                else:
                    beats = (row > blk) | ((row == blk) & (sub > i2 % SUBLANE))
                ranks[v] = ranks[v] + jnp.where(beats, 1.0, 0.0)
        sel_t = jnp.concatenate([jnp.where(r < N_SELECT, 0.0, NEG) for r in ranks]
                                + [jnp.full((LANE - ns, tq), NEG, F32)], axis=0)
        sel = sel_t.T
        selbias = jnp.concatenate([sel[:, :HEAD_DIM].astype(BF16)] * GROUP, axis=0)
        qa_ref[g] = jnp.concatenate([q4, selbias], axis=1)

    rows = GROUP * tq
    n_rb = rows // ATT_ROWS

    def scores(g, k_ref, k0, width, k_cols, key_ok):
        s_ref[g, :, 0:width] = _dot_nt(qa_ref[g, :, 0:k_cols], k_ref[g, pl.ds(k0, width), :])

        for r0 in range(0, rows, ATT_ROWS):
            sb = s_ref[g, r0:r0 + ATT_ROWS, 0:width]
            if key_ok is not None:
                t = q0 + r0 % tq + lax.broadcasted_iota(jnp.int32, (ATT_ROWS, 1), 0)
                key = k0 + lax.broadcasted_iota(jnp.int32, (1, width), 1)
                sb = jnp.where(key_ok(t, key), sb, NEG)
                s_ref[g, r0:r0 + ATT_ROWS, 0:width] = sb
            mx = sb[:, 0:LANE]
            for j in range(1, width // LANE):
                mx = jnp.maximum(mx, sb[:, j * LANE:(j + 1) * LANE])
            part_ref[g, r0:r0 + ATT_ROWS, :] = mx
        return jnp.max(part_ref[g], axis=-1, keepdims=True)

    def weights(g, width, m_rep_ref):
        for r0 in range(0, rows, ATT_ROWS):
            m = m_rep_ref[g, r0:r0 + ATT_ROWS, :]
            sb = s_ref[g, r0:r0 + ATT_ROWS, 0:width]
            p_ref[g, r0:r0 + ATT_ROWS, 0:width] = jnp.exp(sb - jnp.concatenate([m] * (width // LANE), axis=1)).astype(BF16)

    w0 = pl.multiple_of(jnp.maximum(q0 - WINDOW, 0), Q_TILE)
    in_window = lambda t, key: (t - key >= 0) & (t - key < WINDOW)
    o_w = []
    for g in range(KV_HEADS):
        m_ref[g] = jnp.broadcast_to(scores(g, kw_ref, w0, WIN_KEYS, HEAD_DIM, in_window), (rows, LANE))
        weights(g, WIN_KEYS, m_ref)
        acc_w = _dot(p_ref[g, :, 0:WIN_KEYS], vw_ref[g, pl.ds(w0, WIN_KEYS), :])
        o_w.append(acc_w[:, :HEAD_DIM] * (1.0 / acc_w[:, HEAD_DIM:HEAD_DIM + 1]))

    for g in range(KV_HEADS):
        m_ref[g] = jnp.full((rows, LANE), NEG, F32)
        acc_ref[g] = jnp.zeros((rows, LANE), F32)

    def chunk(k0, key_ok):
        blk_max = [scores(g, ksa_ref, k0, SEL_CHUNK, 2 * HEAD_DIM, key_ok) for g in range(KV_HEADS)]
        for g in range(KV_HEADS):
            m_old = m_ref[g]
            m_new = jnp.maximum(m_old, blk_max[g])
            m_ref[g] = m_new
            weights(g, SEL_CHUNK, m_ref)
            acc_ref[g] = (jnp.exp(m_old - m_new) * acc_ref[g]
                          + _dot(p_ref[g, :, 0:SEL_CHUNK], vs_ref[g, pl.ds(k0, SEL_CHUNK), :]))

    n_full = q0 // SEL_CHUNK

    def full_chunk(c, carry):
        chunk(pl.multiple_of(c * SEL_CHUNK, SEL_CHUNK), None)
        return carry

    lax.fori_loop(0, n_full, full_chunk, 0)
    tail0 = pl.multiple_of(jnp.maximum(q0 + tq - SEL_CHUNK, 0), Q_TILE)
    chunk(tail0, lambda t, key: (key <= t) & (key >= n_full * SEL_CHUNK))

    for g in range(KV_HEADS):
        acc_s = acc_ref[g]
        o_s = acc_s[:, :HEAD_DIM] * (1.0 / acc_s[:, HEAD_DIM:HEAD_DIM + 1])
        for h in range(GROUP):
            hh = GROUP * g + h
            r = slice(h * tq, (h + 1) * tq)
            c = N_BRANCH * hh
            o = gates[:, c:c + 1] * o_c[g][r] + gates[:, c + 1:c + 2] * o_s[r] + gates[:, c + 2:c + 3] * o_w[g][r]
            o_ref[:, hh * HEAD_DIM:(hh + 1) * HEAD_DIM] = o.astype(BF16)


def _nsa_prompt(q, gates, ksa, vs, kw, vw, kc, vc, ssel, *, bsz, seq_len):
    tq = Q_TILE
    nt = seq_len // tq
    ns = seq_len // SEL_BLOCK
    n_blk = kc.shape[2]
    rows = GROUP * tq
    row = lambda c: pl.BlockSpec((tq, c), lambda bi, i: (bi * nt + i, 0))
    seq = lambda c: pl.BlockSpec((KV_HEADS, seq_len, c), lambda bi, i: (0, bi, 0))
    tok = pl.BlockSpec((1, KV_HEADS, n_blk, HEAD_DIM), lambda bi, i: (bi, 0, 0, 0))
    return pl.pallas_call(
        functools.partial(_nsa_prompt_kernel, tq=tq, ns=ns),
        grid=(bsz, nt),
        in_specs=[row(ATTN_WIDTH), row(GATE_PAD), seq(2 * HEAD_DIM), seq(2 * HEAD_DIM), seq(HEAD_DIM), seq(2 * HEAD_DIM),
                  tok, tok, _const_spec(ssel.shape)],
        out_specs=row(ATTN_WIDTH),
        out_shape=jax.ShapeDtypeStruct((bsz * seq_len, ATTN_WIDTH), BF16),
        scratch_shapes=[pltpu.VMEM((KV_HEADS, rows, 2 * HEAD_DIM), BF16), pltpu.VMEM((KV_HEADS, rows, WIN_KEYS), F32),
                        pltpu.VMEM((KV_HEADS, rows, WIN_KEYS), BF16), pltpu.VMEM((KV_HEADS, rows, LANE), F32),
                        pltpu.VMEM((KV_HEADS, rows, LANE), F32), pltpu.VMEM((KV_HEADS, rows, LANE), F32)],
        compiler_params=_params("parallel", "parallel"), name="nsa_prompt",
    )(q, gates, ksa, vs, kw, vw, kc, vc, ssel)


def _swiglu_residual(x1, gf, wg_ref, wu_ref, wd_ref):
    h = _rms(x1, gf).astype(BF16)
    act = jax.nn.silu(_dot(h, wg_ref[...])) * _dot(h, wu_ref[...])
    return x1 + _dot(act.astype(BF16), wd_ref[...])


def _ffn0_kernel(x_ref, a_ref, c_ref, woa_ref, woc_ref, gf_ref, wg_ref, wu_ref, wd_ref, o_ref):
    x1 = x_ref[...] + (_dot(a_ref[...], woa_ref[...]) + _dot(c_ref[...], woc_ref[...]))
    o_ref[...] = _swiglu_residual(x1, gf_ref[...], wg_ref, wu_ref, wd_ref)


def _ffn0(x, a, c, woa, woc, gf, wg, wu, wd, *, tm):
    m = x.shape[0]
    row = lambda cols: pl.BlockSpec((tm, cols), lambda i: (i, 0))
    return pl.pallas_call(
        _ffn0_kernel,
        grid=(m // tm,),
        in_specs=[row(D_MODEL), row(ATTN_WIDTH), row(CONV_CH), _const_spec(woa.shape), _const_spec(woc.shape),
                  _const_spec(gf.shape), _const_spec(wg.shape), _const_spec(wu.shape), _const_spec(wd.shape)],
        out_specs=row(D_MODEL), out_shape=jax.ShapeDtypeStruct((m, D_MODEL), F32),
        compiler_params=_params("parallel"), name="outproj_ffn",
    )(x, a, c, woa, woc, gf, wg, wu, wd)


def _pool_mix(x, h, win_sums, cnts, pw_ref, ps):
    ys = []
    for g in range(len(POOL_WINDOWS)):
        z = win_sums[g] / cnts[g] - h[:, g * POOL_GROUP:(g + 1) * POOL_GROUP]
        ys.append(_dot(z.astype(BF16), pw_ref[g]))
    return x + jnp.concatenate(ys, axis=1) * ps


def _layer1_prompt_kernel(xprev_ref, x_ref, gm_ref, pw_ref, ps_ref, gf_ref, wg_ref, wu_ref, wd_ref, gfin_ref,
                          y_ref, hst_ref, *, tm):
    i = pl.program_id(1)
    x = x_ref[0]
    h = _rms(x, gm_ref[...])
    hprev = jnp.where(i == 0, 0.0, _rms(xprev_ref[0], gm_ref[...]))
    hst_ref[0] = h[tm - POOL_HALO:, :]
    pos = i * tm + lax.broadcasted_iota(jnp.int32, (tm, 1), 0)
    sums, cnts = [], []
    for g, w in enumerate(POOL_WINDOWS):
        e = jnp.concatenate([hprev[:, g * POOL_GROUP:(g + 1) * POOL_GROUP],
                             h[:, g * POOL_GROUP:(g + 1) * POOL_GROUP]], axis=0)
        span = 1
        while span < w:
            e = e[span:] + e[:-span]
            span *= 2
        first = POOL_HALO - (w - 1)
        sums.append(e[first:first + tm])
        cnts.append(jnp.minimum(w, pos + 1).astype(F32))
    x1 = _pool_mix(x, h, sums, cnts, pw_ref, ps_ref[...])
    x2 = _swiglu_residual(x1, gf_ref[...], wg_ref, wu_ref, wd_ref)
    y_ref[0] = _rms(x2, gfin_ref[...])


def _layer1_prompt(x, gm, pw, ps, gf, wg, wu, wd, gfin, *, tm):
    bsz, t, _ = x.shape
    hb = tm // POOL_HALO
    return pl.pallas_call(
        functools.partial(_layer1_prompt_kernel, tm=tm),
        grid=(bsz, t // tm),
        in_specs=[pl.BlockSpec((1, POOL_HALO, D_MODEL), lambda bi, i: (bi, jnp.maximum(i * hb - 1, 0), 0)),
                  pl.BlockSpec((1, tm, D_MODEL), lambda bi, i: (bi, i, 0)),
                  _const_spec(gm.shape), _const_spec(pw.shape), _const_spec(ps.shape), _const_spec(gf.shape),
                  _const_spec(wg.shape), _const_spec(wu.shape), _const_spec(wd.shape), _const_spec(gfin.shape)],
        out_specs=(pl.BlockSpec((1, tm, D_MODEL), lambda bi, i: (bi, i, 0)),
                   pl.BlockSpec((1, POOL_HALO, D_MODEL), lambda bi, i: (bi, 0, 0))),
        out_shape=(jax.ShapeDtypeStruct((bsz, t, D_MODEL), F32),
                   jax.ShapeDtypeStruct((bsz, POOL_HALO, D_MODEL), F32)),
        compiler_params=_params("parallel", "arbitrary"), name="layer1_prompt",
    )(x, x, gm, pw, ps, gf, wg, wu, wd, gfin)


def _layer1_sample_kernel(x_ref, hist_ref, gm_ref, pw_ref, ps_ref, gf_ref, wg_ref, wu_ref, wd_ref, gfin_ref,
                          y_ref, h_ref, *, first_pos):
    x = x_ref[...]
    h = _rms(x, gm_ref[...])
    h_ref[...] = h
    sums, cnts = [], []
    for g, w in enumerate(POOL_WINDOWS):
        c = slice(g * POOL_GROUP, (g + 1) * POOL_GROUP)
        s = h[:, c]
        for k in range(1, w):
            s = s + hist_ref[POOL_BUF - k][:, c]
        sums.append(s)
        cnts.append(float(min(w, first_pos + 1)))
    x1 = _pool_mix(x, h, sums, cnts, pw_ref, ps_ref[...])
    x2 = _swiglu_residual(x1, gf_ref[...], wg_ref, wu_ref, wd_ref)
    y_ref[...] = _rms(x2, gfin_ref[...])


def _layer1_sample(x, hist, gm, pw, ps, gf, wg, wu, wd, gfin, *, first_pos):
    m = x.shape[0]
    args = (x, hist, gm, pw, ps, gf, wg, wu, wd, gfin)
    out = jax.ShapeDtypeStruct((m, D_MODEL), F32)
    return pl.pallas_call(
        functools.partial(_layer1_sample_kernel, first_pos=first_pos),
        grid=(1,),
        in_specs=[_const_spec(a.shape) for a in args],
        out_specs=(_const_spec((m, D_MODEL)), _const_spec((m, D_MODEL))), out_shape=(out, out),
        compiler_params=_params("arbitrary"), name="layer1_sample",
    )(*args)


def _page_copy(cache_ref, xt_ref, sem, phys, p):
    return pltpu.make_async_copy(cache_ref.at[phys], xt_ref.at[p], sem)


def _cmp_sample_kernel(pt_ref, q_ref, cache_ref, wc_ref, pos_ref, w2_ref, ssel_ref, oc_ref, idx_ref,
                       xt_ref, xl_ref, hbuf_ref, sem, *, n_pages, past, ns, nsp):
    b = pl.program_id(0)
    n_ch = n_pages * (PAGE_SIZE // CMP_STRIDE)

    def gather(sample):
        def start(p, carry):
            _page_copy(cache_ref, xt_ref, sem, pt_ref[sample * n_pages + p], p).start()
            return carry

        lax.fori_loop(0, n_pages, start, 0)

    def wait(p, carry):
        _page_copy(cache_ref, xt_ref, sem, 0, p).wait()
        return carry

    per_page = PAGE_SIZE // CMP_STRIDE
    r_out = lax.broadcasted_iota(jnp.int32, (PAGE_SIZE, PAGE_SIZE), 0)
    r_in = lax.broadcasted_iota(jnp.int32, (PAGE_SIZE, PAGE_SIZE), 1)
    perm = (r_in == (r_out % per_page) * CMP_STRIDE + r_out // per_page).astype(BF16)

    batch = 8

    def to_rows(i, carry):
        p0 = pl.multiple_of(i * batch, batch)
        pages = xt_ref[pl.ds(p0, batch)].reshape(batch * KV_COLS, PAGE_SIZE)
        xp = _dot_nt(perm, pages.astype(BF16))
        for k in range(batch):
            c0 = pl.multiple_of((p0 + k) * per_page, per_page)
            for l in range(CMP_STRIDE):
                xl_ref[l, pl.ds(c0, per_page), :] = xp[l * per_page:(l + 1) * per_page, k * KV_COLS:(k + 1) * KV_COLS]
        return carry

    @pl.when(b == 0)
    def _():
        gather(0)

    lax.fori_loop(0, n_pages, wait, 0)
    lax.fori_loop(0, n_pages // batch, to_rows, 0)

    @pl.when(b + 1 < pl.num_programs(0))
    def _():
        gather(b + 1)

    rows = min(256, n_ch)
    for r0 in range(0, n_ch, rows):
        chunk_rows = jnp.concatenate([xl_ref[l, r0:r0 + rows, :].astype(BF16) for l in range(CMP_STRIDE)], axis=1)
        hbuf_ref[r0:r0 + rows, :] = _dot(chunk_rows, wc_ref[...])
    hbuf_ref[n_ch:n_ch + 8, :] = jnp.zeros((8, 2 * KV_COLS), F32)
    cmp = _compress_rows(hbuf_ref[0:n_ch, :KV_COLS], hbuf_ref[pl.ds(1, n_ch), KV_COLS:],
                         pos_ref[0:1, :], w2_ref[...])

    q8 = q_ref[0]
    head = lax.broadcasted_iota(jnp.int32, (N_HEADS, 1), 0)
    end = lax.broadcasted_iota(jnp.int32, (1, n_ch), 1) * CMP_STRIDE + (CMP_LEN - 1)
    cmask = jnp.broadcast_to(end <= past, (N_HEADS, n_ch))
    o_c = jnp.zeros((N_HEADS, HEAD_DIM), F32)
    jl = lax.broadcasted_iota(jnp.int32, (1, nsp), 1)
    ii = lax.broadcasted_iota(jnp.int32, (nsp, nsp), 0)
    jj = lax.broadcasted_iota(jnp.int32, (nsp, nsp), 1)
    cur = past // SEL_BLOCK
    t_col = jnp.full((N_HEADS, 1), past, jnp.int32)
    for g in range(KV_HEADS):
        in_group = head // GROUP == g
        kc = cmp[:, g * HEAD_DIM:(g + 1) * HEAD_DIM].astype(BF16)
        vc = cmp[:, (KV_HEADS + g) * HEAD_DIM:(KV_HEADS + g + 1) * HEAD_DIM].astype(BF16)
        sc = _dot_nt(q8, kc)
        p_c = _softmax_rows(jnp.where(cmask, sc, NEG), cmask)
        o_c = jnp.where(in_group, _dot(p_c.astype(BF16), vc), o_c)
        imp = jnp.sum(jnp.where(in_group, p_c, 0.0), axis=0, keepdims=True)
        s_b = _block_scores(jnp.broadcast_to(imp, (N_HEADS, n_ch)), ssel_ref[...], t_col, ns)
        s_b = jnp.broadcast_to(s_b[0:1], (nsp, nsp))
        s_t = s_b.T
        beats = (s_t > s_b) | ((s_t == s_b) & (ii < jj))
        rank = jnp.sum(jnp.where(beats, 1.0, 0.0), axis=0, keepdims=True)
        sel = (rank < N_SELECT) & (jl != cur)
        sel_f = jnp.where(sel, 1.0, 0.0)
        before = _dot(jnp.broadcast_to(sel_f, (N_HEADS, nsp)).astype(BF16),
                      jnp.where(ii < jj, 1.0, 0.0).astype(BF16))[0:1]
        slot = lax.broadcasted_iota(jnp.int32, (N_SELECT, nsp), 0)
        pick = jnp.broadcast_to(sel, (N_SELECT, nsp)) & (jnp.broadcast_to(before, (N_SELECT, nsp)) == slot.astype(F32))
        blk = jnp.sum(jnp.where(pick, lax.broadcasted_iota(jnp.int32, (N_SELECT, nsp), 1), 0), axis=1, keepdims=True)
        idx_ref[0, g] = blk
    oc_ref[0] = o_c


def _cmp_sample(page_table, q8, cache_pages, wc, pos_term, w2, ssel, *, past, ns):
    bsz, n_pages = page_table.shape
    n_ch = n_pages * (PAGE_SIZE // CMP_STRIDE)
    nsp = ssel.shape[1]
    grid_spec = pltpu.PrefetchScalarGridSpec(
        num_scalar_prefetch=1, grid=(bsz,),
        in_specs=[pl.BlockSpec((1, N_HEADS, HEAD_DIM), lambda b, pt: (b, 0, 0)),
                  pl.BlockSpec(memory_space=pl.ANY),
                  _const_spec(wc.shape), _const_spec(pos_term.shape), _const_spec(w2.shape), _const_spec(ssel.shape)],
        out_specs=(pl.BlockSpec((1, N_HEADS, HEAD_DIM), lambda b, pt: (b, 0, 0)),
                   pl.BlockSpec((1, KV_HEADS, N_SELECT, 1), lambda b, pt: (b, 0, 0, 0))),
        scratch_shapes=[pltpu.VMEM((n_pages, KV_COLS, PAGE_SIZE), F32), pltpu.VMEM((CMP_STRIDE, n_ch, KV_COLS), F32),
                        pltpu.VMEM((n_ch + 8, 2 * KV_COLS), F32), pltpu.SemaphoreType.DMA(())],
    )
    return pl.pallas_call(
        functools.partial(_cmp_sample_kernel, n_pages=n_pages, past=past, ns=ns, nsp=nsp),
        grid_spec=grid_spec,
        out_shape=(jax.ShapeDtypeStruct((bsz, N_HEADS, HEAD_DIM), F32),
                   jax.ShapeDtypeStruct((bsz, KV_HEADS, N_SELECT, 1), jnp.int32)),
        compiler_params=_params("arbitrary"), name="cmp_sample",
    )(page_table.reshape(-1), q8, cache_pages, wc, pos_term, w2, ssel)


def _slab_copy(cache_ref, dst_ref, sem, src, g, k):
    return pltpu.make_async_copy(cache_ref.at[src], dst_ref.at[g, :, pl.ds(k * PAGE_SIZE, PAGE_SIZE)], sem)


def _mix_sample_kernel(idx_ref, pt_ref, q_ref, gate_ref, oc_ref, kvs_ref, kvw_ref, win_ref, cst_ref, a_ref,
                       cw_ref, cb_ref, lg_ref, lb_ref, cache_ref, att_ref, conv_ref, kt_ref, vt_ref, sem,
                       *, n_pages, n_gather, past, w_buf):
    b = pl.program_id(0)
    per_page = PAGE_SIZE // SEL_BLOCK
    halves = []
    for g in range(KV_HEADS):
        for k in range(n_gather):
            blk = idx_ref[(b * KV_HEADS + g) * N_SELECT + k]
            slab = pt_ref[b * n_pages + blk // per_page] * (2 * KV_HEADS) + g
            _slab_copy(cache_ref, kt_ref, sem, slab, g, k).start()
            _slab_copy(cache_ref, vt_ref, sem, slab + KV_HEADS, g, k).start()
            halves.append(blk % per_page)

    y = (jnp.sum(cw_ref[0:CONV_WIDTH - 1, :] * cst_ref[0], axis=0, keepdims=True)
         + cw_ref[CONV_WIDTH - 1:CONV_WIDTH, :] * a_ref[0])
    conv_ref[0] = _conv_ln_silu(y, cb_ref[...], lg_ref[...], lb_ref[...])

    def wait(s, carry):
        _slab_copy(cache_ref, kt_ref, sem, 0, 0, 0).wait()
        return carry

    lax.fori_loop(0, 2 * KV_HEADS * n_gather, wait, 0)

    q8 = q_ref[0]
    q8f = q8.astype(F32)
    head = lax.broadcasted_iota(jnp.int32, (N_HEADS, 1), 0)
    gates = gate_ref[0]
    kvs_new, kvw_new = kvs_ref[0], kvw_ref[0]
    jw = lax.broadcasted_iota(jnp.int32, (1, w_buf), 1)
    wdiff = w_buf - jw
    wmask = jnp.broadcast_to((wdiff < WINDOW) & (past - wdiff >= 0), (N_HEADS, w_buf))
    half_of_lane = lax.broadcasted_iota(jnp.int32, (1, PAGE_SIZE), 1) // SEL_BLOCK
    out = jnp.zeros((N_HEADS, HEAD_DIM), F32)

    def with_new_row(kt, vt, mask, k_new, v_new):
        s = jnp.where(mask, _dot(q8, kt.astype(BF16)), NEG)
        s_new = jnp.sum(q8f * k_new.astype(BF16).astype(F32), axis=-1, keepdims=True)
        m = jnp.maximum(jnp.max(s, axis=-1, keepdims=True), s_new)
        e = jnp.where(mask, jnp.exp(s - m), 0.0)
        e_new = jnp.exp(s_new - m)
        num = _dot_nt(e.astype(BF16), vt.astype(BF16)) + e_new * v_new
        return num / (jnp.sum(e, axis=-1, keepdims=True) + e_new)

    for g in range(KV_HEADS):
        kc, vc = slice(g * HEAD_DIM, (g + 1) * HEAD_DIM), slice((KV_HEADS + g) * HEAD_DIM, (KV_HEADS + g + 1) * HEAD_DIM)
        smask = jnp.concatenate([half_of_lane == halves[g * n_gather + k] for k in range(n_gather)], axis=1)
        smask = jnp.broadcast_to(smask, (N_HEADS, n_gather * PAGE_SIZE))
        o_s = with_new_row(kt_ref[g], vt_ref[g], smask, kvs_new[:, kc], kvs_new[:, vc])
        o_w = with_new_row(win_ref[0, g], win_ref[0, KV_HEADS + g], wmask, kvw_new[:, kc], kvw_new[:, vc])
        mixed = gates[:, 0:1] * oc_ref[0] + gates[:, 1:2] * o_s + gates[:, 2:3] * o_w
        out = jnp.where(head // GROUP == g, mixed, out)
    att_ref[0] = out


def _mix_sample(idx, page_table, q8, gates8, o_c, kvs, kvw, win, conv_state, a, cw, cb, lg, lb, cache_slabs,
                *, past, n_gather):
    bsz, n_pages = page_table.shape
    w_buf = win.shape[-1]
    one = lambda *shape: pl.BlockSpec((1,) + shape, lambda b, *_: (b,) + (0,) * len(shape))
    const = lambda shape: pl.BlockSpec(shape, lambda b, *_: (0,) * len(shape))
    gathered = pltpu.VMEM((KV_HEADS, HEAD_DIM, n_gather * PAGE_SIZE), F32)
    grid_spec = pltpu.PrefetchScalarGridSpec(
        num_scalar_prefetch=2, grid=(bsz,),
        in_specs=[one(N_HEADS, HEAD_DIM), one(N_HEADS, N_BRANCH), one(N_HEADS, HEAD_DIM), one(1, KV_COLS),
                  one(1, KV_COLS), one(2 * KV_HEADS, HEAD_DIM, w_buf), one(CONV_WIDTH - 1, CONV_CH), one(1, CONV_CH),
                  const(cw.shape), const(cb.shape), const(lg.shape), const(lb.shape),
                  pl.BlockSpec(memory_space=pl.ANY)],
        out_specs=(one(N_HEADS, HEAD_DIM), one(1, CONV_CH)),
        scratch_shapes=[gathered, gathered, pltpu.SemaphoreType.DMA(())],
    )
    return pl.pallas_call(
        functools.partial(_mix_sample_kernel, n_pages=n_pages, n_gather=n_gather, past=past, w_buf=w_buf),
        grid_spec=grid_spec,
        out_shape=(jax.ShapeDtypeStruct((bsz, N_HEADS, HEAD_DIM), F32),
                   jax.ShapeDtypeStruct((bsz, 1, CONV_CH), F32)),
        compiler_params=_params("arbitrary"), name="mix_sample",
    )(idx.reshape(-1), page_table.reshape(-1), q8, gates8, o_c, kvs, kvw, win, conv_state, a, cw, cb, lg, lb,
      cache_slabs)


def _inproj_weight(w_in):
    off = ATTN_WIDTH + N_BRANCH * KV_COLS
    n_gate = N_HEADS * N_BRANCH
    gate = jnp.pad(w_in[:, off:off + n_gate], ((0, 0), (0, GATE_PAD - n_gate)))
    return jnp.concatenate([w_in[:, :off], gate, w_in[:, off + n_gate:]], axis=1).astype(BF16)


def _compress_weights(pos_k, w1_k, w2_k, pos_v, w1_v, w2_v):
    n_slot = 2 * KV_HEADS
    eye = jnp.eye(n_slot, dtype=F32)

    def place(per_slot):
        w = jnp.stack(per_slot, axis=0)
        full = jnp.einsum('jclde,jk->ljdcke', w, eye)
        return full.reshape(CHUNK_COLS, N_SUB * KV_COLS)

    w1k = w1_k.reshape(N_SUB, CMP_STRIDE, HEAD_DIM, HEAD_DIM)
    w1v = w1_v.reshape(N_SUB, CMP_STRIDE, HEAD_DIM, HEAD_DIM)
    wc = place([w1k, w1k, w1v, w1v]).astype(BF16)
    pk = pos_k.reshape(N_SUB, CMP_STRIDE, 1, HEAD_DIM)
    pv = pos_v.reshape(N_SUB, CMP_STRIDE, 1, HEAD_DIM)
    posx = jnp.concatenate([pk, pk, pv, pv], axis=2).reshape(N_SUB, CHUNK_COLS)
    posx = jnp.pad(posx, ((0, 8 - N_SUB), (0, 0))).astype(BF16)
    w2 = jnp.einsum('jef,jk->jekf', jnp.stack([w2_k, w2_k, w2_v, w2_v]), eye).reshape(KV_COLS, KV_COLS).astype(BF16)
    return wc, posx, w2


def _selection_matrix(n_rows, n_cols):
    n = jnp.arange(n_rows)[:, None]
    j = jnp.arange(n_cols)[None, :]
    cnt = jnp.zeros((n_rows, n_cols), F32)
    for m in range(SEL_RATIO):
        for sub in range(N_SUB):
            cnt = cnt + (SEL_RATIO * j + m - sub == n).astype(F32)
    return cnt.astype(BF16)


def kernel(x_prompt, x_sample, cache_cmp_kv, cache_slc_kv, state_win_kv, state_conv, state_pool, page_table,
           norm_mix, norm_ffn, norm_final, w_in_a, w_out_a, cmp_pos_k, cmp_w1_k, cmp_w2_k, cmp_pos_v, cmp_w1_v,
           cmp_w2_v, conv_w, conv_b, conv_ln_g, conv_ln_b, pool_w, pool_scale, w_ffn_gate, w_ffn_up, w_ffn_down):
    bp, tp, _ = x_prompt.shape
    bs, s_new, _ = x_sample.shape
    n_pages = page_table.shape[1]
    past = n_pages * PAGE_SIZE
    w_buf = state_win_kv.shape[2]
    n_pool = cache_cmp_kv.shape[1]
    ns_p = tp // SEL_BLOCK
    ns_s = -(-(past + s_new) // SEL_BLOCK)
    assert s_new == 1 and tp % SEL_CHUNK == 0 and tp >= WIN_KEYS and N_SELECT <= ns_p <= HEAD_DIM
    assert ns_s > N_SELECT and norm_mix.shape[0] == 2 and ns_p % SUBLANE == 0

    row = lambda v: v.reshape(1, -1)
    w_in = _inproj_weight(w_in_a[0])
    wkvt = w_in_a[0, :, ATTN_WIDTH:ATTN_WIDTH + N_BRANCH * KV_COLS].T.astype(BF16)
    wc, posx, w2 = _compress_weights(cmp_pos_k[0], cmp_w1_k[0], cmp_w2_k[0], cmp_pos_v[0], cmp_w1_v[0], cmp_w2_v[0])
    woa, woc = w_out_a[0, :ATTN_WIDTH].astype(BF16), w_out_a[0, ATTN_WIDTH:].astype(BF16)
    wg, wu, wd = w_ffn_gate.astype(BF16), w_ffn_up.astype(BF16), w_ffn_down.astype(BF16)
    cw, cb, lg, lb = conv_w[0], row(conv_b[0]), row(conv_ln_g[0]), row(conv_ln_b[0])
    pw, ps = pool_w[0].astype(BF16), row(pool_scale[0])

    m = bp * tp
    xp = x_prompt.reshape(m, D_MODEL)
    q, kvc, kvs, kvw, gates, a, ksa, vs, kw, vw, kvct, kvst, kvwt = _inproj(
        xp, row(norm_mix[0]), w_in, wkvt, tm=512, seq_len=tp, t_len=tp)
    c_out = _conv_prompt(a.reshape(bp, tp, CONV_CH), cw, cb, lg, lb, tc=512)
    kc, vc, pos_term = _compress_prompt(kvc.reshape(bp, tp // CMP_STRIDE, CHUNK_COLS), wc, posx, w2)
    a_out = _nsa_prompt(q, gates, ksa, vs, kw, vw, kc, vc, _selection_matrix(tp // CMP_STRIDE, LANE),
                        bsz=bp, seq_len=tp)
    xp = _ffn0(xp, a_out, c_out.reshape(m, CONV_CH), woa, woc, row(norm_ffn[0]), wg[0], wu[0], wd[0], tm=512)
    y_prompt, pool_tail = _layer1_prompt(xp.reshape(bp, tp, D_MODEL), row(norm_mix[1]), pw, ps, row(norm_ffn[1]),
                                         wg[1], wu[1], wd[1], row(norm_final), tm=512)
    kv6 = lambda z, b: z.reshape(1, b, -1, 2, KV_HEADS, HEAD_DIM)
    kv6_t = lambda zt: jnp.transpose(zt.reshape(bp, 2, KV_HEADS, HEAD_DIM, tp), (0, 4, 1, 2, 3))[None]
    new_cmp_p, new_slc_p = kv6_t(kvct), kv6_t(kvst)
    new_win_p = kv6_t(kvwt)[:, :, -min(WINDOW, tp):]
    new_conv_p = a.reshape(1, bp, tp, CONV_CH)[:, :, -(CONV_WIDTH - 1):]
    new_pool_p = pool_tail[None, :, -POOL_BUF:]

    xs = x_sample.reshape(bs, D_MODEL)
    q, kvc, kvs, kvw, gates, a = _inproj(xs, row(norm_mix[0]), w_in, wkvt, tm=bs, seq_len=1, t_len=bs)[:6]
    q8 = q.reshape(bs, N_HEADS, HEAD_DIM)
    nsp = -(-ns_s // LANE) * LANE
    rows_last = lambda c: jnp.transpose(c, (0, 2, 3, 4, 1))
    o_c, idx = _cmp_sample(page_table, q8, rows_last(cache_cmp_kv[0]).reshape(n_pool, KV_COLS, PAGE_SIZE),
                           wc, pos_term, w2,
                           _selection_matrix(n_pages * (PAGE_SIZE // CMP_STRIDE), nsp), past=past, ns=ns_s)
    gates8 = gates[:, :N_HEADS * N_BRANCH].reshape(bs, N_HEADS, N_BRANCH)
    att, c_s = _mix_sample(idx, page_table, q8, gates8, o_c, kvs.reshape(bs, 1, KV_COLS), kvw.reshape(bs, 1, KV_COLS),
                           rows_last(state_win_kv[0]).reshape(bs, 2 * KV_HEADS, HEAD_DIM, w_buf), state_conv[0],
                           a.reshape(bs, 1, CONV_CH), cw, cb, lg, lb,
                           rows_last(cache_slc_kv[0]).reshape(n_pool * 2 * KV_HEADS, HEAD_DIM, PAGE_SIZE),
                           past=past, n_gather=N_SELECT - 1)
    xs = _ffn0(xs, att.reshape(bs, ATTN_WIDTH).astype(BF16), c_s.reshape(bs, CONV_CH).astype(BF16), woa, woc,
               row(norm_ffn[0]), wg[0], wu[0], wd[0], tm=bs)
    y_sample, h_s = _layer1_sample(xs, jnp.swapaxes(state_pool[0], 0, 1), row(norm_mix[1]), pw, ps, row(norm_ffn[1]),
                                   wg[1], wu[1], wd[1], row(norm_final), first_pos=past)
    new_win_s = jnp.concatenate([state_win_kv[0], kv6(kvw, bs)[0]], axis=1)[None, :, -w_buf:]
    new_conv_s = jnp.concatenate([state_conv[0], a.reshape(bs, 1, CONV_CH)], axis=1)[None, :, -(CONV_WIDTH - 1):]
    new_pool_s = jnp.concatenate([state_pool[0], h_s[:, None, :]], axis=1)[None, :, -POOL_BUF:]

    return (y_prompt, y_sample.reshape(bs, s_new, D_MODEL), new_cmp_p, kv6(kvc, bs), new_slc_p, kv6(kvs, bs),
            new_win_p, new_win_s, new_conv_p, new_conv_s, new_pool_p, new_pool_s)
```

```python
import functools

import jax
import jax.numpy as jnp
from jax import lax
from jax.experimental import pallas as pl
from jax.experimental.pallas import tpu as pltpu

F32 = jnp.float32
BF16 = jnp.bfloat16

D_MODEL = 1024
N_HEADS = 8
HEAD_DIM = 64
KV_HEADS = 2
GROUP = N_HEADS // KV_HEADS
ATTN_WIDTH = N_HEADS * HEAD_DIM
KV_COLS = 2 * KV_HEADS * HEAD_DIM
CMP_LEN = 32
CMP_STRIDE = 16
N_SUB = CMP_LEN // CMP_STRIDE
SEL_BLOCK = 64
SEL_RATIO = SEL_BLOCK // CMP_STRIDE
N_SELECT = 16
WINDOW = 512
PAGE_SIZE = 128
N_BRANCH = 3
SCALE = HEAD_DIM ** -0.5
CONV_CH = D_MODEL // 2
CONV_WIDTH = 31
POOL_WINDOWS = (2, 4, 8, 16)
POOL_GROUP = D_MODEL // len(POOL_WINDOWS)
POOL_BUF = max(POOL_WINDOWS) - 1
EPS = 1e-6
BIG = 1e9
NEG = -1e30
TINY = float(jnp.finfo(jnp.float32).tiny)

LANE = 128
SUBLANE = 8
CHUNK_COLS = CMP_STRIDE * KV_COLS
GATE_PAD = LANE
IN_COLS = ATTN_WIDTH + N_BRANCH * KV_COLS + GATE_PAD + 2 * CONV_CH
VMEM_LIMIT = 56 * 1024 * 1024

Q_TILE = 128
SEL_CHUNK = 512
WIN_KEYS = WINDOW + Q_TILE
ATT_ROWS = 64
CONV_HALO = 32
CONV_ROWS = 32
POOL_HALO = 16


def _dot(a, b):
    return jnp.dot(a, b, preferred_element_type=F32)


def _dot_nt(a, b):
    return lax.dot_general(a, b, (((1,), (1,)), ((), ())), preferred_element_type=F32)


def _dot_exact_lhs(a, b):
    hi = a.astype(BF16)
    r1 = a - hi.astype(F32)
    mid = r1.astype(BF16)
    lo = (r1 - mid.astype(F32)).astype(BF16)
    return _dot(hi, b) + _dot(mid, b) + _dot(lo, b)


def _rms(x, g):
    return x * lax.rsqrt(jnp.mean(x * x, axis=-1, keepdims=True) + EPS) * g


def _softmax_rows(s, mask):
    m = jnp.max(s, axis=-1, keepdims=True)
    e = jnp.where(mask, jnp.exp(s - m), 0.0)
    return e / jnp.maximum(jnp.sum(e, axis=-1, keepdims=True), TINY)


def _softmax_parts(s, mask):
    s = jnp.where(mask, s, NEG)
    e = jnp.where(mask, jnp.exp(s - jnp.max(s, axis=-1, keepdims=True)), 0.0)
    return e, 1.0 / jnp.maximum(jnp.sum(e, axis=-1, keepdims=True), TINY)


def _params(*sem):
    return pltpu.CompilerParams(dimension_semantics=sem, vmem_limit_bytes=VMEM_LIMIT)


def _const_spec(shape):
    nd = len(shape)
    return pl.BlockSpec(shape, lambda *_: (0,) * nd, pipeline_mode=pl.Buffered(1))


def _inproj_kernel(x_ref, g_ref, w_ref, q_ref, kvc_ref, kvs_ref, kvw_ref, gate_ref, a_ref,
                   ksa_ref, vs_ref, kw_ref, vw_ref, *kvt_refs, tm, seq_len):
    h = _rms(x_ref[...], g_ref[...])
    z = _dot(h.astype(BF16), w_ref[...])
    off = ATTN_WIDTH
    for br, kvt_ref in enumerate(kvt_refs):
        kvt_ref[0] = z[:, off + br * KV_COLS:off + (br + 1) * KV_COLS].T
    q_ref[...] = (z[:, :off] * SCALE).astype(BF16)
    kvc_ref[...] = z[:, off:off + KV_COLS]
    kvs = z[:, off + KV_COLS:off + 2 * KV_COLS]
    kvw = z[:, off + 2 * KV_COLS:off + 3 * KV_COLS]
    kvs_ref[...] = kvs
    kvw_ref[...] = kvw
    off += 3 * KV_COLS
    gate_ref[...] = jax.nn.sigmoid(z[:, off:off + GATE_PAD])
    off += GATE_PAD
    a_ref[...] = z[:, off:off + CONV_CH] * jax.nn.sigmoid(z[:, off + CONV_CH:])
    pos = (pl.program_id(0) * tm + lax.broadcasted_iota(jnp.int32, (tm, HEAD_DIM), 0)) % seq_len
    lane = lax.broadcasted_iota(jnp.int32, (tm, HEAD_DIM), 1)
    onehot = (pos // SEL_BLOCK == lane).astype(BF16)
    ones_col = (lane == 0).astype(BF16)
    for g in range(KV_HEADS):
        k0, v0 = g * HEAD_DIM, (KV_HEADS + g) * HEAD_DIM
        ksa_ref[g] = jnp.concatenate([kvs[:, k0:k0 + HEAD_DIM].astype(BF16), onehot], axis=1)
        vs_ref[g] = jnp.concatenate([kvs[:, v0:v0 + HEAD_DIM].astype(BF16), ones_col], axis=1)
        kw_ref[g] = kvw[:, k0:k0 + HEAD_DIM].astype(BF16)
        vw_ref[g] = jnp.concatenate([kvw[:, v0:v0 + HEAD_DIM].astype(BF16), ones_col], axis=1)


def _inproj(x, g, w, *, tm, seq_len, transposed_kv):
    m = x.shape[0]
    per_row = seq_len // tm if transposed_kv else 1
    row = lambda c: pl.BlockSpec((tm, c), lambda i: (i, 0))
    grp = lambda c: pl.BlockSpec((KV_HEADS, tm, c), lambda i: (0, i, 0))
    tr = pl.BlockSpec((1, KV_COLS, tm), lambda i: (i // per_row, 0, i % per_row))
    kvt = jax.ShapeDtypeStruct((m // seq_len, KV_COLS, seq_len), F32)
    n_t = N_BRANCH if transposed_kv else 0
    out_shape = (
        jax.ShapeDtypeStruct((m, ATTN_WIDTH), BF16),
        jax.ShapeDtypeStruct((m, KV_COLS), F32), jax.ShapeDtypeStruct((m, KV_COLS), F32),
        jax.ShapeDtypeStruct((m, KV_COLS), F32),
        jax.ShapeDtypeStruct((m, GATE_PAD), F32), jax.ShapeDtypeStruct((m, CONV_CH), F32),
        jax.ShapeDtypeStruct((KV_HEADS, m, 2 * HEAD_DIM), BF16),
        jax.ShapeDtypeStruct((KV_HEADS, m, 2 * HEAD_DIM), BF16),
        jax.ShapeDtypeStruct((KV_HEADS, m, HEAD_DIM), BF16),
        jax.ShapeDtypeStruct((KV_HEADS, m, 2 * HEAD_DIM), BF16),
    ) + (kvt,) * n_t
    out_specs = (row(ATTN_WIDTH), row(KV_COLS), row(KV_COLS), row(KV_COLS), row(GATE_PAD), row(CONV_CH),
                 grp(2 * HEAD_DIM), grp(2 * HEAD_DIM), grp(HEAD_DIM), grp(2 * HEAD_DIM)) + (tr,) * n_t
    return pl.pallas_call(
        functools.partial(_inproj_kernel, tm=tm, seq_len=seq_len),
        grid=(m // tm,),
        in_specs=[row(D_MODEL), _const_spec((1, D_MODEL)), _const_spec((D_MODEL, IN_COLS))],
        out_specs=out_specs, out_shape=out_shape,
        compiler_params=_params("parallel"), name="inproj",
    )(x, g, w)


def _conv_ln_silu(y, b, lg, lb):
    y = y + b
    mu = jnp.mean(y, axis=-1, keepdims=True)
    var = jnp.mean(jnp.square(y - mu), axis=-1, keepdims=True)
    return jax.nn.silu((y - mu) * lax.rsqrt(var + EPS) * lg + lb)


def _conv_kernel(prev_ref, a_ref, w_ref, b_ref, lg_ref, lb_ref, o_ref, ext_ref, sh_ref, *, tc):
    first = pl.program_id(1) == 0
    ext_ref[0:CONV_HALO, :] = jnp.where(first, 0.0, prev_ref[0])
    ext_ref[CONV_HALO:CONV_HALO + tc, :] = a_ref[0]
    lead = CONV_HALO - (CONV_WIDTH - 1)
    span = sh_ref.shape[1]
    for r in range(1, SUBLANE):
        sh_ref[r - 1] = ext_ref[r:r + span, :]

    for r0 in range(0, tc, CONV_ROWS):
        acc = jnp.zeros((CONV_ROWS, CONV_CH), F32)
        for k in range(CONV_WIDTH):
            r = (lead + k) % SUBLANE
            i0 = r0 + lead + k - r
            rows = ext_ref[i0:i0 + CONV_ROWS, :] if r == 0 else sh_ref[r - 1, i0:i0 + CONV_ROWS, :]
            acc = acc + w_ref[k:k + 1, :] * rows
        o_ref[0, r0:r0 + CONV_ROWS, :] = _conv_ln_silu(acc, b_ref[...], lg_ref[...], lb_ref[...]).astype(BF16)


def _conv_prompt(a, w, b, lg, lb, *, tc):
    bsz, t, _ = a.shape
    hb = tc // CONV_HALO
    return pl.pallas_call(
        functools.partial(_conv_kernel, tc=tc),
        grid=(bsz, t // tc),
        in_specs=[pl.BlockSpec((1, CONV_HALO, CONV_CH), lambda bi, i: (bi, jnp.maximum(i * hb - 1, 0), 0)),
                  pl.BlockSpec((1, tc, CONV_CH), lambda bi, i: (bi, i, 0)),
                  _const_spec((CONV_WIDTH, CONV_CH)), _const_spec((1, CONV_CH)),
                  _const_spec((1, CONV_CH)), _const_spec((1, CONV_CH))],
        out_specs=pl.BlockSpec((1, tc, CONV_CH), lambda bi, i: (bi, i, 0)),
        out_shape=jax.ShapeDtypeStruct((bsz, t, CONV_CH), BF16),
        scratch_shapes=[pltpu.VMEM((CONV_HALO + tc, CONV_CH), F32),
                        pltpu.VMEM((SUBLANE - 1, CONV_HALO + tc - SUBLANE, CONV_CH), F32)],
        compiler_params=_params("parallel", "parallel"), name="conv_prompt",
    )(a, a, w, b, lg, lb)


def _compress_rows(h0, h1_next, pos_term, w2):
    hid = pos_term + h0 + h1_next
    return _dot(jax.nn.gelu(hid).astype(BF16), w2)


def _pos_term(posx_ref, wc_ref):
    hp = _dot(posx_ref[...], wc_ref[...])
    return hp[0:1, :KV_COLS] + hp[1:2, KV_COLS:]


def _compress_kernel(x_ref, wc_ref, posx_ref, w2_ref, kc_ref, vc_ref, pos_ref):
    hh = _dot(x_ref[0].astype(BF16), wc_ref[...])
    h1 = hh[:, KV_COLS:]
    h1_next = jnp.concatenate([h1[1:], jnp.zeros((1, KV_COLS), F32)], axis=0)
    pos_term = _pos_term(posx_ref, wc_ref)
    pos_ref[...] = jnp.broadcast_to(pos_term, pos_ref.shape)
    cmp = _compress_rows(hh[:, :KV_COLS], h1_next, pos_term, w2_ref[...])
    for g in range(KV_HEADS):
        kc_ref[0, g] = cmp[:, g * HEAD_DIM:(g + 1) * HEAD_DIM].astype(BF16)
        vc_ref[0, g] = cmp[:, (KV_HEADS + g) * HEAD_DIM:(KV_HEADS + g + 1) * HEAD_DIM].astype(BF16)


def _compress_prompt(kvc_chunks, wc, posx, w2):
    bsz, n_ch, _ = kvc_chunks.shape
    tok = jax.ShapeDtypeStruct((bsz, KV_HEADS, n_ch, HEAD_DIM), BF16)
    tok_spec = pl.BlockSpec((1, KV_HEADS, n_ch, HEAD_DIM), lambda bi: (bi, 0, 0, 0))
    return pl.pallas_call(
        _compress_kernel,
        grid=(bsz,),
        in_specs=[pl.BlockSpec((1, n_ch, CHUNK_COLS), lambda bi: (bi, 0, 0)),
                  _const_spec(wc.shape), _const_spec(posx.shape), _const_spec(w2.shape)],
        out_specs=(tok_spec, tok_spec, pl.BlockSpec((8, KV_COLS), lambda bi: (0, 0))),
        out_shape=(tok, tok, jax.ShapeDtypeStruct((8, KV_COLS), F32)),
        compiler_params=_params("arbitrary"), name="compress_prompt",
    )(kvc_chunks, wc, posx, w2)


def _block_scores(imp, ssel, t_col, ns):
    s = _dot_exact_lhs(imp, ssel)
    j = lax.broadcasted_iota(jnp.int32, s.shape, 1)
    cur = t_col // SEL_BLOCK
    valid = j * SEL_BLOCK <= t_col
    forced = valid & ((j == 0) | (j == cur) | (j == cur - 1))
    s = jnp.where(forced, BIG, jnp.where(valid, s, -BIG))
    return jnp.where(j < ns, s, -3.0 * BIG)


def _nsa_prompt_kernel(q_ref, gate_ref, ksa_ref, vs_ref, kw_ref, vw_ref, kc_ref, vc_ref, ssel_ref, o_ref,
                       qa_ref, s_ref, p_ref, part_ref, m_ref, acc_ref, *, tq, ns):
    i = pl.program_id(1)
    q0 = i * tq
    t_col = q0 + lax.broadcasted_iota(jnp.int32, (tq, 1), 0)
    t4 = jnp.concatenate([t_col] * GROUP, axis=0)
    gates = gate_ref[...]
    n_blk = kc_ref.shape[2]
    end = lax.broadcasted_iota(jnp.int32, (1, n_blk), 1) * CMP_STRIDE + (CMP_LEN - 1)
    cmask = end <= t4
    sub = lax.broadcasted_iota(jnp.int32, (SUBLANE, tq), 0)
    o_c = []
    for g in range(KV_HEADS):
        q4 = jnp.concatenate(
            [q_ref[:, (GROUP * g + h) * HEAD_DIM:(GROUP * g + h + 1) * HEAD_DIM] for h in range(GROUP)], axis=0)

        e_c, r_c = _softmax_parts(_dot_nt(q4, kc_ref[0, g]), cmask)
        o_c.append(_dot(e_c.astype(BF16), vc_ref[0, g]) * r_c)
        p_c = e_c * r_c
        imp = p_c[0:tq] + p_c[tq:2 * tq] + p_c[2 * tq:3 * tq] + p_c[3 * tq:4 * tq]

        s_t = _block_scores(imp, ssel_ref[...], t_col, ns).T
        blocks = [s_t[v * SUBLANE:(v + 1) * SUBLANE] for v in range(ns // SUBLANE)]
        ranks = [jnp.zeros((SUBLANE, tq), F32) for _ in blocks]
        for i2 in range(ns):
            row = s_t[i2:i2 + 1, :]
            for v, blk in enumerate(blocks):
                if v > i2 // SUBLANE:
                    beats = row >= blk
                elif v < i2 // SUBLANE:
                    beats = row > blk
                else:
                    beats = (row > blk) | ((row == blk) & (sub > i2 % SUBLANE))
                ranks[v] = ranks[v] + jnp.where(beats, 1.0, 0.0)
        sel_t = jnp.concatenate([jnp.where(r < N_SELECT, 0.0, NEG) for r in ranks]
                                + [jnp.full((LANE - ns, tq), NEG, F32)], axis=0)
        sel = sel_t.T
        selbias = jnp.concatenate([sel[:, :HEAD_DIM].astype(BF16)] * GROUP, axis=0)
        qa_ref[g] = jnp.concatenate([q4, selbias], axis=1)

    rows = GROUP * tq

    def scores(slot, g, k_ref, k0, width, k_cols, key_ok):
        s_ref[slot, :, 0:width] = _dot_nt(qa_ref[g, :, 0:k_cols], k_ref[g, pl.ds(k0, width), :])

        for r0 in range(0, rows, ATT_ROWS):
            sb = s_ref[slot, r0:r0 + ATT_ROWS, 0:width]
            if key_ok is not None:
                t = q0 + r0 % tq + lax.broadcasted_iota(jnp.int32, (ATT_ROWS, 1), 0)
                key = k0 + lax.broadcasted_iota(jnp.int32, (1, width), 1)
                sb = jnp.where(key_ok(t, key), sb, NEG)
                s_ref[slot, r0:r0 + ATT_ROWS, 0:width] = sb
            mx = sb[:, 0:LANE]
            for j in range(1, width // LANE):
                mx = jnp.maximum(mx, sb[:, j * LANE:(j + 1) * LANE])
            part_ref[slot, r0:r0 + ATT_ROWS, :] = mx
        return jnp.max(part_ref[slot], axis=-1, keepdims=True)

    def weights(slot, width):
        for r0 in range(0, rows, ATT_ROWS):
            m = m_ref[slot, r0:r0 + ATT_ROWS, :]
            sb = s_ref[slot, r0:r0 + ATT_ROWS, 0:width]
            p_ref[slot, r0:r0 + ATT_ROWS, 0:width] = jnp.exp(sb - jnp.concatenate([m] * (width // LANE), axis=1)).astype(BF16)

    for g in range(KV_HEADS):
        m_ref[g] = jnp.full((rows, LANE), NEG, F32)
        acc_ref[g] = jnp.zeros((rows, LANE), F32)

    def chunk(k0, key_ok):
        blk_max = [scores(g, g, ksa_ref, k0, SEL_CHUNK, 2 * HEAD_DIM, key_ok) for g in range(KV_HEADS)]
        for g in range(KV_HEADS):
            m_old = m_ref[g]
            m_new = jnp.maximum(m_old, blk_max[g])
            m_ref[g] = m_new
            weights(g, SEL_CHUNK)
            acc_ref[g] = (jnp.exp(m_old - m_new) * acc_ref[g]
                          + _dot(p_ref[g, :, 0:SEL_CHUNK], vs_ref[g, pl.ds(k0, SEL_CHUNK), :]))

    n_full = q0 // SEL_CHUNK

    def full_chunk(c, carry):
        chunk(pl.multiple_of(c * SEL_CHUNK, SEL_CHUNK), None)
        return carry

    lax.fori_loop(0, n_full, full_chunk, 0)
    tail0 = pl.multiple_of(jnp.maximum(q0 + tq - SEL_CHUNK, 0), Q_TILE)
    chunk(tail0, lambda t, key: (key <= t) & (key >= n_full * SEL_CHUNK))

    w0 = pl.multiple_of(jnp.maximum(q0 - WINDOW, 0), Q_TILE)
    in_window = lambda t, key: (t - key >= 0) & (t - key < WINDOW)
    o_w = []
    for g in range(KV_HEADS):
        slot = KV_HEADS + g
        m_ref[slot] = jnp.broadcast_to(scores(slot, g, kw_ref, w0, WIN_KEYS, HEAD_DIM, in_window), (rows, LANE))
        weights(slot, WIN_KEYS)
        acc_w = _dot(p_ref[slot, :, 0:WIN_KEYS], vw_ref[g, pl.ds(w0, WIN_KEYS), :])
        o_w.append(acc_w[:, :HEAD_DIM] * (1.0 / acc_w[:, HEAD_DIM:HEAD_DIM + 1]))

    for g in range(KV_HEADS):
        acc_s = acc_ref[g]
        o_s = acc_s[:, :HEAD_DIM] * (1.0 / acc_s[:, HEAD_DIM:HEAD_DIM + 1])
        for h in range(GROUP):
            hh = GROUP * g + h
            r = slice(h * tq, (h + 1) * tq)
            c = N_BRANCH * hh
            o = gates[:, c:c + 1] * o_c[g][r] + gates[:, c + 1:c + 2] * o_s[r] + gates[:, c + 2:c + 3] * o_w[g][r]
            o_ref[:, hh * HEAD_DIM:(hh + 1) * HEAD_DIM] = o.astype(BF16)


def _nsa_prompt(q, gates, ksa, vs, kw, vw, kc, vc, ssel, *, bsz, seq_len):
    tq = Q_TILE
    nt = seq_len // tq
    ns = seq_len // SEL_BLOCK
    n_blk = kc.shape[2]
    rows = GROUP * tq
    row = lambda c: pl.BlockSpec((tq, c), lambda bi, i: (bi * nt + i, 0))
    seq = lambda c: pl.BlockSpec((KV_HEADS, seq_len, c), lambda bi, i: (0, bi, 0))
    tok = pl.BlockSpec((1, KV_HEADS, n_blk, HEAD_DIM), lambda bi, i: (bi, 0, 0, 0))
    return pl.pallas_call(
        functools.partial(_nsa_prompt_kernel, tq=tq, ns=ns),
        grid=(bsz, nt),
        in_specs=[row(ATTN_WIDTH), row(GATE_PAD), seq(2 * HEAD_DIM), seq(2 * HEAD_DIM), seq(HEAD_DIM), seq(2 * HEAD_DIM),
                  tok, tok, _const_spec(ssel.shape)],
        out_specs=row(ATTN_WIDTH),
        out_shape=jax.ShapeDtypeStruct((bsz * seq_len, ATTN_WIDTH), BF16),
        scratch_shapes=[pltpu.VMEM((KV_HEADS, rows, 2 * HEAD_DIM), BF16), pltpu.VMEM((2 * KV_HEADS, rows, WIN_KEYS), F32),
                        pltpu.VMEM((2 * KV_HEADS, rows, WIN_KEYS), BF16), pltpu.VMEM((2 * KV_HEADS, rows, LANE), F32),
                        pltpu.VMEM((2 * KV_HEADS, rows, LANE), F32), pltpu.VMEM((KV_HEADS, rows, LANE), F32)],
        compiler_params=_params("parallel", "parallel"), name="nsa_prompt",
    )(q, gates, ksa, vs, kw, vw, kc, vc, ssel)


def _swiglu_residual(x1, gf, wg_ref, wu_ref, wd_ref):
    h = _rms(x1, gf).astype(BF16)
    act = jax.nn.silu(_dot(h, wg_ref[...])) * _dot(h, wu_ref[...])
    return x1 + _dot(act.astype(BF16), wd_ref[...])


def _ffn0_kernel(x_ref, a_ref, c_ref, woa_ref, woc_ref, gf_ref, wg_ref, wu_ref, wd_ref, o_ref):
    x1 = x_ref[...] + (_dot(a_ref[...], woa_ref[...]) + _dot(c_ref[...], woc_ref[...]))
    o_ref[...] = _swiglu_residual(x1, gf_ref[...], wg_ref, wu_ref, wd_ref)


def _ffn0(x, a, c, woa, woc, gf, wg, wu, wd, *, tm):
    m = x.shape[0]
    row = lambda cols: pl.BlockSpec((tm, cols), lambda i: (i, 0))
    return pl.pallas_call(
        _ffn0_kernel,
        grid=(m // tm,),
        in_specs=[row(D_MODEL), row(ATTN_WIDTH), row(CONV_CH), _const_spec(woa.shape), _const_spec(woc.shape),
                  _const_spec(gf.shape), _const_spec(wg.shape), _const_spec(wu.shape), _const_spec(wd.shape)],
        out_specs=row(D_MODEL), out_shape=jax.ShapeDtypeStruct((m, D_MODEL), F32),
        compiler_params=_params("parallel"), name="outproj_ffn",
    )(x, a, c, woa, woc, gf, wg, wu, wd)


def _pool_mix(x, h, win_sums, cnts, pw_ref, ps):
    ys = []
    for g in range(len(POOL_WINDOWS)):
        z = win_sums[g] / cnts[g] - h[:, g * POOL_GROUP:(g + 1) * POOL_GROUP]
        ys.append(_dot(z.astype(BF16), pw_ref[g]))
    return x + jnp.concatenate(ys, axis=1) * ps


def _layer1_prompt_kernel(xprev_ref, x_ref, gm_ref, pw_ref, ps_ref, gf_ref, wg_ref, wu_ref, wd_ref, gfin_ref,
                          y_ref, hst_ref, *, tm):
    i = pl.program_id(1)
    x = x_ref[0]
    h = _rms(x, gm_ref[...])
    hprev = jnp.where(i == 0, 0.0, _rms(xprev_ref[0], gm_ref[...]))
    hst_ref[0] = h[tm - POOL_HALO:, :]
    pos = i * tm + lax.broadcasted_iota(jnp.int32, (tm, 1), 0)
    sums, cnts = [], []
    for g, w in enumerate(POOL_WINDOWS):
        e = jnp.concatenate([hprev[:, g * POOL_GROUP:(g + 1) * POOL_GROUP],
                             h[:, g * POOL_GROUP:(g + 1) * POOL_GROUP]], axis=0)
        span = 1
        while span < w:
            e = e[span:] + e[:-span]
            span *= 2
        first = POOL_HALO - (w - 1)
        sums.append(e[first:first + tm])
        cnts.append(jnp.minimum(w, pos + 1).astype(F32))
    x1 = _pool_mix(x, h, sums, cnts, pw_ref, ps_ref[...])
    x2 = _swiglu_residual(x1, gf_ref[...], wg_ref, wu_ref, wd_ref)
    y_ref[0] = _rms(x2, gfin_ref[...])


def _layer1_prompt(x, gm, pw, ps, gf, wg, wu, wd, gfin, *, tm):
    bsz, t, _ = x.shape
    hb = tm // POOL_HALO
    return pl.pallas_call(
        functools.partial(_layer1_prompt_kernel, tm=tm),
        grid=(bsz, t // tm),
        in_specs=[pl.BlockSpec((1, POOL_HALO, D_MODEL), lambda bi, i: (bi, jnp.maximum(i * hb - 1, 0), 0)),
                  pl.BlockSpec((1, tm, D_MODEL), lambda bi, i: (bi, i, 0)),
                  _const_spec(gm.shape), _const_spec(pw.shape), _const_spec(ps.shape), _const_spec(gf.shape),
                  _const_spec(wg.shape), _const_spec(wu.shape), _const_spec(wd.shape), _const_spec(gfin.shape)],
        out_specs=(pl.BlockSpec((1, tm, D_MODEL), lambda bi, i: (bi, i, 0)),
                   pl.BlockSpec((1, POOL_HALO, D_MODEL), lambda bi, i: (bi, 0, 0))),
        out_shape=(jax.ShapeDtypeStruct((bsz, t, D_MODEL), F32),
                   jax.ShapeDtypeStruct((bsz, POOL_HALO, D_MODEL), F32)),
        compiler_params=_params("parallel", "arbitrary"), name="layer1_prompt",
    )(x, x, gm, pw, ps, gf, wg, wu, wd, gfin)


def _layer1_sample_kernel(x_ref, hist_ref, gm_ref, pw_ref, ps_ref, gf_ref, wg_ref, wu_ref, wd_ref, gfin_ref,
                          y_ref, h_ref, *, first_pos):
    x = x_ref[...]
    h = _rms(x, gm_ref[...])
    h_ref[...] = h
    sums, cnts = [], []
    for g, w in enumerate(POOL_WINDOWS):
        c = slice(g * POOL_GROUP, (g + 1) * POOL_GROUP)
        s = h[:, c]
        for k in range(1, w):
            s = s + hist_ref[POOL_BUF - k][:, c]
        sums.append(s)
        cnts.append(float(min(w, first_pos + 1)))
    x1 = _pool_mix(x, h, sums, cnts, pw_ref, ps_ref[...])
    x2 = _swiglu_residual(x1, gf_ref[...], wg_ref, wu_ref, wd_ref)
    y_ref[...] = _rms(x2, gfin_ref[...])


def _layer1_sample(x, hist, gm, pw, ps, gf, wg, wu, wd, gfin, *, first_pos):
    m = x.shape[0]
    args = (x, hist, gm, pw, ps, gf, wg, wu, wd, gfin)
    out = jax.ShapeDtypeStruct((m, D_MODEL), F32)
    return pl.pallas_call(
        functools.partial(_layer1_sample_kernel, first_pos=first_pos),
        grid=(1,),
        in_specs=[_const_spec(a.shape) for a in args],
        out_specs=(_const_spec((m, D_MODEL)), _const_spec((m, D_MODEL))), out_shape=(out, out),
        compiler_params=_params("arbitrary"), name="layer1_sample",
    )(*args)


def _page_copy(cache_ref, xt_ref, sem, phys, p):
    return pltpu.make_async_copy(cache_ref.at[phys], xt_ref.at[p], sem)


def _cmp_sample_kernel(pt_ref, q_ref, cache_ref, wkv_ref, pos_ref, w2_ref, ssel_ref, oc_ref, idx_ref,
                       xt_ref, xl_ref, hbuf_ref, sem, *, n_pages, past, ns, nsp):
    b = pl.program_id(0)
    n_ch = n_pages * (PAGE_SIZE // CMP_STRIDE)

    def gather(sample):
        def start(p, carry):
            _page_copy(cache_ref, xt_ref, sem, pt_ref[sample * n_pages + p], p).start()
            return carry

        lax.fori_loop(0, n_pages, start, 0)

    def wait(p, carry):
        _page_copy(cache_ref, xt_ref, sem, 0, p).wait()
        return carry

    per_page = PAGE_SIZE // CMP_STRIDE
    r_out = lax.broadcasted_iota(jnp.int32, (PAGE_SIZE, PAGE_SIZE), 0)
    r_in = lax.broadcasted_iota(jnp.int32, (PAGE_SIZE, PAGE_SIZE), 1)
    perm = (r_in == (r_out % per_page) * CMP_STRIDE + r_out // per_page).astype(BF16)

    batch = 8

    def to_rows(i, carry):
        p0 = pl.multiple_of(i * batch, batch)
        pages = xt_ref[pl.ds(p0, batch)].reshape(batch * KV_COLS, PAGE_SIZE)
        xp = _dot_nt(perm, pages.astype(BF16))
        for k in range(batch):
            c0 = pl.multiple_of((p0 + k) * per_page, per_page)
            for l in range(CMP_STRIDE):
                xl_ref[l, pl.ds(c0, per_page), :] = xp[l * per_page:(l + 1) * per_page, k * KV_COLS:(k + 1) * KV_COLS]
        return carry

    @pl.when(b == 0)
    def _():
        gather(0)

    lax.fori_loop(0, n_pages, wait, 0)
    lax.fori_loop(0, n_pages // batch, to_rows, 0)

    @pl.when(b + 1 < pl.num_programs(0))
    def _():
        gather(b + 1)

    rows = min(256, n_ch)
    half = KV_COLS // 2
    for r0 in range(0, n_ch, rows):
        for kv in range(2):
            chunk_rows = jnp.concatenate(
                [xl_ref[l, r0:r0 + rows, kv * half:(kv + 1) * half].astype(BF16) for l in range(CMP_STRIDE)], axis=1)
            hbuf_ref[r0:r0 + rows, kv * KV_COLS:(kv + 1) * KV_COLS] = _dot(chunk_rows, wkv_ref[kv])
    hbuf_ref[n_ch:n_ch + 8, :] = jnp.zeros((8, 2 * KV_COLS), F32)
    first = lambda r: jnp.concatenate([hbuf_ref[r, 0:half], hbuf_ref[r, KV_COLS:KV_COLS + half]], axis=1)
    second = lambda r: jnp.concatenate([hbuf_ref[r, half:KV_COLS], hbuf_ref[r, KV_COLS + half:]], axis=1)
    cmp = _compress_rows(first(pl.ds(0, n_ch)), second(pl.ds(1, n_ch)), pos_ref[0:1, :], w2_ref[...])

    q8 = q_ref[0]
    head = lax.broadcasted_iota(jnp.int32, (N_HEADS, 1), 0)
    end = lax.broadcasted_iota(jnp.int32, (1, n_ch), 1) * CMP_STRIDE + (CMP_LEN - 1)
    cmask = jnp.broadcast_to(end <= past, (N_HEADS, n_ch))
    o_c = jnp.zeros((N_HEADS, HEAD_DIM), F32)
    jl = lax.broadcasted_iota(jnp.int32, (1, nsp), 1)
    ii = lax.broadcasted_iota(jnp.int32, (nsp, nsp), 0)
    jj = lax.broadcasted_iota(jnp.int32, (nsp, nsp), 1)
    cur = past // SEL_BLOCK
    t_col = jnp.full((N_HEADS, 1), past, jnp.int32)
    for g in range(KV_HEADS):
        in_group = head // GROUP == g
        kc = cmp[:, g * HEAD_DIM:(g + 1) * HEAD_DIM].astype(BF16)
        vc = cmp[:, (KV_HEADS + g) * HEAD_DIM:(KV_HEADS + g + 1) * HEAD_DIM].astype(BF16)
        sc = _dot_nt(q8, kc)
        p_c = _softmax_rows(jnp.where(cmask, sc, NEG), cmask)
        o_c = jnp.where(in_group, _dot(p_c.astype(BF16), vc), o_c)
        imp = jnp.sum(jnp.where(in_group, p_c, 0.0), axis=0, keepdims=True)
        s_b = _block_scores(jnp.broadcast_to(imp, (N_HEADS, n_ch)), ssel_ref[...], t_col, ns)
        s_b = jnp.broadcast_to(s_b[0:1], (nsp, nsp))
        s_t = s_b.T
        beats = (s_t > s_b) | ((s_t == s_b) & (ii < jj))
        rank = jnp.sum(jnp.where(beats, 1.0, 0.0), axis=0, keepdims=True)
        sel = (rank < N_SELECT) & (jl != cur)
        sel_f = jnp.where(sel, 1.0, 0.0)
        before = _dot(jnp.broadcast_to(sel_f, (N_HEADS, nsp)).astype(BF16),
                      jnp.where(ii < jj, 1.0, 0.0).astype(BF16))[0:1]
        slot = lax.broadcasted_iota(jnp.int32, (N_SELECT, nsp), 0)
        pick = jnp.broadcast_to(sel, (N_SELECT, nsp)) & (jnp.broadcast_to(before, (N_SELECT, nsp)) == slot.astype(F32))
        blk = jnp.sum(jnp.where(pick, lax.broadcasted_iota(jnp.int32, (N_SELECT, nsp), 1), 0), axis=1, keepdims=True)
        idx_ref[0, g] = blk
    oc_ref[0] = o_c


def _cmp_sample(page_table, q8, cache_pages, wc, pos_term, w2, ssel, *, past, ns):
    bsz, n_pages = page_table.shape
    n_ch = n_pages * (PAGE_SIZE // CMP_STRIDE)
    nsp = ssel.shape[1]
    grid_spec = pltpu.PrefetchScalarGridSpec(
        num_scalar_prefetch=1, grid=(bsz,),
        in_specs=[pl.BlockSpec((1, N_HEADS, HEAD_DIM), lambda b, pt: (b, 0, 0)),
                  pl.BlockSpec(memory_space=pl.ANY),
                  _const_spec(wc.shape), _const_spec(pos_term.shape), _const_spec(w2.shape), _const_spec(ssel.shape)],
        out_specs=(pl.BlockSpec((1, N_HEADS, HEAD_DIM), lambda b, pt: (b, 0, 0)),
                   pl.BlockSpec((1, KV_HEADS, N_SELECT, 1), lambda b, pt: (b, 0, 0, 0))),
        scratch_shapes=[pltpu.VMEM((n_pages, KV_COLS, PAGE_SIZE), F32), pltpu.VMEM((CMP_STRIDE, n_ch, KV_COLS), F32),
                        pltpu.VMEM((n_ch + 8, 2 * KV_COLS), F32), pltpu.SemaphoreType.DMA(())],
    )
    return pl.pallas_call(
        functools.partial(_cmp_sample_kernel, n_pages=n_pages, past=past, ns=ns, nsp=nsp),
        grid_spec=grid_spec,
        out_shape=(jax.ShapeDtypeStruct((bsz, N_HEADS, HEAD_DIM), F32),
                   jax.ShapeDtypeStruct((bsz, KV_HEADS, N_SELECT, 1), jnp.int32)),
        compiler_params=_params("arbitrary"), name="cmp_sample",
    )(page_table.reshape(-1), q8, cache_pages, wc, pos_term, w2, ssel)


def _slab_copy(cache_ref, dst_ref, sem, src, g, k):
    return pltpu.make_async_copy(cache_ref.at[src], dst_ref.at[g, :, pl.ds(k * PAGE_SIZE, PAGE_SIZE)], sem)


def _mix_sample_kernel(idx_ref, pt_ref, q_ref, gate_ref, oc_ref, kvs_ref, kvw_ref, win_ref, cst_ref, a_ref,
                       cw_ref, cb_ref, lg_ref, lb_ref, cache_ref, att_ref, conv_ref, kt_ref, vt_ref, sem,
                       *, n_pages, n_gather, past, w_buf):
    b = pl.program_id(0)
    per_page = PAGE_SIZE // SEL_BLOCK
    halves = []
    for g in range(KV_HEADS):
        for k in range(n_gather):
            blk = idx_ref[(b * KV_HEADS + g) * N_SELECT + k]
            slab = pt_ref[b * n_pages + blk // per_page] * (2 * KV_HEADS) + g
            _slab_copy(cache_ref, kt_ref, sem, slab, g, k).start()
            _slab_copy(cache_ref, vt_ref, sem, slab + KV_HEADS, g, k).start()
            halves.append(blk % per_page)

    y = (jnp.sum(cw_ref[0:CONV_WIDTH - 1, :] * cst_ref[0], axis=0, keepdims=True)
         + cw_ref[CONV_WIDTH - 1:CONV_WIDTH, :] * a_ref[0])
    conv_ref[0] = _conv_ln_silu(y, cb_ref[...], lg_ref[...], lb_ref[...])

    def wait(s, carry):
        _slab_copy(cache_ref, kt_ref, sem, 0, 0, 0).wait()
        return carry

    lax.fori_loop(0, 2 * KV_HEADS * n_gather, wait, 0)

    q8 = q_ref[0]
    q8f = q8.astype(F32)
    head = lax.broadcasted_iota(jnp.int32, (N_HEADS, 1), 0)
    gates = gate_ref[0]
    kvs_new, kvw_new = kvs_ref[0], kvw_ref[0]
    jw = lax.broadcasted_iota(jnp.int32, (1, w_buf), 1)
    wdiff = w_buf - jw
    wmask = jnp.broadcast_to((wdiff < WINDOW) & (past - wdiff >= 0), (N_HEADS, w_buf))
    half_of_lane = lax.broadcasted_iota(jnp.int32, (1, PAGE_SIZE), 1) // SEL_BLOCK
    out = jnp.zeros((N_HEADS, HEAD_DIM), F32)

    def with_new_row(kt, vt, mask, k_new, v_new):
        s = jnp.where(mask, _dot(q8, kt.astype(BF16)), NEG)
        s_new = jnp.sum(q8f * k_new.astype(BF16).astype(F32), axis=-1, keepdims=True)
        m = jnp.maximum(jnp.max(s, axis=-1, keepdims=True), s_new)
        e = jnp.where(mask, jnp.exp(s - m), 0.0)
        e_new = jnp.exp(s_new - m)
        num = _dot_nt(e.astype(BF16), vt.astype(BF16)) + e_new * v_new
        return num / (jnp.sum(e, axis=-1, keepdims=True) + e_new)

    for g in range(KV_HEADS):
        kc, vc = slice(g * HEAD_DIM, (g + 1) * HEAD_DIM), slice((KV_HEADS + g) * HEAD_DIM, (KV_HEADS + g + 1) * HEAD_DIM)
        smask = jnp.concatenate([half_of_lane == halves[g * n_gather + k] for k in range(n_gather)], axis=1)
        smask = jnp.broadcast_to(smask, (N_HEADS, n_gather * PAGE_SIZE))
        o_s = with_new_row(kt_ref[g], vt_ref[g], smask, kvs_new[:, kc], kvs_new[:, vc])
        o_w = with_new_row(win_ref[0, g], win_ref[0, KV_HEADS + g], wmask, kvw_new[:, kc], kvw_new[:, vc])
        mixed = gates[:, 0:1] * oc_ref[0] + gates[:, 1:2] * o_s + gates[:, 2:3] * o_w
        out = jnp.where(head // GROUP == g, mixed, out)
    att_ref[0] = out


def _mix_sample(idx, page_table, q8, gates8, o_c, kvs, kvw, win, conv_state, a, cw, cb, lg, lb, cache_slabs,
                *, past, n_gather):
    bsz, n_pages = page_table.shape
    w_buf = win.shape[-1]
    one = lambda *shape: pl.BlockSpec((1,) + shape, lambda b, *_: (b,) + (0,) * len(shape))
    const = lambda shape: pl.BlockSpec(shape, lambda b, *_: (0,) * len(shape))
    gathered = pltpu.VMEM((KV_HEADS, HEAD_DIM, n_gather * PAGE_SIZE), F32)
    grid_spec = pltpu.PrefetchScalarGridSpec(
        num_scalar_prefetch=2, grid=(bsz,),
        in_specs=[one(N_HEADS, HEAD_DIM), one(N_HEADS, N_BRANCH), one(N_HEADS, HEAD_DIM), one(1, KV_COLS),
                  one(1, KV_COLS), one(2 * KV_HEADS, HEAD_DIM, w_buf), one(CONV_WIDTH - 1, CONV_CH), one(1, CONV_CH),
                  const(cw.shape), const(cb.shape), const(lg.shape), const(lb.shape),
                  pl.BlockSpec(memory_space=pl.ANY)],
        out_specs=(one(N_HEADS, HEAD_DIM), one(1, CONV_CH)),
        scratch_shapes=[gathered, gathered, pltpu.SemaphoreType.DMA(())],
    )
    return pl.pallas_call(
        functools.partial(_mix_sample_kernel, n_pages=n_pages, n_gather=n_gather, past=past, w_buf=w_buf),
        grid_spec=grid_spec,
        out_shape=(jax.ShapeDtypeStruct((bsz, N_HEADS, HEAD_DIM), F32),
                   jax.ShapeDtypeStruct((bsz, 1, CONV_CH), F32)),
        compiler_params=_params("arbitrary"), name="mix_sample",
    )(idx.reshape(-1), page_table.reshape(-1), q8, gates8, o_c, kvs, kvw, win, conv_state, a, cw, cb, lg, lb,
      cache_slabs)


def _inproj_weight(w_in):
    off = ATTN_WIDTH + N_BRANCH * KV_COLS
    n_gate = N_HEADS * N_BRANCH
    gate = jnp.pad(w_in[:, off:off + n_gate], ((0, 0), (0, GATE_PAD - n_gate)))
    return jnp.concatenate([w_in[:, :off], gate, w_in[:, off + n_gate:]], axis=1).astype(BF16)


def _compress_weights(pos_k, w1_k, w2_k, pos_v, w1_v, w2_v):
    n_slot = 2 * KV_HEADS
    eye = jnp.eye(n_slot, dtype=F32)

    def place(per_slot):
        w = jnp.stack(per_slot, axis=0)
        full = jnp.einsum('jclde,jk->ljdcke', w, eye)
        return full.reshape(CHUNK_COLS, N_SUB * KV_COLS)

    w1k = w1_k.reshape(N_SUB, CMP_STRIDE, HEAD_DIM, HEAD_DIM)
    w1v = w1_v.reshape(N_SUB, CMP_STRIDE, HEAD_DIM, HEAD_DIM)
    wc = place([w1k, w1k, w1v, w1v]).astype(BF16)
    pk = pos_k.reshape(N_SUB, CMP_STRIDE, 1, HEAD_DIM)
    pv = pos_v.reshape(N_SUB, CMP_STRIDE, 1, HEAD_DIM)
    posx = jnp.concatenate([pk, pk, pv, pv], axis=2).reshape(N_SUB, CHUNK_COLS)
    posx = jnp.pad(posx, ((0, 8 - N_SUB), (0, 0))).astype(BF16)
    w2 = jnp.einsum('jef,jk->jekf', jnp.stack([w2_k, w2_k, w2_v, w2_v]), eye).reshape(KV_COLS, KV_COLS).astype(BF16)
    return wc, posx, w2


def _split_kv(wc):
    half = KV_COLS // 2
    w = wc.reshape(CMP_STRIDE, 2, half, N_SUB, 2, half)
    return jnp.stack([w[:, kv, :, :, kv, :].reshape(CMP_STRIDE * half, N_SUB * half) for kv in range(2)])


def _selection_matrix(n_rows, n_cols):
    n = jnp.arange(n_rows)[:, None]
    j = jnp.arange(n_cols)[None, :]
    cnt = jnp.zeros((n_rows, n_cols), F32)
    for m in range(SEL_RATIO):
        for sub in range(N_SUB):
            cnt = cnt + (SEL_RATIO * j + m - sub == n).astype(F32)
    return cnt.astype(BF16)


def kernel(x_prompt, x_sample, cache_cmp_kv, cache_slc_kv, state_win_kv, state_conv, state_pool, page_table,
           norm_mix, norm_ffn, norm_final, w_in_a, w_out_a, cmp_pos_k, cmp_w1_k, cmp_w2_k, cmp_pos_v, cmp_w1_v,
           cmp_w2_v, conv_w, conv_b, conv_ln_g, conv_ln_b, pool_w, pool_scale, w_ffn_gate, w_ffn_up, w_ffn_down):
    bp, tp, _ = x_prompt.shape
    bs, s_new, _ = x_sample.shape
    n_pages = page_table.shape[1]
    past = n_pages * PAGE_SIZE
    w_buf = state_win_kv.shape[2]
    n_pool = cache_cmp_kv.shape[1]
    ns_p = tp // SEL_BLOCK
    ns_s = -(-(past + s_new) // SEL_BLOCK)
    assert s_new == 1 and tp % SEL_CHUNK == 0 and tp >= WIN_KEYS and N_SELECT <= ns_p <= HEAD_DIM
    assert ns_s > N_SELECT and norm_mix.shape[0] == 2 and ns_p % SUBLANE == 0

    row = lambda v: v.reshape(1, -1)
    w_in = _inproj_weight(w_in_a[0])
    wc, posx, w2 = _compress_weights(cmp_pos_k[0], cmp_w1_k[0], cmp_w2_k[0], cmp_pos_v[0], cmp_w1_v[0], cmp_w2_v[0])
    woa, woc = w_out_a[0, :ATTN_WIDTH].astype(BF16), w_out_a[0, ATTN_WIDTH:].astype(BF16)
    wg, wu, wd = w_ffn_gate.astype(BF16), w_ffn_up.astype(BF16), w_ffn_down.astype(BF16)
    cw, cb, lg, lb = conv_w[0], row(conv_b[0]), row(conv_ln_g[0]), row(conv_ln_b[0])
    pw, ps = pool_w[0].astype(BF16), row(pool_scale[0])

    m = bp * tp
    xp = x_prompt.reshape(m, D_MODEL)
    q, kvc, kvs, kvw, gates, a, ksa, vs, kw, vw, kvct, kvst, kvwt = _inproj(
        xp, row(norm_mix[0]), w_in, tm=512, seq_len=tp, transposed_kv=True)
    c_out = _conv_prompt(a.reshape(bp, tp, CONV_CH), cw, cb, lg, lb, tc=512)
    kc, vc, pos_term = _compress_prompt(kvc.reshape(bp, tp // CMP_STRIDE, CHUNK_COLS), wc, posx, w2)
    a_out = _nsa_prompt(q, gates, ksa, vs, kw, vw, kc, vc, _selection_matrix(tp // CMP_STRIDE, LANE),
                        bsz=bp, seq_len=tp)
    xp = _ffn0(xp, a_out, c_out.reshape(m, CONV_CH), woa, woc, row(norm_ffn[0]), wg[0], wu[0], wd[0], tm=512)
    y_prompt, pool_tail = _layer1_prompt(xp.reshape(bp, tp, D_MODEL), row(norm_mix[1]), pw, ps, row(norm_ffn[1]),
                                         wg[1], wu[1], wd[1], row(norm_final), tm=512)
    kv6 = lambda z, b: z.reshape(1, b, -1, 2, KV_HEADS, HEAD_DIM)
    kv6_t = lambda zt: jnp.transpose(zt.reshape(bp, 2, KV_HEADS, HEAD_DIM, tp), (0, 4, 1, 2, 3))[None]
    new_cmp_p, new_slc_p = kv6_t(kvct), kv6_t(kvst)
    new_win_p = kv6_t(kvwt)[:, :, -min(WINDOW, tp):]
    new_conv_p = a.reshape(1, bp, tp, CONV_CH)[:, :, -(CONV_WIDTH - 1):]
    new_pool_p = pool_tail[None, :, -POOL_BUF:]

    xs = x_sample.reshape(bs, D_MODEL)
    q, kvc, kvs, kvw, gates, a = _inproj(xs, row(norm_mix[0]), w_in, tm=bs, seq_len=1, transposed_kv=False)[:6]
    q8 = q.reshape(bs, N_HEADS, HEAD_DIM)
    nsp = -(-ns_s // LANE) * LANE
    rows_last = lambda c: jnp.transpose(c, (0, 2, 3, 4, 1))
    o_c, idx = _cmp_sample(page_table, q8, rows_last(cache_cmp_kv[0]).reshape(n_pool, KV_COLS, PAGE_SIZE),
                           _split_kv(wc), pos_term, w2,
                           _selection_matrix(n_pages * (PAGE_SIZE // CMP_STRIDE), nsp), past=past, ns=ns_s)
    gates8 = gates[:, :N_HEADS * N_BRANCH].reshape(bs, N_HEADS, N_BRANCH)
    att, c_s = _mix_sample(idx, page_table, q8, gates8, o_c, kvs.reshape(bs, 1, KV_COLS), kvw.reshape(bs, 1, KV_COLS),
                           rows_last(state_win_kv[0]).reshape(bs, 2 * KV_HEADS, HEAD_DIM, w_buf), state_conv[0],
                           a.reshape(bs, 1, CONV_CH), cw, cb, lg, lb,
                           rows_last(cache_slc_kv[0]).reshape(n_pool * 2 * KV_HEADS, HEAD_DIM, PAGE_SIZE),
                           past=past, n_gather=N_SELECT - 1)
    xs = _ffn0(xs, att.reshape(bs, ATTN_WIDTH).astype(BF16), c_s.reshape(bs, CONV_CH).astype(BF16), woa, woc,
               row(norm_ffn[0]), wg[0], wu[0], wd[0], tm=bs)
    y_sample, h_s = _layer1_sample(xs, jnp.swapaxes(state_pool[0], 0, 1), row(norm_mix[1]), pw, ps, row(norm_ffn[1]),
                                   wg[1], wu[1], wd[1], row(norm_final), first_pos=past)
    new_win_s = jnp.concatenate([state_win_kv[0], kv6(kvw, bs)[0]], axis=1)[None, :, -w_buf:]
    new_conv_s = jnp.concatenate([state_conv[0], a.reshape(bs, 1, CONV_CH)], axis=1)[None, :, -(CONV_WIDTH - 1):]
    new_pool_s = jnp.concatenate([state_pool[0], h_s[:, None, :]], axis=1)[None, :, -POOL_BUF:]

    return (y_prompt, y_sample.reshape(bs, s_new, D_MODEL), new_cmp_p, kv6(kvc, bs), new_slc_p, kv6(kvs, bs),
            new_win_p, new_win_s, new_conv_p, new_conv_s, new_pool_p, new_pool_s)
```

```python
import functools

import jax
import jax.numpy as jnp
from jax import lax
from jax.experimental import pallas as pl
from jax.experimental.pallas import tpu as pltpu

F32 = jnp.float32
BF16 = jnp.bfloat16

D_MODEL = 1024
N_HEADS = 8
HEAD_DIM = 64
KV_HEADS = 2
GROUP = N_HEADS // KV_HEADS
ATTN_WIDTH = N_HEADS * HEAD_DIM
KV_COLS = 2 * KV_HEADS * HEAD_DIM
CMP_LEN = 32
CMP_STRIDE = 16
N_SUB = CMP_LEN // CMP_STRIDE
SEL_BLOCK = 64
SEL_RATIO = SEL_BLOCK // CMP_STRIDE
N_SELECT = 16
WINDOW = 512
PAGE_SIZE = 128
N_BRANCH = 3
SCALE = HEAD_DIM ** -0.5
CONV_CH = D_MODEL // 2
CONV_WIDTH = 31
POOL_WINDOWS = (2, 4, 8, 16)
POOL_GROUP = D_MODEL // len(POOL_WINDOWS)
POOL_BUF = max(POOL_WINDOWS) - 1
EPS = 1e-6
BIG = 1e9
NEG = -1e30
TINY = float(jnp.finfo(jnp.float32).tiny)

LANE = 128
SUBLANE = 8
CHUNK_COLS = CMP_STRIDE * KV_COLS
GATE_PAD = LANE
IN_COLS = ATTN_WIDTH + N_BRANCH * KV_COLS + GATE_PAD + 2 * CONV_CH
VMEM_LIMIT = 56 * 1024 * 1024

Q_TILE = 128
SEL_CHUNK = 512
WIN_KEYS = WINDOW + Q_TILE
ATT_ROWS = 64
CONV_HALO = 32
CONV_ROWS = 32
POOL_HALO = 16


def _dot(a, b):
    return jnp.dot(a, b, preferred_element_type=F32)


def _dot_nt(a, b):
    return lax.dot_general(a, b, (((1,), (1,)), ((), ())), preferred_element_type=F32)


def _dot_exact_lhs(a, b):
    hi = a.astype(BF16)
    r1 = a - hi.astype(F32)
    mid = r1.astype(BF16)
    lo = (r1 - mid.astype(F32)).astype(BF16)
    return _dot(hi, b) + _dot(mid, b) + _dot(lo, b)


def _rms(x, g):
    return x * lax.rsqrt(jnp.mean(x * x, axis=-1, keepdims=True) + EPS) * g


def _softmax_rows(s, mask):
    m = jnp.max(s, axis=-1, keepdims=True)
    e = jnp.where(mask, jnp.exp(s - m), 0.0)
    return e / jnp.maximum(jnp.sum(e, axis=-1, keepdims=True), TINY)


def _softmax_parts(s, mask):
    s = jnp.where(mask, s, NEG)
    e = jnp.where(mask, jnp.exp(s - jnp.max(s, axis=-1, keepdims=True)), 0.0)
    return e, 1.0 / jnp.maximum(jnp.sum(e, axis=-1, keepdims=True), TINY)


def _params(*sem):
    return pltpu.CompilerParams(dimension_semantics=sem, vmem_limit_bytes=VMEM_LIMIT)


def _const_spec(shape):
    nd = len(shape)
    return pl.BlockSpec(shape, lambda *_: (0,) * nd, pipeline_mode=pl.Buffered(1))


def _inproj_kernel(x_ref, g_ref, w_ref, q_ref, kvc_ref, kvs_ref, kvw_ref, gate_ref, a_ref,
                   ksa_ref, vs_ref, kw_ref, vw_ref, *kvt_refs, tm, seq_len):
    h = _rms(x_ref[...], g_ref[...])
    z = _dot(h.astype(BF16), w_ref[...])
    off = ATTN_WIDTH
    for br, kvt_ref in enumerate(kvt_refs):
        kvt_ref[0] = z[:, off + br * KV_COLS:off + (br + 1) * KV_COLS].T
    q_ref[...] = (z[:, :off] * SCALE).astype(BF16)
    kvc_ref[...] = z[:, off:off + KV_COLS]
    kvs = z[:, off + KV_COLS:off + 2 * KV_COLS]
    kvw = z[:, off + 2 * KV_COLS:off + 3 * KV_COLS]
    kvs_ref[...] = kvs
    kvw_ref[...] = kvw
    off += 3 * KV_COLS
    gate_ref[...] = jax.nn.sigmoid(z[:, off:off + GATE_PAD])
    off += GATE_PAD
    a_ref[...] = z[:, off:off + CONV_CH] * jax.nn.sigmoid(z[:, off + CONV_CH:])
    pos = (pl.program_id(0) * tm + lax.broadcasted_iota(jnp.int32, (tm, HEAD_DIM), 0)) % seq_len
    lane = lax.broadcasted_iota(jnp.int32, (tm, HEAD_DIM), 1)
    onehot = (pos // SEL_BLOCK == lane).astype(BF16)
    ones_col = (lane == 0).astype(BF16)
    for g in range(KV_HEADS):
        k0, v0 = g * HEAD_DIM, (KV_HEADS + g) * HEAD_DIM
        ksa_ref[g] = jnp.concatenate([kvs[:, k0:k0 + HEAD_DIM].astype(BF16), onehot], axis=1)
        vs_ref[g] = jnp.concatenate([kvs[:, v0:v0 + HEAD_DIM].astype(BF16), ones_col], axis=1)
        kw_ref[g] = kvw[:, k0:k0 + HEAD_DIM].astype(BF16)
        vw_ref[g] = jnp.concatenate([kvw[:, v0:v0 + HEAD_DIM].astype(BF16), ones_col], axis=1)


def _inproj(x, g, w, *, tm, seq_len, transposed_kv):
    m = x.shape[0]
    per_row = seq_len // tm if transposed_kv else 1
    row = lambda c: pl.BlockSpec((tm, c), lambda i: (i, 0))
    grp = lambda c: pl.BlockSpec((KV_HEADS, tm, c), lambda i: (0, i, 0))
    tr = pl.BlockSpec((1, KV_COLS, tm), lambda i: (i // per_row, 0, i % per_row))
    kvt = jax.ShapeDtypeStruct((m // seq_len, KV_COLS, seq_len), F32)
    n_t = N_BRANCH if transposed_kv else 0
    out_shape = (
        jax.ShapeDtypeStruct((m, ATTN_WIDTH), BF16),
        jax.ShapeDtypeStruct((m, KV_COLS), F32), jax.ShapeDtypeStruct((m, KV_COLS), F32),
        jax.ShapeDtypeStruct((m, KV_COLS), F32),
        jax.ShapeDtypeStruct((m, GATE_PAD), F32), jax.ShapeDtypeStruct((m, CONV_CH), F32),
        jax.ShapeDtypeStruct((KV_HEADS, m, 2 * HEAD_DIM), BF16),
        jax.ShapeDtypeStruct((KV_HEADS, m, 2 * HEAD_DIM), BF16),
        jax.ShapeDtypeStruct((KV_HEADS, m, HEAD_DIM), BF16),
        jax.ShapeDtypeStruct((KV_HEADS, m, 2 * HEAD_DIM), BF16),
    ) + (kvt,) * n_t
    out_specs = (row(ATTN_WIDTH), row(KV_COLS), row(KV_COLS), row(KV_COLS), row(GATE_PAD), row(CONV_CH),
                 grp(2 * HEAD_DIM), grp(2 * HEAD_DIM), grp(HEAD_DIM), grp(2 * HEAD_DIM)) + (tr,) * n_t
    return pl.pallas_call(
        functools.partial(_inproj_kernel, tm=tm, seq_len=seq_len),
        grid=(m // tm,),
        in_specs=[row(D_MODEL), _const_spec((1, D_MODEL)), _const_spec((D_MODEL, IN_COLS))],
        out_specs=out_specs, out_shape=out_shape,
        compiler_params=_params("parallel"), name="inproj",
    )(x, g, w)


def _conv_ln_silu(y, b, lg, lb):
    y = y + b
    mu = jnp.mean(y, axis=-1, keepdims=True)
    var = jnp.mean(jnp.square(y - mu), axis=-1, keepdims=True)
    return jax.nn.silu((y - mu) * lax.rsqrt(var + EPS) * lg + lb)


def _conv_kernel(prev_ref, a_ref, w_ref, b_ref, lg_ref, lb_ref, o_ref, ext_ref, sh_ref, *, tc):
    first = pl.program_id(1) == 0
    ext_ref[0:CONV_HALO, :] = jnp.where(first, 0.0, prev_ref[0])
    ext_ref[CONV_HALO:CONV_HALO + tc, :] = a_ref[0]
    lead = CONV_HALO - (CONV_WIDTH - 1)
    span = sh_ref.shape[1]
    for r in range(1, SUBLANE):
        sh_ref[r - 1] = ext_ref[r:r + span, :]

    for r0 in range(0, tc, CONV_ROWS):
        acc = jnp.zeros((CONV_ROWS, CONV_CH), F32)
        for k in range(CONV_WIDTH):
            r = (lead + k) % SUBLANE
            i0 = r0 + lead + k - r
            rows = ext_ref[i0:i0 + CONV_ROWS, :] if r == 0 else sh_ref[r - 1, i0:i0 + CONV_ROWS, :]
            acc = acc + jnp.concatenate([w_ref[k]] * (CONV_ROWS // SUBLANE), axis=0) * rows
        o_ref[0, r0:r0 + CONV_ROWS, :] = _conv_ln_silu(acc, b_ref[...], lg_ref[...], lb_ref[...]).astype(BF16)


def _conv_prompt(a, w, b, lg, lb, *, tc):
    bsz, t, _ = a.shape
    hb = tc // CONV_HALO
    return pl.pallas_call(
        functools.partial(_conv_kernel, tc=tc),
        grid=(bsz, t // tc),
        in_specs=[pl.BlockSpec((1, CONV_HALO, CONV_CH), lambda bi, i: (bi, jnp.maximum(i * hb - 1, 0), 0)),
                  pl.BlockSpec((1, tc, CONV_CH), lambda bi, i: (bi, i, 0)),
                  _const_spec((CONV_WIDTH, SUBLANE, CONV_CH)), _const_spec((1, CONV_CH)),
                  _const_spec((1, CONV_CH)), _const_spec((1, CONV_CH))],
        out_specs=pl.BlockSpec((1, tc, CONV_CH), lambda bi, i: (bi, i, 0)),
        out_shape=jax.ShapeDtypeStruct((bsz, t, CONV_CH), BF16),
        scratch_shapes=[pltpu.VMEM((CONV_HALO + tc, CONV_CH), F32),
                        pltpu.VMEM((SUBLANE - 1, CONV_HALO + tc - SUBLANE, CONV_CH), F32)],
        compiler_params=_params("parallel", "parallel"), name="conv_prompt",
    )(a, a, w, b, lg, lb)


def _compress_rows(h0, h1_next, pos_term, w2):
    hid = pos_term + h0 + h1_next
    return _dot(jax.nn.gelu(hid).astype(BF16), w2)


def _pos_term(posx_ref, wc_ref):
    hp = _dot(posx_ref[...], wc_ref[...])
    return hp[0:1, :KV_COLS] + hp[1:2, KV_COLS:]


def _compress_kernel(x_ref, wc_ref, posx_ref, w2_ref, kc_ref, vc_ref, pos_ref):
    hh = _dot(x_ref[0].astype(BF16), wc_ref[...])
    h1 = hh[:, KV_COLS:]
    h1_next = jnp.concatenate([h1[1:], jnp.zeros((1, KV_COLS), F32)], axis=0)
    pos_term = _pos_term(posx_ref, wc_ref)
    pos_ref[...] = jnp.broadcast_to(pos_term, pos_ref.shape)
    cmp = _compress_rows(hh[:, :KV_COLS], h1_next, pos_term, w2_ref[...])
    for g in range(KV_HEADS):
        kc_ref[0, g] = cmp[:, g * HEAD_DIM:(g + 1) * HEAD_DIM].astype(BF16)
        vc_ref[0, g] = cmp[:, (KV_HEADS + g) * HEAD_DIM:(KV_HEADS + g + 1) * HEAD_DIM].astype(BF16)


def _compress_prompt(kvc_chunks, wc, posx, w2):
    bsz, n_ch, _ = kvc_chunks.shape
    tok = jax.ShapeDtypeStruct((bsz, KV_HEADS, n_ch, HEAD_DIM), BF16)
    tok_spec = pl.BlockSpec((1, KV_HEADS, n_ch, HEAD_DIM), lambda bi: (bi, 0, 0, 0))
    return pl.pallas_call(
        _compress_kernel,
        grid=(bsz,),
        in_specs=[pl.BlockSpec((1, n_ch, CHUNK_COLS), lambda bi: (bi, 0, 0)),
                  _const_spec(wc.shape), _const_spec(posx.shape), _const_spec(w2.shape)],
        out_specs=(tok_spec, tok_spec, pl.BlockSpec((8, KV_COLS), lambda bi: (0, 0))),
        out_shape=(tok, tok, jax.ShapeDtypeStruct((8, KV_COLS), F32)),
        compiler_params=_params("arbitrary"), name="compress_prompt",
    )(kvc_chunks, wc, posx, w2)


def _block_scores(imp, ssel, t_col, ns):
    s = _dot_exact_lhs(imp, ssel)
    j = lax.broadcasted_iota(jnp.int32, s.shape, 1)
    cur = t_col // SEL_BLOCK
    valid = j * SEL_BLOCK <= t_col
    forced = valid & ((j == 0) | (j == cur) | (j == cur - 1))
    s = jnp.where(forced, BIG, jnp.where(valid, s, -BIG))
    return jnp.where(j < ns, s, -3.0 * BIG)


def _nsa_prompt_kernel(q_ref, gate_ref, ksa_ref, vs_ref, kw_ref, vw_ref, kc_ref, vc_ref, ssel_ref, o_ref,
                       qa_ref, s_ref, p_ref, part_ref, m_ref, acc_ref, bias_ref, *, tq, ns):
    i = pl.program_id(1)
    q0 = i * tq
    t_col = q0 + lax.broadcasted_iota(jnp.int32, (tq, 1), 0)
    t4 = jnp.concatenate([t_col] * GROUP, axis=0)
    gates = gate_ref[...]
    n_blk = kc_ref.shape[2]
    rows = GROUP * tq
    end = lax.broadcasted_iota(jnp.int32, (1, n_blk), 1) * CMP_STRIDE + (CMP_LEN - 1)
    cmask = end <= t4

    sub = lax.broadcasted_iota(jnp.int32, (SUBLANE, tq), 0)
    o_cs, q4s = [], []
    for g in range(KV_HEADS):
        q4 = jnp.concatenate(
            [q_ref[:, (GROUP * g + h) * HEAD_DIM:(GROUP * g + h + 1) * HEAD_DIM] for h in range(GROUP)], axis=0)
        q4s.append(q4)

        e_c, r_c = _softmax_parts(_dot_nt(q4, kc_ref[0, g]), cmask)
        o_cs.append(_dot(e_c.astype(BF16), vc_ref[0, g]) * r_c)
        p_c = e_c * r_c
        imp = p_c[0:tq] + p_c[tq:2 * tq] + p_c[2 * tq:3 * tq] + p_c[3 * tq:4 * tq]

        s_t = _block_scores(imp, ssel_ref[...], t_col, ns).T
        blocks = [s_t[v * SUBLANE:(v + 1) * SUBLANE] for v in range(ns // SUBLANE)]
        ranks = [jnp.zeros((SUBLANE, tq), F32) for _ in blocks]
        for i2 in range(ns):
            row = s_t[i2:i2 + 1, :]
            for v, blk in enumerate(blocks):
                if v > i2 // SUBLANE:
                    beats = row >= blk
                elif v < i2 // SUBLANE:
                    beats = row > blk
                else:
                    beats = (row > blk) | ((row == blk) & (sub > i2 % SUBLANE))
                ranks[v] = ranks[v] + jnp.where(beats, 1.0, 0.0)
        sel_t = jnp.concatenate([jnp.where(r < N_SELECT, 0.0, NEG) for r in ranks]
                                + [jnp.full((LANE - ns, tq), NEG, F32)], axis=0)
        sel = sel_t.T
        selbias = jnp.concatenate([sel[:, :HEAD_DIM].astype(BF16)] * GROUP, axis=0)
        qa_ref[g] = jnp.concatenate([q4, selbias], axis=1)

    def set_bias(b, k0, width, key_ok):
        t = t_col
        key = k0 + lax.broadcasted_iota(jnp.int32, (1, width), 1)
        bias_ref[b, :, 0:width] = jnp.where(key_ok(t, key), 0.0, NEG)

    def scores(slot, q, k_ref, g, k0, width, bias):
        s_ref[slot, :, 0:width] = _dot_nt(q, k_ref[g, pl.ds(k0, width), :])

        for r0 in range(0, rows, ATT_ROWS):
            sb = s_ref[slot, r0:r0 + ATT_ROWS, 0:width]
            if bias is not None:
                sb = sb + bias_ref[bias, r0 % tq:r0 % tq + ATT_ROWS, 0:width]
                s_ref[slot, r0:r0 + ATT_ROWS, 0:width] = sb
            mx = sb[:, 0:LANE]
            for j in range(1, width // LANE):
                mx = jnp.maximum(mx, sb[:, j * LANE:(j + 1) * LANE])
            part_ref[slot, r0:r0 + ATT_ROWS, :] = mx
        return jnp.max(part_ref[slot], axis=-1, keepdims=True)

    def weights(slot, width):
        for r0 in range(0, rows, ATT_ROWS):
            m = m_ref[slot, r0:r0 + ATT_ROWS, :]
            sb = s_ref[slot, r0:r0 + ATT_ROWS, 0:width]
            p_ref[slot, r0:r0 + ATT_ROWS, 0:width] = jnp.exp(sb - jnp.concatenate([m] * (width // LANE), axis=1)).astype(BF16)

    w0 = pl.multiple_of(jnp.maximum(q0 - WINDOW, 0), Q_TILE)
    set_bias(1, w0, WIN_KEYS, lambda t, key: (t - key >= 0) & (t - key < WINDOW))

    def window(g):
        slot = KV_HEADS + g
        m_ref[slot] = jnp.broadcast_to(scores(slot, q4s[g], kw_ref, g, w0, WIN_KEYS, 1), (rows, LANE))
        weights(slot, WIN_KEYS)
        acc_w = _dot(p_ref[slot, :, 0:WIN_KEYS], vw_ref[g, pl.ds(w0, WIN_KEYS), :])
        return acc_w[:, :HEAD_DIM] * (1.0 / acc_w[:, HEAD_DIM:HEAD_DIM + 1])

    o_w = [window(0)]

    for g in range(KV_HEADS):
        m_ref[g] = jnp.full((rows, LANE), NEG, F32)
        acc_ref[g] = jnp.zeros((rows, LANE), F32)

    def chunk(k0, bias):
        blk_max = [scores(g, qa_ref[g], ksa_ref, g, k0, SEL_CHUNK, bias) for g in range(KV_HEADS)]
        for g in range(KV_HEADS):
            m_old = m_ref[g]
            m_new = jnp.maximum(m_old, blk_max[g])
            m_ref[g] = m_new
            weights(g, SEL_CHUNK)
            acc_ref[g] = (jnp.exp(m_old - m_new) * acc_ref[g]
                          + _dot(p_ref[g, :, 0:SEL_CHUNK], vs_ref[g, pl.ds(k0, SEL_CHUNK), :]))

    n_full = q0 // SEL_CHUNK

    def full_chunk(c, carry):
        chunk(pl.multiple_of(c * SEL_CHUNK, SEL_CHUNK), None)
        return carry

    lax.fori_loop(0, n_full, full_chunk, 0)

    tail0 = pl.multiple_of(jnp.maximum(q0 + tq - SEL_CHUNK, 0), Q_TILE)
    set_bias(0, tail0, SEL_CHUNK, lambda t, key: (key <= t) & (key >= n_full * SEL_CHUNK))
    chunk(tail0, 0)
    o_w.append(window(1))

    for g in range(KV_HEADS):
        o_c = o_cs[g]
        acc_s = acc_ref[g]
        o_s = acc_s[:, :HEAD_DIM] * (1.0 / acc_s[:, HEAD_DIM:HEAD_DIM + 1])
        for h in range(GROUP):
            hh = GROUP * g + h
            r = slice(h * tq, (h + 1) * tq)
            c = N_BRANCH * hh
            o = gates[:, c:c + 1] * o_c[r] + gates[:, c + 1:c + 2] * o_s[r] + gates[:, c + 2:c + 3] * o_w[g][r]
            o_ref[:, hh * HEAD_DIM:(hh + 1) * HEAD_DIM] = o.astype(BF16)


def _nsa_prompt(q, gates, ksa, vs, kw, vw, kc, vc, ssel, *, bsz, seq_len):
    tq = Q_TILE
    nt = seq_len // tq
    ns = seq_len // SEL_BLOCK
    n_blk = kc.shape[2]
    rows = GROUP * tq
    row = lambda c: pl.BlockSpec((tq, c), lambda bi, i: (bi * nt + i, 0))
    seq = lambda c: pl.BlockSpec((KV_HEADS, seq_len, c), lambda bi, i: (0, bi, 0))
    tok = pl.BlockSpec((1, KV_HEADS, n_blk, HEAD_DIM), lambda bi, i: (bi, 0, 0, 0))
    return pl.pallas_call(
        functools.partial(_nsa_prompt_kernel, tq=tq, ns=ns),
        grid=(bsz, nt),
        in_specs=[row(ATTN_WIDTH), row(GATE_PAD), seq(2 * HEAD_DIM), seq(2 * HEAD_DIM), seq(HEAD_DIM),
                  seq(2 * HEAD_DIM), tok, tok, _const_spec(ssel.shape)],
        out_specs=row(ATTN_WIDTH),
        out_shape=jax.ShapeDtypeStruct((bsz * seq_len, ATTN_WIDTH), BF16),
        scratch_shapes=[pltpu.VMEM((KV_HEADS, rows, 2 * HEAD_DIM), BF16),
                        pltpu.VMEM((2 * KV_HEADS, rows, WIN_KEYS), F32),
                        pltpu.VMEM((2 * KV_HEADS, rows, WIN_KEYS), BF16), pltpu.VMEM((2 * KV_HEADS, rows, LANE), F32),
                        pltpu.VMEM((2 * KV_HEADS, rows, LANE), F32), pltpu.VMEM((KV_HEADS, rows, LANE), F32),
                        pltpu.VMEM((2, tq, WIN_KEYS), F32)],
        compiler_params=_params("parallel", "parallel"), name="nsa_prompt",
    )(q, gates, ksa, vs, kw, vw, kc, vc, ssel)


def _swiglu_residual(x1, gf, wg_ref, wu_ref, wd_ref):
    h = _rms(x1, gf).astype(BF16)
    act = jax.nn.silu(_dot(h, wg_ref[...])) * _dot(h, wu_ref[...])
    return x1 + _dot(act.astype(BF16), wd_ref[...])


def _ffn0_kernel(x_ref, a_ref, c_ref, woa_ref, woc_ref, gf_ref, wg_ref, wu_ref, wd_ref, o_ref):
    x1 = x_ref[...] + (_dot(a_ref[...], woa_ref[...]) + _dot(c_ref[...], woc_ref[...]))
    o_ref[...] = _swiglu_residual(x1, gf_ref[...], wg_ref, wu_ref, wd_ref)


def _ffn0(x, a, c, woa, woc, gf, wg, wu, wd, *, tm):
    m = x.shape[0]
    row = lambda cols: pl.BlockSpec((tm, cols), lambda i: (i, 0))
    return pl.pallas_call(
        _ffn0_kernel,
        grid=(m // tm,),
        in_specs=[row(D_MODEL), row(ATTN_WIDTH), row(CONV_CH), _const_spec(woa.shape), _const_spec(woc.shape),
                  _const_spec(gf.shape), _const_spec(wg.shape), _const_spec(wu.shape), _const_spec(wd.shape)],
        out_specs=row(D_MODEL), out_shape=jax.ShapeDtypeStruct((m, D_MODEL), F32),
        compiler_params=_params("parallel"), name="outproj_ffn",
    )(x, a, c, woa, woc, gf, wg, wu, wd)


def _pool_mix(x, h, win_sums, cnts, pw_ref, ps):
    ys = []
    for g in range(len(POOL_WINDOWS)):
        z = win_sums[g] / cnts[g] - h[:, g * POOL_GROUP:(g + 1) * POOL_GROUP]
        ys.append(_dot(z.astype(BF16), pw_ref[g]))
    return x + jnp.concatenate(ys, axis=1) * ps


def _layer1_prompt_kernel(xprev_ref, x_ref, gm_ref, pw_ref, ps_ref, gf_ref, wg_ref, wu_ref, wd_ref, gfin_ref,
                          y_ref, hst_ref, *, tm):
    i = pl.program_id(1)
    x = x_ref[0]
    h = _rms(x, gm_ref[...])
    hprev = jnp.where(i == 0, 0.0, _rms(xprev_ref[0], gm_ref[...]))
    hst_ref[0] = h[tm - POOL_HALO:, :]
    pos = i * tm + lax.broadcasted_iota(jnp.int32, (tm, 1), 0)
    sums, cnts = [], []
    for g, w in enumerate(POOL_WINDOWS):
        e = jnp.concatenate([hprev[:, g * POOL_GROUP:(g + 1) * POOL_GROUP],
                             h[:, g * POOL_GROUP:(g + 1) * POOL_GROUP]], axis=0)
        span = 1
        while span < w:
            e = e[span:] + e[:-span]
            span *= 2
        first = POOL_HALO - (w - 1)
        sums.append(e[first:first + tm])
        cnts.append(jnp.minimum(w, pos + 1).astype(F32))
    x1 = _pool_mix(x, h, sums, cnts, pw_ref, ps_ref[...])
    x2 = _swiglu_residual(x1, gf_ref[...], wg_ref, wu_ref, wd_ref)
    y_ref[0] = _rms(x2, gfin_ref[...])


def _layer1_prompt(x, gm, pw, ps, gf, wg, wu, wd, gfin, *, tm):
    bsz, t, _ = x.shape
    hb = tm // POOL_HALO
    return pl.pallas_call(
        functools.partial(_layer1_prompt_kernel, tm=tm),
        grid=(bsz, t // tm),
        in_specs=[pl.BlockSpec((1, POOL_HALO, D_MODEL), lambda bi, i: (bi, jnp.maximum(i * hb - 1, 0), 0)),
                  pl.BlockSpec((1, tm, D_MODEL), lambda bi, i: (bi, i, 0)),
                  _const_spec(gm.shape), _const_spec(pw.shape), _const_spec(ps.shape), _const_spec(gf.shape),
                  _const_spec(wg.shape), _const_spec(wu.shape), _const_spec(wd.shape), _const_spec(gfin.shape)],
        out_specs=(pl.BlockSpec((1, tm, D_MODEL), lambda bi, i: (bi, i, 0)),
                   pl.BlockSpec((1, POOL_HALO, D_MODEL), lambda bi, i: (bi, 0, 0))),
        out_shape=(jax.ShapeDtypeStruct((bsz, t, D_MODEL), F32),
                   jax.ShapeDtypeStruct((bsz, POOL_HALO, D_MODEL), F32)),
        compiler_params=_params("parallel", "arbitrary"), name="layer1_prompt",
    )(x, x, gm, pw, ps, gf, wg, wu, wd, gfin)


def _layer1_sample_kernel(x_ref, hist_ref, gm_ref, pw_ref, ps_ref, gf_ref, wg_ref, wu_ref, wd_ref, gfin_ref,
                          y_ref, h_ref, *, first_pos):
    x = x_ref[...]
    h = _rms(x, gm_ref[...])
    h_ref[...] = h
    sums, cnts = [], []
    for g, w in enumerate(POOL_WINDOWS):
        c = slice(g * POOL_GROUP, (g + 1) * POOL_GROUP)
        s = h[:, c]
        for k in range(1, w):
            s = s + hist_ref[POOL_BUF - k][:, c]
        sums.append(s)
        cnts.append(float(min(w, first_pos + 1)))
    x1 = _pool_mix(x, h, sums, cnts, pw_ref, ps_ref[...])
    x2 = _swiglu_residual(x1, gf_ref[...], wg_ref, wu_ref, wd_ref)
    y_ref[...] = _rms(x2, gfin_ref[...])


def _layer1_sample(x, hist, gm, pw, ps, gf, wg, wu, wd, gfin, *, first_pos):
    m = x.shape[0]
    args = (x, hist, gm, pw, ps, gf, wg, wu, wd, gfin)
    out = jax.ShapeDtypeStruct((m, D_MODEL), F32)
    return pl.pallas_call(
        functools.partial(_layer1_sample_kernel, first_pos=first_pos),
        grid=(1,),
        in_specs=[_const_spec(a.shape) for a in args],
        out_specs=(_const_spec((m, D_MODEL)), _const_spec((m, D_MODEL))), out_shape=(out, out),
        compiler_params=_params("arbitrary"), name="layer1_sample",
    )(*args)


def _page_copy(cache_ref, xt_ref, sem, phys, p):
    return pltpu.make_async_copy(cache_ref.at[phys], xt_ref.at[p], sem)


def _cmp_sample_kernel(pt_ref, q_ref, cache_ref, wkv_ref, pos_ref, w2_ref, ssel_ref, oc_ref, idx_ref,
                       xt_ref, xl_ref, hbuf_ref, sem, *, n_pages, past, ns, nsp):
    b = pl.program_id(0)
    n_ch = n_pages * (PAGE_SIZE // CMP_STRIDE)

    def gather(sample):
        def start(p, carry):
            _page_copy(cache_ref, xt_ref, sem, pt_ref[sample * n_pages + p], p).start()
            return carry

        lax.fori_loop(0, n_pages, start, 0)

    def wait(p, carry):
        _page_copy(cache_ref, xt_ref, sem, 0, p).wait()
        return carry

    per_page = PAGE_SIZE // CMP_STRIDE
    r_out = lax.broadcasted_iota(jnp.int32, (PAGE_SIZE, PAGE_SIZE), 0)
    r_in = lax.broadcasted_iota(jnp.int32, (PAGE_SIZE, PAGE_SIZE), 1)
    perm = (r_in == (r_out % per_page) * CMP_STRIDE + r_out // per_page).astype(BF16)

    batch = 8

    def to_rows(i, carry):
        p0 = pl.multiple_of(i * batch, batch)
        pages = xt_ref[pl.ds(p0, batch)].reshape(batch * KV_COLS, PAGE_SIZE)
        xp = _dot_nt(perm, pages.astype(BF16))
        for k in range(batch):
            c0 = pl.multiple_of((p0 + k) * per_page, per_page)
            for l in range(CMP_STRIDE):
                xl_ref[l, pl.ds(c0, per_page), :] = xp[l * per_page:(l + 1) * per_page, k * KV_COLS:(k + 1) * KV_COLS]
        return carry

    @pl.when(b == 0)
    def _():
        gather(0)

    lax.fori_loop(0, n_pages, wait, 0)
    lax.fori_loop(0, n_pages // batch, to_rows, 0)

    @pl.when(b + 1 < pl.num_programs(0))
    def _():
        gather(b + 1)

    rows = min(256, n_ch)
    half = KV_COLS // 2
    for r0 in range(0, n_ch, rows):
        for kv in range(2):
            chunk_rows = jnp.concatenate(
                [xl_ref[l, r0:r0 + rows, kv * half:(kv + 1) * half].astype(BF16) for l in range(CMP_STRIDE)], axis=1)
            hbuf_ref[r0:r0 + rows, kv * KV_COLS:(kv + 1) * KV_COLS] = _dot(chunk_rows, wkv_ref[kv])
    hbuf_ref[n_ch:n_ch + 8, :] = jnp.zeros((8, 2 * KV_COLS), F32)
    first = lambda r: jnp.concatenate([hbuf_ref[r, 0:half], hbuf_ref[r, KV_COLS:KV_COLS + half]], axis=1)
    second = lambda r: jnp.concatenate([hbuf_ref[r, half:KV_COLS], hbuf_ref[r, KV_COLS + half:]], axis=1)
    cmp = _compress_rows(first(pl.ds(0, n_ch)), second(pl.ds(1, n_ch)), pos_ref[0:1, :], w2_ref[...])

    q8 = q_ref[0]
    head = lax.broadcasted_iota(jnp.int32, (N_HEADS, 1), 0)
    end = lax.broadcasted_iota(jnp.int32, (1, n_ch), 1) * CMP_STRIDE + (CMP_LEN - 1)
    cmask = jnp.broadcast_to(end <= past, (N_HEADS, n_ch))
    o_c = jnp.zeros((N_HEADS, HEAD_DIM), F32)
    jl = lax.broadcasted_iota(jnp.int32, (1, nsp), 1)
    ii = lax.broadcasted_iota(jnp.int32, (nsp, nsp), 0)
    jj = lax.broadcasted_iota(jnp.int32, (nsp, nsp), 1)
    cur = past // SEL_BLOCK
    t_col = jnp.full((N_HEADS, 1), past, jnp.int32)
    for g in range(KV_HEADS):
        in_group = head // GROUP == g
        kc = cmp[:, g * HEAD_DIM:(g + 1) * HEAD_DIM].astype(BF16)
        vc = cmp[:, (KV_HEADS + g) * HEAD_DIM:(KV_HEADS + g + 1) * HEAD_DIM].astype(BF16)
        sc = _dot_nt(q8, kc)
        p_c = _softmax_rows(jnp.where(cmask, sc, NEG), cmask)
        o_c = jnp.where(in_group, _dot(p_c.astype(BF16), vc), o_c)
        imp = jnp.sum(jnp.where(in_group, p_c, 0.0), axis=0, keepdims=True)
        s_b = _block_scores(jnp.broadcast_to(imp, (N_HEADS, n_ch)), ssel_ref[...], t_col, ns)
        s_b = jnp.broadcast_to(s_b[0:1], (nsp, nsp))
        s_t = s_b.T
        beats = (s_t > s_b) | ((s_t == s_b) & (ii < jj))
        rank = jnp.sum(jnp.where(beats, 1.0, 0.0), axis=0, keepdims=True)
        sel = (rank < N_SELECT) & (jl != cur)
        sel_f = jnp.where(sel, 1.0, 0.0)
        before = _dot(jnp.broadcast_to(sel_f, (N_HEADS, nsp)).astype(BF16),
                      jnp.where(ii < jj, 1.0, 0.0).astype(BF16))[0:1]
        slot = lax.broadcasted_iota(jnp.int32, (N_SELECT, nsp), 0)
        pick = jnp.broadcast_to(sel, (N_SELECT, nsp)) & (jnp.broadcast_to(before, (N_SELECT, nsp)) == slot.astype(F32))
        blk = jnp.sum(jnp.where(pick, lax.broadcasted_iota(jnp.int32, (N_SELECT, nsp), 1), 0), axis=1, keepdims=True)
        idx_ref[0, g] = blk
    oc_ref[0] = o_c


def _cmp_sample(page_table, q8, cache_pages, wc, pos_term, w2, ssel, *, past, ns):
    bsz, n_pages = page_table.shape
    n_ch = n_pages * (PAGE_SIZE // CMP_STRIDE)
    nsp = ssel.shape[1]
    grid_spec = pltpu.PrefetchScalarGridSpec(
        num_scalar_prefetch=1, grid=(bsz,),
        in_specs=[pl.BlockSpec((1, N_HEADS, HEAD_DIM), lambda b, pt: (b, 0, 0)),
                  pl.BlockSpec(memory_space=pl.ANY),
                  _const_spec(wc.shape), _const_spec(pos_term.shape), _const_spec(w2.shape), _const_spec(ssel.shape)],
        out_specs=(pl.BlockSpec((1, N_HEADS, HEAD_DIM), lambda b, pt: (b, 0, 0)),
                   pl.BlockSpec((1, KV_HEADS, N_SELECT, 1), lambda b, pt: (b, 0, 0, 0))),
        scratch_shapes=[pltpu.VMEM((n_pages, KV_COLS, PAGE_SIZE), F32), pltpu.VMEM((CMP_STRIDE, n_ch, KV_COLS), F32),
                        pltpu.VMEM((n_ch + 8, 2 * KV_COLS), F32), pltpu.SemaphoreType.DMA(())],
    )
    return pl.pallas_call(
        functools.partial(_cmp_sample_kernel, n_pages=n_pages, past=past, ns=ns, nsp=nsp),
        grid_spec=grid_spec,
        out_shape=(jax.ShapeDtypeStruct((bsz, N_HEADS, HEAD_DIM), F32),
                   jax.ShapeDtypeStruct((bsz, KV_HEADS, N_SELECT, 1), jnp.int32)),
        compiler_params=_params("arbitrary"), name="cmp_sample",
    )(page_table.reshape(-1), q8, cache_pages, wc, pos_term, w2, ssel)


def _slab_copy(cache_ref, dst_ref, sem, src, g, k):
    return pltpu.make_async_copy(cache_ref.at[src], dst_ref.at[g, :, pl.ds(k * PAGE_SIZE, PAGE_SIZE)], sem)


def _mix_sample_kernel(idx_ref, pt_ref, q_ref, gate_ref, oc_ref, kvs_ref, kvw_ref, win_ref, cst_ref, a_ref,
                       cw_ref, cb_ref, lg_ref, lb_ref, cache_ref, att_ref, conv_ref, kt_ref, vt_ref, sem,
                       *, n_pages, n_gather, past, w_buf):
    b = pl.program_id(0)
    per_page = PAGE_SIZE // SEL_BLOCK
    halves = []
    for g in range(KV_HEADS):
        for k in range(n_gather):
            blk = idx_ref[(b * KV_HEADS + g) * N_SELECT + k]
            slab = pt_ref[b * n_pages + blk // per_page] * (2 * KV_HEADS) + g
            _slab_copy(cache_ref, kt_ref, sem, slab, g, k).start()
            _slab_copy(cache_ref, vt_ref, sem, slab + KV_HEADS, g, k).start()
            halves.append(blk % per_page)

    y = (jnp.sum(cw_ref[0:CONV_WIDTH - 1, :] * cst_ref[0], axis=0, keepdims=True)
         + cw_ref[CONV_WIDTH - 1:CONV_WIDTH, :] * a_ref[0])
    conv_ref[0] = _conv_ln_silu(y, cb_ref[...], lg_ref[...], lb_ref[...])

    def wait(s, carry):
        _slab_copy(cache_ref, kt_ref, sem, 0, 0, 0).wait()
        return carry

    lax.fori_loop(0, 2 * KV_HEADS * n_gather, wait, 0)

    q8 = q_ref[0]
    q8f = q8.astype(F32)
    head = lax.broadcasted_iota(jnp.int32, (N_HEADS, 1), 0)
    gates = gate_ref[0]
    kvs_new, kvw_new = kvs_ref[0], kvw_ref[0]
    jw = lax.broadcasted_iota(jnp.int32, (1, w_buf), 1)
    wdiff = w_buf - jw
    wmask = jnp.broadcast_to((wdiff < WINDOW) & (past - wdiff >= 0), (N_HEADS, w_buf))
    half_of_lane = lax.broadcasted_iota(jnp.int32, (1, PAGE_SIZE), 1) // SEL_BLOCK
    out = jnp.zeros((N_HEADS, HEAD_DIM), F32)

    def with_new_row(kt, vt, mask, k_new, v_new):
        s = jnp.where(mask, _dot(q8, kt.astype(BF16)), NEG)
        s_new = jnp.sum(q8f * k_new.astype(BF16).astype(F32), axis=-1, keepdims=True)
        m = jnp.maximum(jnp.max(s, axis=-1, keepdims=True), s_new)
        e = jnp.where(mask, jnp.exp(s - m), 0.0)
        e_new = jnp.exp(s_new - m)
        num = _dot_nt(e.astype(BF16), vt.astype(BF16)) + e_new * v_new
        return num / (jnp.sum(e, axis=-1, keepdims=True) + e_new)

    for g in range(KV_HEADS):
        kc, vc = slice(g * HEAD_DIM, (g + 1) * HEAD_DIM), slice((KV_HEADS + g) * HEAD_DIM, (KV_HEADS + g + 1) * HEAD_DIM)
        smask = jnp.concatenate([half_of_lane == halves[g * n_gather + k] for k in range(n_gather)], axis=1)
        smask = jnp.broadcast_to(smask, (N_HEADS, n_gather * PAGE_SIZE))
        o_s = with_new_row(kt_ref[g], vt_ref[g], smask, kvs_new[:, kc], kvs_new[:, vc])
        o_w = with_new_row(win_ref[0, g], win_ref[0, KV_HEADS + g], wmask, kvw_new[:, kc], kvw_new[:, vc])
        mixed = gates[:, 0:1] * oc_ref[0] + gates[:, 1:2] * o_s + gates[:, 2:3] * o_w
        out = jnp.where(head // GROUP == g, mixed, out)
    att_ref[0] = out


def _mix_sample(idx, page_table, q8, gates8, o_c, kvs, kvw, win, conv_state, a, cw, cb, lg, lb, cache_slabs,
                *, past, n_gather):
    bsz, n_pages = page_table.shape
    w_buf = win.shape[-1]
    one = lambda *shape: pl.BlockSpec((1,) + shape, lambda b, *_: (b,) + (0,) * len(shape))
    const = lambda shape: pl.BlockSpec(shape, lambda b, *_: (0,) * len(shape))
    gathered = pltpu.VMEM((KV_HEADS, HEAD_DIM, n_gather * PAGE_SIZE), F32)
    grid_spec = pltpu.PrefetchScalarGridSpec(
        num_scalar_prefetch=2, grid=(bsz,),
        in_specs=[one(N_HEADS, HEAD_DIM), one(N_HEADS, N_BRANCH), one(N_HEADS, HEAD_DIM), one(1, KV_COLS),
                  one(1, KV_COLS), one(2 * KV_HEADS, HEAD_DIM, w_buf), one(CONV_WIDTH - 1, CONV_CH), one(1, CONV_CH),
                  const(cw.shape), const(cb.shape), const(lg.shape), const(lb.shape),
                  pl.BlockSpec(memory_space=pl.ANY)],
        out_specs=(one(N_HEADS, HEAD_DIM), one(1, CONV_CH)),
        scratch_shapes=[gathered, gathered, pltpu.SemaphoreType.DMA(())],
    )
    return pl.pallas_call(
        functools.partial(_mix_sample_kernel, n_pages=n_pages, n_gather=n_gather, past=past, w_buf=w_buf),
        grid_spec=grid_spec,
        out_shape=(jax.ShapeDtypeStruct((bsz, N_HEADS, HEAD_DIM), F32),
                   jax.ShapeDtypeStruct((bsz, 1, CONV_CH), F32)),
        compiler_params=_params("arbitrary"), name="mix_sample",
    )(idx.reshape(-1), page_table.reshape(-1), q8, gates8, o_c, kvs, kvw, win, conv_state, a, cw, cb, lg, lb,
      cache_slabs)


def _inproj_weight(w_in):
    off = ATTN_WIDTH + N_BRANCH * KV_COLS
    n_gate = N_HEADS * N_BRANCH
    gate = jnp.pad(w_in[:, off:off + n_gate], ((0, 0), (0, GATE_PAD - n_gate)))
    return jnp.concatenate([w_in[:, :off], gate, w_in[:, off + n_gate:]], axis=1).astype(BF16)


def _compress_weights(pos_k, w1_k, w2_k, pos_v, w1_v, w2_v):
    n_slot = 2 * KV_HEADS
    eye = jnp.eye(n_slot, dtype=F32)

    def place(per_slot):
        w = jnp.stack(per_slot, axis=0)
        full = jnp.einsum('jclde,jk->ljdcke', w, eye)
        return full.reshape(CHUNK_COLS, N_SUB * KV_COLS)

    w1k = w1_k.reshape(N_SUB, CMP_STRIDE, HEAD_DIM, HEAD_DIM)
    w1v = w1_v.reshape(N_SUB, CMP_STRIDE, HEAD_DIM, HEAD_DIM)
    wc = place([w1k, w1k, w1v, w1v]).astype(BF16)
    pk = pos_k.reshape(N_SUB, CMP_STRIDE, 1, HEAD_DIM)
    pv = pos_v.reshape(N_SUB, CMP_STRIDE, 1, HEAD_DIM)
    posx = jnp.concatenate([pk, pk, pv, pv], axis=2).reshape(N_SUB, CHUNK_COLS)
    posx = jnp.pad(posx, ((0, 8 - N_SUB), (0, 0))).astype(BF16)
    w2 = jnp.einsum('jef,jk->jekf', jnp.stack([w2_k, w2_k, w2_v, w2_v]), eye).reshape(KV_COLS, KV_COLS).astype(BF16)
    return wc, posx, w2


def _split_kv(wc):
    half = KV_COLS // 2
    w = wc.reshape(CMP_STRIDE, 2, half, N_SUB, 2, half)
    return jnp.stack([w[:, kv, :, :, kv, :].reshape(CMP_STRIDE * half, N_SUB * half) for kv in range(2)])


def _selection_matrix(n_rows, n_cols):
    n = jnp.arange(n_rows)[:, None]
    j = jnp.arange(n_cols)[None, :]
    cnt = jnp.zeros((n_rows, n_cols), F32)
    for m in range(SEL_RATIO):
        for sub in range(N_SUB):
            cnt = cnt + (SEL_RATIO * j + m - sub == n).astype(F32)
    return cnt.astype(BF16)


def kernel(x_prompt, x_sample, cache_cmp_kv, cache_slc_kv, state_win_kv, state_conv, state_pool, page_table,
           norm_mix, norm_ffn, norm_final, w_in_a, w_out_a, cmp_pos_k, cmp_w1_k, cmp_w2_k, cmp_pos_v, cmp_w1_v,
           cmp_w2_v, conv_w, conv_b, conv_ln_g, conv_ln_b, pool_w, pool_scale, w_ffn_gate, w_ffn_up, w_ffn_down):
    bp, tp, _ = x_prompt.shape
    bs, s_new, _ = x_sample.shape
    n_pages = page_table.shape[1]
    past = n_pages * PAGE_SIZE
    w_buf = state_win_kv.shape[2]
    n_pool = cache_cmp_kv.shape[1]
    ns_p = tp // SEL_BLOCK
    ns_s = -(-(past + s_new) // SEL_BLOCK)
    assert s_new == 1 and tp % SEL_CHUNK == 0 and tp >= WIN_KEYS and N_SELECT <= ns_p <= HEAD_DIM
    assert ns_s > N_SELECT and norm_mix.shape[0] == 2 and ns_p % SUBLANE == 0

    row = lambda v: v.reshape(1, -1)
    w_in = _inproj_weight(w_in_a[0])
    wc, posx, w2 = _compress_weights(cmp_pos_k[0], cmp_w1_k[0], cmp_w2_k[0], cmp_pos_v[0], cmp_w1_v[0], cmp_w2_v[0])
    woa, woc = w_out_a[0, :ATTN_WIDTH].astype(BF16), w_out_a[0, ATTN_WIDTH:].astype(BF16)
    wg, wu, wd = w_ffn_gate.astype(BF16), w_ffn_up.astype(BF16), w_ffn_down.astype(BF16)
    cw, cb, lg, lb = conv_w[0], row(conv_b[0]), row(conv_ln_g[0]), row(conv_ln_b[0])
    pw, ps = pool_w[0].astype(BF16), row(pool_scale[0])

    m = bp * tp
    xp = x_prompt.reshape(m, D_MODEL)
    q, kvc, kvs, kvw, gates, a, ksa, vs, kw, vw, kvct, kvst, kvwt = _inproj(
        xp, row(norm_mix[0]), w_in, tm=512, seq_len=tp, transposed_kv=True)
    cw_rows = jnp.broadcast_to(cw[:, None, :], (CONV_WIDTH, SUBLANE, CONV_CH))
    c_out = _conv_prompt(a.reshape(bp, tp, CONV_CH), cw_rows, cb, lg, lb, tc=512)
    kc, vc, pos_term = _compress_prompt(kvc.reshape(bp, tp // CMP_STRIDE, CHUNK_COLS), wc, posx, w2)
    a_out = _nsa_prompt(q, gates, ksa, vs, kw, vw, kc, vc, _selection_matrix(tp // CMP_STRIDE, LANE),
                        bsz=bp, seq_len=tp)
    xp = _ffn0(xp, a_out, c_out.reshape(m, CONV_CH), woa, woc, row(norm_ffn[0]), wg[0], wu[0], wd[0], tm=512)
    y_prompt, pool_tail = _layer1_prompt(xp.reshape(bp, tp, D_MODEL), row(norm_mix[1]), pw, ps, row(norm_ffn[1]),
                                         wg[1], wu[1], wd[1], row(norm_final), tm=512)
    kv6 = lambda z, b: z.reshape(1, b, -1, 2, KV_HEADS, HEAD_DIM)
    kv6_t = lambda zt: jnp.transpose(zt.reshape(bp, 2, KV_HEADS, HEAD_DIM, tp), (0, 4, 1, 2, 3))[None]
    new_cmp_p, new_slc_p = kv6_t(kvct), kv6_t(kvst)
    new_win_p = kv6_t(kvwt)[:, :, -min(WINDOW, tp):]
    new_conv_p = a.reshape(1, bp, tp, CONV_CH)[:, :, -(CONV_WIDTH - 1):]
    new_pool_p = pool_tail[None, :, -POOL_BUF:]

    xs = x_sample.reshape(bs, D_MODEL)
    q, kvc, kvs, kvw, gates, a = _inproj(xs, row(norm_mix[0]), w_in, tm=bs, seq_len=1, transposed_kv=False)[:6]
    q8 = q.reshape(bs, N_HEADS, HEAD_DIM)
    nsp = -(-ns_s // LANE) * LANE
    rows_last = lambda c: jnp.transpose(c, (0, 2, 3, 4, 1))
    o_c, idx = _cmp_sample(page_table, q8, rows_last(cache_cmp_kv[0]).reshape(n_pool, KV_COLS, PAGE_SIZE),
                           _split_kv(wc), pos_term, w2,
                           _selection_matrix(n_pages * (PAGE_SIZE // CMP_STRIDE), nsp), past=past, ns=ns_s)
    gates8 = gates[:, :N_HEADS * N_BRANCH].reshape(bs, N_HEADS, N_BRANCH)
    att, c_s = _mix_sample(idx, page_table, q8, gates8, o_c, kvs.reshape(bs, 1, KV_COLS), kvw.reshape(bs, 1, KV_COLS),
                           rows_last(state_win_kv[0]).reshape(bs, 2 * KV_HEADS, HEAD_DIM, w_buf), state_conv[0],
                           a.reshape(bs, 1, CONV_CH), cw, cb, lg, lb,
                           rows_last(cache_slc_kv[0]).reshape(n_pool * 2 * KV_HEADS, HEAD_DIM, PAGE_SIZE),
                           past=past, n_gather=N_SELECT - 1)
    xs = _ffn0(xs, att.reshape(bs, ATTN_WIDTH).astype(BF16), c_s.reshape(bs, CONV_CH).astype(BF16), woa, woc,
               row(norm_ffn[0]), wg[0], wu[0], wd[0], tm=bs)
    y_sample, h_s = _layer1_sample(xs, jnp.swapaxes(state_pool[0], 0, 1), row(norm_mix[1]), pw, ps, row(norm_ffn[1]),
                                   wg[1], wu[1], wd[1], row(norm_final), first_pos=past)
    new_win_s = jnp.concatenate([state_win_kv[0], kv6(kvw, bs)[0]], axis=1)[None, :, -w_buf:]
    new_conv_s = jnp.concatenate([state_conv[0], a.reshape(bs, 1, CONV_CH)], axis=1)[None, :, -(CONV_WIDTH - 1):]
    new_pool_s = jnp.concatenate([state_pool[0], h_s[:, None, :]], axis=1)[None, :, -POOL_BUF:]

    return (y_prompt, y_sample.reshape(bs, s_new, D_MODEL), new_cmp_p, kv6(kvc, bs), new_slc_p, kv6(kvs, bs),
            new_win_p, new_win_s, new_conv_p, new_conv_s, new_pool_p, new_pool_s)
```

```python
import functools

import jax
import jax.numpy as jnp
from jax import lax
from jax.experimental import pallas as pl
from jax.experimental.pallas import tpu as pltpu

F32 = jnp.float32
BF16 = jnp.bfloat16

D_MODEL = 1024
N_HEADS = 8
HEAD_DIM = 64
KV_HEADS = 2
GROUP = N_HEADS // KV_HEADS
ATTN_WIDTH = N_HEADS * HEAD_DIM
KV_COLS = 2 * KV_HEADS * HEAD_DIM
CMP_LEN = 32
CMP_STRIDE = 16
N_SUB = CMP_LEN // CMP_STRIDE
SEL_BLOCK = 64
SEL_RATIO = SEL_BLOCK // CMP_STRIDE
N_SELECT = 16
WINDOW = 512
PAGE_SIZE = 128
N_BRANCH = 3
SCALE = HEAD_DIM ** -0.5
CONV_CH = D_MODEL // 2
CONV_WIDTH = 31
POOL_WINDOWS = (2, 4, 8, 16)
POOL_GROUP = D_MODEL // len(POOL_WINDOWS)
POOL_BUF = max(POOL_WINDOWS) - 1
EPS = 1e-6
BIG = 1e9
NEG = -1e30
TINY = float(jnp.finfo(jnp.float32).tiny)

LANE = 128
SUBLANE = 8
CHUNK_COLS = CMP_STRIDE * KV_COLS
GATE_PAD = LANE
IN_COLS = ATTN_WIDTH + N_BRANCH * KV_COLS + GATE_PAD + 2 * CONV_CH
VMEM_LIMIT = 56 * 1024 * 1024

Q_TILE = 128
SEL_CHUNK = 512
WIN_KEYS = WINDOW + Q_TILE
ATT_ROWS = 64
CONV_HALO = 32
CONV_ROWS = 32
POOL_HALO = 16


def _dot(a, b):
    return jnp.dot(a, b, preferred_element_type=F32)


def _dot_nt(a, b):
    return lax.dot_general(a, b, (((1,), (1,)), ((), ())), preferred_element_type=F32)


def _dot_exact_lhs(a, b):
    hi = a.astype(BF16)
    r1 = a - hi.astype(F32)
    mid = r1.astype(BF16)
    lo = (r1 - mid.astype(F32)).astype(BF16)
    return _dot(hi, b) + _dot(mid, b) + _dot(lo, b)


def _rms(x, g):
    return x * lax.rsqrt(jnp.mean(x * x, axis=-1, keepdims=True) + EPS) * g


def _softmax_rows(s, mask):
    m = jnp.max(s, axis=-1, keepdims=True)
    e = jnp.where(mask, jnp.exp(s - m), 0.0)
    return e / jnp.maximum(jnp.sum(e, axis=-1, keepdims=True), TINY)


def _softmax_parts(s, mask):
    s = jnp.where(mask, s, NEG)
    e = jnp.where(mask, jnp.exp(s - jnp.max(s, axis=-1, keepdims=True)), 0.0)
    return e, 1.0 / jnp.maximum(jnp.sum(e, axis=-1, keepdims=True), TINY)


def _params(*sem):
    return pltpu.CompilerParams(dimension_semantics=sem, vmem_limit_bytes=VMEM_LIMIT)


def _const_spec(shape):
    nd = len(shape)
    return pl.BlockSpec(shape, lambda *_: (0,) * nd, pipeline_mode=pl.Buffered(1))


def _inproj_kernel(x_ref, g_ref, w_ref, q_ref, kvc_ref, kvs_ref, kvw_ref, gate_ref, a_ref,
                   ksa_ref, vs_ref, kw_ref, vw_ref, *kvt_refs, tm, seq_len):
    h = _rms(x_ref[...], g_ref[...])
    z = _dot(h.astype(BF16), w_ref[...])
    off = ATTN_WIDTH
    for br, kvt_ref in enumerate(kvt_refs):
        kvt_ref[0] = z[:, off + br * KV_COLS:off + (br + 1) * KV_COLS].T
    q_ref[...] = (z[:, :off] * SCALE).astype(BF16)
    kvc_ref[...] = z[:, off:off + KV_COLS]
    kvs = z[:, off + KV_COLS:off + 2 * KV_COLS]
    kvw = z[:, off + 2 * KV_COLS:off + 3 * KV_COLS]
    kvs_ref[...] = kvs
    kvw_ref[...] = kvw
    off += 3 * KV_COLS
    gate_ref[...] = jax.nn.sigmoid(z[:, off:off + GATE_PAD])
    off += GATE_PAD
    a_ref[...] = z[:, off:off + CONV_CH] * jax.nn.sigmoid(z[:, off + CONV_CH:])
    pos = (pl.program_id(0) * tm + lax.broadcasted_iota(jnp.int32, (tm, HEAD_DIM), 0)) % seq_len
    lane = lax.broadcasted_iota(jnp.int32, (tm, HEAD_DIM), 1)
    onehot = (pos // SEL_BLOCK == lane).astype(BF16)
    ones_col = (lane == 0).astype(BF16)
    for g in range(KV_HEADS):
        k0, v0 = g * HEAD_DIM, (KV_HEADS + g) * HEAD_DIM
        ksa_ref[g] = jnp.concatenate([kvs[:, k0:k0 + HEAD_DIM].astype(BF16), onehot], axis=1)
        vs_ref[g] = jnp.concatenate([kvs[:, v0:v0 + HEAD_DIM].astype(BF16), ones_col], axis=1)
        kw_ref[g] = kvw[:, k0:k0 + HEAD_DIM].astype(BF16)
        vw_ref[g] = jnp.concatenate([kvw[:, v0:v0 + HEAD_DIM].astype(BF16), ones_col], axis=1)


def _inproj(x, g, w, *, tm, seq_len, transposed_kv):
    m = x.shape[0]
    per_row = seq_len // tm if transposed_kv else 1
    row = lambda c: pl.BlockSpec((tm, c), lambda i: (i, 0))
    grp = lambda c: pl.BlockSpec((KV_HEADS, tm, c), lambda i: (0, i, 0))
    tr = pl.BlockSpec((1, KV_COLS, tm), lambda i: (i // per_row, 0, i % per_row))
    kvt = jax.ShapeDtypeStruct((m // seq_len, KV_COLS, seq_len), F32)
    n_t = N_BRANCH if transposed_kv else 0
    out_shape = (
        jax.ShapeDtypeStruct((m, ATTN_WIDTH), BF16),
        jax.ShapeDtypeStruct((m, KV_COLS), F32), jax.ShapeDtypeStruct((m, KV_COLS), F32),
        jax.ShapeDtypeStruct((m, KV_COLS), F32),
        jax.ShapeDtypeStruct((m, GATE_PAD), F32), jax.ShapeDtypeStruct((m, CONV_CH), F32),
        jax.ShapeDtypeStruct((KV_HEADS, m, 2 * HEAD_DIM), BF16),
        jax.ShapeDtypeStruct((KV_HEADS, m, 2 * HEAD_DIM), BF16),
        jax.ShapeDtypeStruct((KV_HEADS, m, HEAD_DIM), BF16),
        jax.ShapeDtypeStruct((KV_HEADS, m, 2 * HEAD_DIM), BF16),
    ) + (kvt,) * n_t
    out_specs = (row(ATTN_WIDTH), row(KV_COLS), row(KV_COLS), row(KV_COLS), row(GATE_PAD), row(CONV_CH),
                 grp(2 * HEAD_DIM), grp(2 * HEAD_DIM), grp(HEAD_DIM), grp(2 * HEAD_DIM)) + (tr,) * n_t
    return pl.pallas_call(
        functools.partial(_inproj_kernel, tm=tm, seq_len=seq_len),
        grid=(m // tm,),
        in_specs=[row(D_MODEL), _const_spec((1, D_MODEL)), _const_spec((D_MODEL, IN_COLS))],
        out_specs=out_specs, out_shape=out_shape,
        compiler_params=_params("parallel"), name="inproj",
    )(x, g, w)


def _conv_ln_silu(y, b, lg, lb):
    y = y + b
    mu = jnp.mean(y, axis=-1, keepdims=True)
    var = jnp.mean(jnp.square(y - mu), axis=-1, keepdims=True)
    return jax.nn.silu((y - mu) * lax.rsqrt(var + EPS) * lg + lb)


def _conv_kernel(prev_ref, a_ref, w_ref, b_ref, lg_ref, lb_ref, o_ref, ext_ref, sh_ref, *, tc):
    first = pl.program_id(1) == 0
    ext_ref[0:CONV_HALO, :] = jnp.where(first, 0.0, prev_ref[0])
    ext_ref[CONV_HALO:CONV_HALO + tc, :] = a_ref[0]
    lead = CONV_HALO - (CONV_WIDTH - 1)
    span = sh_ref.shape[1]
    for r in range(1, SUBLANE):
        sh_ref[r - 1] = ext_ref[r:r + span, :]

    for r0 in range(0, tc, CONV_ROWS):
        acc = jnp.zeros((CONV_ROWS, CONV_CH), F32)
        for k in range(CONV_WIDTH):
            r = (lead + k) % SUBLANE
            i0 = r0 + lead + k - r
            rows = ext_ref[i0:i0 + CONV_ROWS, :] if r == 0 else sh_ref[r - 1, i0:i0 + CONV_ROWS, :]
            acc = acc + jnp.concatenate([w_ref[k]] * (CONV_ROWS // SUBLANE), axis=0) * rows
        o_ref[0, r0:r0 + CONV_ROWS, :] = _conv_ln_silu(acc, b_ref[...], lg_ref[...], lb_ref[...]).astype(BF16)


def _conv_prompt(a, w, b, lg, lb, *, tc):
    bsz, t, _ = a.shape
    hb = tc // CONV_HALO
    return pl.pallas_call(
        functools.partial(_conv_kernel, tc=tc),
        grid=(bsz, t // tc),
        in_specs=[pl.BlockSpec((1, CONV_HALO, CONV_CH), lambda bi, i: (bi, jnp.maximum(i * hb - 1, 0), 0)),
                  pl.BlockSpec((1, tc, CONV_CH), lambda bi, i: (bi, i, 0)),
                  _const_spec((CONV_WIDTH, SUBLANE, CONV_CH)), _const_spec((1, CONV_CH)),
                  _const_spec((1, CONV_CH)), _const_spec((1, CONV_CH))],
        out_specs=pl.BlockSpec((1, tc, CONV_CH), lambda bi, i: (bi, i, 0)),
        out_shape=jax.ShapeDtypeStruct((bsz, t, CONV_CH), BF16),
        scratch_shapes=[pltpu.VMEM((CONV_HALO + tc, CONV_CH), F32),
                        pltpu.VMEM((SUBLANE - 1, CONV_HALO + tc - SUBLANE, CONV_CH), F32)],
        compiler_params=_params("parallel", "parallel"), name="conv_prompt",
    )(a, a, w, b, lg, lb)


def _compress_rows(h0, h1_next, pos_term, w2):
    hid = pos_term + h0 + h1_next
    return _dot(jax.nn.gelu(hid).astype(BF16), w2)


def _pos_term(posx_ref, wc_ref):
    hp = _dot(posx_ref[...], wc_ref[...])
    return hp[0:1, :KV_COLS] + hp[1:2, KV_COLS:]


def _compress_kernel(x_ref, wc_ref, posx_ref, w2_ref, kc_ref, vc_ref, pos_ref):
    hh = _dot(x_ref[0].astype(BF16), wc_ref[...])
    h1 = hh[:, KV_COLS:]
    h1_next = jnp.concatenate([h1[1:], jnp.zeros((1, KV_COLS), F32)], axis=0)
    pos_term = _pos_term(posx_ref, wc_ref)
    pos_ref[...] = jnp.broadcast_to(pos_term, pos_ref.shape)
    cmp = _compress_rows(hh[:, :KV_COLS], h1_next, pos_term, w2_ref[...])
    for g in range(KV_HEADS):
        kc_ref[0, g] = cmp[:, g * HEAD_DIM:(g + 1) * HEAD_DIM].astype(BF16)
        vc_ref[0, g] = cmp[:, (KV_HEADS + g) * HEAD_DIM:(KV_HEADS + g + 1) * HEAD_DIM].astype(BF16)


def _compress_prompt(kvc_chunks, wc, posx, w2):
    bsz, n_ch, _ = kvc_chunks.shape
    tok = jax.ShapeDtypeStruct((bsz, KV_HEADS, n_ch, HEAD_DIM), BF16)
    tok_spec = pl.BlockSpec((1, KV_HEADS, n_ch, HEAD_DIM), lambda bi: (bi, 0, 0, 0))
    return pl.pallas_call(
        _compress_kernel,
        grid=(bsz,),
        in_specs=[pl.BlockSpec((1, n_ch, CHUNK_COLS), lambda bi: (bi, 0, 0)),
                  _const_spec(wc.shape), _const_spec(posx.shape), _const_spec(w2.shape)],
        out_specs=(tok_spec, tok_spec, pl.BlockSpec((8, KV_COLS), lambda bi: (0, 0))),
        out_shape=(tok, tok, jax.ShapeDtypeStruct((8, KV_COLS), F32)),
        compiler_params=_params("arbitrary"), name="compress_prompt",
    )(kvc_chunks, wc, posx, w2)


def _block_scores(imp, ssel, t_col, ns):
    s = _dot_exact_lhs(imp, ssel)
    j = lax.broadcasted_iota(jnp.int32, s.shape, 1)
    cur = t_col // SEL_BLOCK
    valid = j * SEL_BLOCK <= t_col
    forced = valid & ((j == 0) | (j == cur) | (j == cur - 1))
    s = jnp.where(forced, BIG, jnp.where(valid, s, -BIG))
    return jnp.where(j < ns, s, -3.0 * BIG)


def _nsa_prompt_kernel(q_ref, gate_ref, ksa_ref, vs_ref, kw_ref, vw_ref, kc_ref, vc_ref, ssel_ref, o_ref,
                       qa_ref, s_ref, p_ref, part_ref, m_ref, acc_ref, bias_ref, *, tq, ns):
    i = pl.program_id(1)
    q0 = i * tq
    t_col = q0 + lax.broadcasted_iota(jnp.int32, (tq, 1), 0)
    t4 = jnp.concatenate([t_col] * GROUP, axis=0)
    gates = gate_ref[...]
    n_blk = kc_ref.shape[2]
    rows = GROUP * tq
    end = lax.broadcasted_iota(jnp.int32, (1, n_blk), 1) * CMP_STRIDE + (CMP_LEN - 1)
    cmask = end <= t4

    sub = lax.broadcasted_iota(jnp.int32, (SUBLANE, tq), 0)
    o_cs, q4s = [], []
    for g in range(KV_HEADS):
        q4 = jnp.concatenate(
            [q_ref[:, (GROUP * g + h) * HEAD_DIM:(GROUP * g + h + 1) * HEAD_DIM] for h in range(GROUP)], axis=0)
        q4s.append(q4)

        e_c, r_c = _softmax_parts(_dot_nt(q4, kc_ref[0, g]), cmask)
        o_cs.append(_dot(e_c.astype(BF16), vc_ref[0, g]) * r_c)
        p_c = e_c * r_c
        imp = p_c[0:tq] + p_c[tq:2 * tq] + p_c[2 * tq:3 * tq] + p_c[3 * tq:4 * tq]

        s_t = _block_scores(imp, ssel_ref[...], t_col, ns).T
        blocks = [s_t[v * SUBLANE:(v + 1) * SUBLANE] for v in range(ns // SUBLANE)]
        ranks = [jnp.zeros((SUBLANE, tq), F32) for _ in blocks]
        for i2 in range(ns):
            row = s_t[i2:i2 + 1, :]
            for v, blk in enumerate(blocks):
                if v > i2 // SUBLANE:
                    beats = row >= blk
                elif v < i2 // SUBLANE:
                    beats = row > blk
                else:
                    beats = (row > blk) | ((row == blk) & (sub > i2 % SUBLANE))
                ranks[v] = ranks[v] + jnp.where(beats, 1.0, 0.0)
        sel_t = jnp.concatenate([jnp.where(r < N_SELECT, 0.0, NEG) for r in ranks]
                                + [jnp.full((LANE - ns, tq), NEG, F32)], axis=0)
        sel = sel_t.T
        selbias = jnp.concatenate([sel[:, :HEAD_DIM].astype(BF16)] * GROUP, axis=0)
        qa_ref[g] = jnp.concatenate([q4, selbias], axis=1)

    def set_bias(b, k0, width, key_ok):
        t = t_col
        key = k0 + lax.broadcasted_iota(jnp.int32, (1, width), 1)
        bias_ref[b, :, 0:width] = jnp.where(key_ok(t, key), 0.0, NEG)

    def scores(slot, q, k_ref, g, k0, width, bias):
        s_ref[slot, :, 0:width] = _dot_nt(q, k_ref[g, pl.ds(k0, width), :])

        for r0 in range(0, rows, ATT_ROWS):
            sb = s_ref[slot, r0:r0 + ATT_ROWS, 0:width]
            if bias is not None:
                sb = sb + bias_ref[bias, r0 % tq:r0 % tq + ATT_ROWS, 0:width]
                s_ref[slot, r0:r0 + ATT_ROWS, 0:width] = sb
            mx = sb[:, 0:LANE]
            for j in range(1, width // LANE):
                mx = jnp.maximum(mx, sb[:, j * LANE:(j + 1) * LANE])
            part_ref[slot, r0:r0 + ATT_ROWS, :] = mx
        return jnp.max(part_ref[slot], axis=-1, keepdims=True)

    def weights(slot, width):
        for r0 in range(0, rows, ATT_ROWS):
            m = m_ref[slot, r0:r0 + ATT_ROWS, :]
            sb = s_ref[slot, r0:r0 + ATT_ROWS, 0:width]
            p_ref[slot, r0:r0 + ATT_ROWS, 0:width] = jnp.exp(sb - jnp.concatenate([m] * (width // LANE), axis=1)).astype(BF16)

    w0 = pl.multiple_of(jnp.maximum(q0 - WINDOW, 0), Q_TILE)
    set_bias(1, w0, WIN_KEYS, lambda t, key: (t - key >= 0) & (t - key < WINDOW))

    def window(g):
        slot = KV_HEADS + g
        m_ref[slot] = jnp.broadcast_to(scores(slot, q4s[g], kw_ref, g, w0, WIN_KEYS, 1), (rows, LANE))
        weights(slot, WIN_KEYS)
        acc_w = _dot(p_ref[slot, :, 0:WIN_KEYS], vw_ref[g, pl.ds(w0, WIN_KEYS), :])
        return acc_w[:, :HEAD_DIM] * (1.0 / acc_w[:, HEAD_DIM:HEAD_DIM + 1])

    o_w = [window(0)]

    for g in range(KV_HEADS):
        m_ref[g] = jnp.full((rows, LANE), NEG, F32)
        acc_ref[g] = jnp.zeros((rows, LANE), F32)

    def chunk(k0, bias):
        blk_max = [scores(g, qa_ref[g], ksa_ref, g, k0, SEL_CHUNK, bias) for g in range(KV_HEADS)]
        for g in range(KV_HEADS):
            m_old = m_ref[g]
            m_new = jnp.maximum(m_old, blk_max[g])
            m_ref[g] = m_new
            weights(g, SEL_CHUNK)
            acc_ref[g] = (jnp.exp(m_old - m_new) * acc_ref[g]
                          + _dot(p_ref[g, :, 0:SEL_CHUNK], vs_ref[g, pl.ds(k0, SEL_CHUNK), :]))

    n_full = q0 // SEL_CHUNK

    def full_chunk(c, carry):
        chunk(pl.multiple_of(c * SEL_CHUNK, SEL_CHUNK), None)
        return carry

    lax.fori_loop(0, n_full, full_chunk, 0)

    tail0 = pl.multiple_of(jnp.maximum(q0 + tq - SEL_CHUNK, 0), Q_TILE)
    set_bias(0, tail0, SEL_CHUNK, lambda t, key: (key <= t) & (key >= n_full * SEL_CHUNK))
    chunk(tail0, 0)
    o_w.append(window(1))

    for g in range(KV_HEADS):
        o_c = o_cs[g]
        acc_s = acc_ref[g]
        o_s = acc_s[:, :HEAD_DIM] * (1.0 / acc_s[:, HEAD_DIM:HEAD_DIM + 1])
        for h in range(GROUP):
            hh = GROUP * g + h
            r = slice(h * tq, (h + 1) * tq)
            c = N_BRANCH * hh
            o = gates[:, c:c + 1] * o_c[r] + gates[:, c + 1:c + 2] * o_s[r] + gates[:, c + 2:c + 3] * o_w[g][r]
            o_ref[:, hh * HEAD_DIM:(hh + 1) * HEAD_DIM] = o.astype(BF16)


def _nsa_prompt(q, gates, ksa, vs, kw, vw, kc, vc, ssel, *, bsz, seq_len):
    tq = Q_TILE
    nt = seq_len // tq
    ns = seq_len // SEL_BLOCK
    n_blk = kc.shape[2]
    rows = GROUP * tq
    row = lambda c: pl.BlockSpec((tq, c), lambda bi, i: (bi * nt + i, 0))
    seq = lambda c: pl.BlockSpec((KV_HEADS, seq_len, c), lambda bi, i: (0, bi, 0))
    tok = pl.BlockSpec((1, KV_HEADS, n_blk, HEAD_DIM), lambda bi, i: (bi, 0, 0, 0))
    return pl.pallas_call(
        functools.partial(_nsa_prompt_kernel, tq=tq, ns=ns),
        grid=(bsz, nt),
        in_specs=[row(ATTN_WIDTH), row(GATE_PAD), seq(2 * HEAD_DIM), seq(2 * HEAD_DIM), seq(HEAD_DIM),
                  seq(2 * HEAD_DIM), tok, tok, _const_spec(ssel.shape)],
        out_specs=row(ATTN_WIDTH),
        out_shape=jax.ShapeDtypeStruct((bsz * seq_len, ATTN_WIDTH), BF16),
        scratch_shapes=[pltpu.VMEM((KV_HEADS, rows, 2 * HEAD_DIM), BF16),
                        pltpu.VMEM((2 * KV_HEADS, rows, WIN_KEYS), F32),
                        pltpu.VMEM((2 * KV_HEADS, rows, WIN_KEYS), BF16), pltpu.VMEM((2 * KV_HEADS, rows, LANE), F32),
                        pltpu.VMEM((2 * KV_HEADS, rows, LANE), F32), pltpu.VMEM((KV_HEADS, rows, LANE), F32),
                        pltpu.VMEM((2, tq, WIN_KEYS), F32)],
        compiler_params=_params("parallel", "parallel"), name="nsa_prompt",
    )(q, gates, ksa, vs, kw, vw, kc, vc, ssel)


def _swiglu_residual(x1, gf, wg_ref, wu_ref, wd_ref):
    h = _rms(x1, gf).astype(BF16)
    act = jax.nn.silu(_dot(h, wg_ref[...])) * _dot(h, wu_ref[...])
    return x1 + _dot(act.astype(BF16), wd_ref[...])


def _ffn0_kernel(x_ref, a_ref, c_ref, woa_ref, woc_ref, gf_ref, wg_ref, wu_ref, wd_ref, o_ref):
    x1 = x_ref[...] + (_dot(a_ref[...], woa_ref[...]) + _dot(c_ref[...], woc_ref[...]))
    o_ref[...] = _swiglu_residual(x1, gf_ref[...], wg_ref, wu_ref, wd_ref)


def _ffn0(x, a, c, woa, woc, gf, wg, wu, wd, *, tm):
    m = x.shape[0]
    row = lambda cols: pl.BlockSpec((tm, cols), lambda i: (i, 0))
    return pl.pallas_call(
        _ffn0_kernel,
        grid=(m // tm,),
        in_specs=[row(D_MODEL), row(ATTN_WIDTH), row(CONV_CH), _const_spec(woa.shape), _const_spec(woc.shape),
                  _const_spec(gf.shape), _const_spec(wg.shape), _const_spec(wu.shape), _const_spec(wd.shape)],
        out_specs=row(D_MODEL), out_shape=jax.ShapeDtypeStruct((m, D_MODEL), F32),
        compiler_params=_params("parallel"), name="outproj_ffn",
    )(x, a, c, woa, woc, gf, wg, wu, wd)


def _pool_mix(x, h, win_sums, cnts, pw_ref, ps):
    ys = []
    for g in range(len(POOL_WINDOWS)):
        z = win_sums[g] / cnts[g] - h[:, g * POOL_GROUP:(g + 1) * POOL_GROUP]
        ys.append(_dot(z.astype(BF16), pw_ref[g]))
    return x + jnp.concatenate(ys, axis=1) * ps


def _layer1_prompt_kernel(xprev_ref, x_ref, gm_ref, pw_ref, ps_ref, gf_ref, wg_ref, wu_ref, wd_ref, gfin_ref,
                          y_ref, hst_ref, *, tm):
    i = pl.program_id(1)
    x = x_ref[0]
    h = _rms(x, gm_ref[...])
    hprev = jnp.where(i == 0, 0.0, _rms(xprev_ref[0], gm_ref[...]))
    hst_ref[0] = h[tm - POOL_HALO:, :]
    pos = i * tm + lax.broadcasted_iota(jnp.int32, (tm, 1), 0)
    sums, cnts = [], []
    for g, w in enumerate(POOL_WINDOWS):
        e = jnp.concatenate([hprev[:, g * POOL_GROUP:(g + 1) * POOL_GROUP],
                             h[:, g * POOL_GROUP:(g + 1) * POOL_GROUP]], axis=0)
        span = 1
        while span < w:
            e = e[span:] + e[:-span]
            span *= 2
        first = POOL_HALO - (w - 1)
        sums.append(e[first:first + tm])
        cnts.append(jnp.minimum(w, pos + 1).astype(F32))
    x1 = _pool_mix(x, h, sums, cnts, pw_ref, ps_ref[...])
    x2 = _swiglu_residual(x1, gf_ref[...], wg_ref, wu_ref, wd_ref)
    y_ref[0] = _rms(x2, gfin_ref[...])


def _layer1_prompt(x, gm, pw, ps, gf, wg, wu, wd, gfin, *, tm):
    bsz, t, _ = x.shape
    hb = tm // POOL_HALO
    return pl.pallas_call(
        functools.partial(_layer1_prompt_kernel, tm=tm),
        grid=(bsz, t // tm),
        in_specs=[pl.BlockSpec((1, POOL_HALO, D_MODEL), lambda bi, i: (bi, jnp.maximum(i * hb - 1, 0), 0)),
                  pl.BlockSpec((1, tm, D_MODEL), lambda bi, i: (bi, i, 0)),
                  _const_spec(gm.shape), _const_spec(pw.shape), _const_spec(ps.shape), _const_spec(gf.shape),
                  _const_spec(wg.shape), _const_spec(wu.shape), _const_spec(wd.shape), _const_spec(gfin.shape)],
        out_specs=(pl.BlockSpec((1, tm, D_MODEL), lambda bi, i: (bi, i, 0)),
                   pl.BlockSpec((1, POOL_HALO, D_MODEL), lambda bi, i: (bi, 0, 0))),
        out_shape=(jax.ShapeDtypeStruct((bsz, t, D_MODEL), F32),
                   jax.ShapeDtypeStruct((bsz, POOL_HALO, D_MODEL), F32)),
        compiler_params=_params("parallel", "arbitrary"), name="layer1_prompt",
    )(x, x, gm, pw, ps, gf, wg, wu, wd, gfin)


def _layer1_sample_kernel(x_ref, hist_ref, gm_ref, pw_ref, ps_ref, gf_ref, wg_ref, wu_ref, wd_ref, gfin_ref,
                          y_ref, h_ref, *, first_pos):
    x = x_ref[...]
    h = _rms(x, gm_ref[...])
    h_ref[...] = h
    sums, cnts = [], []
    for g, w in enumerate(POOL_WINDOWS):
        c = slice(g * POOL_GROUP, (g + 1) * POOL_GROUP)
        s = h[:, c]
        for k in range(1, w):
            s = s + hist_ref[POOL_BUF - k][:, c]
        sums.append(s)
        cnts.append(float(min(w, first_pos + 1)))
    x1 = _pool_mix(x, h, sums, cnts, pw_ref, ps_ref[...])
    x2 = _swiglu_residual(x1, gf_ref[...], wg_ref, wu_ref, wd_ref)
    y_ref[...] = _rms(x2, gfin_ref[...])


def _layer1_sample(x, hist, gm, pw, ps, gf, wg, wu, wd, gfin, *, first_pos):
    m = x.shape[0]
    args = (x, hist, gm, pw, ps, gf, wg, wu, wd, gfin)
    out = jax.ShapeDtypeStruct((m, D_MODEL), F32)
    return pl.pallas_call(
        functools.partial(_layer1_sample_kernel, first_pos=first_pos),
        grid=(1,),
        in_specs=[_const_spec(a.shape) for a in args],
        out_specs=(_const_spec((m, D_MODEL)), _const_spec((m, D_MODEL))), out_shape=(out, out),
        compiler_params=_params("arbitrary"), name="layer1_sample",
    )(*args)


def _page_copy(cache_ref, xt_ref, sem, phys, p):
    return pltpu.make_async_copy(cache_ref.at[phys], xt_ref.at[p], sem)


def _cmp_sample_kernel(pt_ref, q_ref, cache_ref, wkv_ref, pos_ref, w2_ref, ssel_ref, oc_ref, idx_ref,
                       xt_ref, xl_ref, hbuf_ref, sem, *, n_pages, past, ns, nsp):
    b = pl.program_id(0)
    n_ch = n_pages * (PAGE_SIZE // CMP_STRIDE)

    def gather(sample):
        def start(p, carry):
            _page_copy(cache_ref, xt_ref, sem, pt_ref[sample * n_pages + p], p).start()
            return carry

        lax.fori_loop(0, n_pages, start, 0)

    def wait(p, carry):
        _page_copy(cache_ref, xt_ref, sem, 0, p).wait()
        return carry

    per_page = PAGE_SIZE // CMP_STRIDE
    r_out = lax.broadcasted_iota(jnp.int32, (PAGE_SIZE, PAGE_SIZE), 0)
    r_in = lax.broadcasted_iota(jnp.int32, (PAGE_SIZE, PAGE_SIZE), 1)
    perm = (r_in == (r_out % per_page) * CMP_STRIDE + r_out // per_page).astype(BF16)

    batch = 8

    def to_rows(i, carry):
        p0 = pl.multiple_of(i * batch, batch)
        pages = xt_ref[pl.ds(p0, batch)].reshape(batch * KV_COLS, PAGE_SIZE)
        xp = _dot_nt(perm, pages.astype(BF16))
        for k in range(batch):
            c0 = pl.multiple_of((p0 + k) * per_page, per_page)
            for l in range(CMP_STRIDE):
                xl_ref[l, pl.ds(c0, per_page), :] = xp[l * per_page:(l + 1) * per_page, k * KV_COLS:(k + 1) * KV_COLS]
        return carry

    @pl.when(b == 0)
    def _():
        gather(0)

    lax.fori_loop(0, n_pages, wait, 0)
    lax.fori_loop(0, n_pages // batch, to_rows, 0)

    @pl.when(b + 1 < pl.num_programs(0))
    def _():
        gather(b + 1)

    rows = min(256, n_ch)
    half = KV_COLS // 2
    for r0 in range(0, n_ch, rows):
        for kv in range(2):
            chunk_rows = jnp.concatenate(
                [xl_ref[l, r0:r0 + rows, kv * half:(kv + 1) * half].astype(BF16) for l in range(CMP_STRIDE)], axis=1)
            hbuf_ref[r0:r0 + rows, kv * KV_COLS:(kv + 1) * KV_COLS] = _dot(chunk_rows, wkv_ref[kv])
    hbuf_ref[n_ch:n_ch + 8, :] = jnp.zeros((8, 2 * KV_COLS), F32)
    first = lambda r: jnp.concatenate([hbuf_ref[r, 0:half], hbuf_ref[r, KV_COLS:KV_COLS + half]], axis=1)
    second = lambda r: jnp.concatenate([hbuf_ref[r, half:KV_COLS], hbuf_ref[r, KV_COLS + half:]], axis=1)
    cmp = _compress_rows(first(pl.ds(0, n_ch)), second(pl.ds(1, n_ch)), pos_ref[0:1, :], w2_ref[...])

    q8 = q_ref[0]
    head = lax.broadcasted_iota(jnp.int32, (N_HEADS, 1), 0)
    end = lax.broadcasted_iota(jnp.int32, (1, n_ch), 1) * CMP_STRIDE + (CMP_LEN - 1)
    cmask = jnp.broadcast_to(end <= past, (N_HEADS, n_ch))
    o_c = jnp.zeros((N_HEADS, HEAD_DIM), F32)
    jl = lax.broadcasted_iota(jnp.int32, (1, nsp), 1)
    ii = lax.broadcasted_iota(jnp.int32, (nsp, nsp), 0)
    jj = lax.broadcasted_iota(jnp.int32, (nsp, nsp), 1)
    cur = past // SEL_BLOCK
    t_col = jnp.full((N_HEADS, 1), past, jnp.int32)
    for g in range(KV_HEADS):
        in_group = head // GROUP == g
        kc = cmp[:, g * HEAD_DIM:(g + 1) * HEAD_DIM].astype(BF16)
        vc = cmp[:, (KV_HEADS + g) * HEAD_DIM:(KV_HEADS + g + 1) * HEAD_DIM].astype(BF16)
        sc = _dot_nt(q8, kc)
        p_c = _softmax_rows(jnp.where(cmask, sc, NEG), cmask)
        o_c = jnp.where(in_group, _dot(p_c.astype(BF16), vc), o_c)
        imp = jnp.sum(jnp.where(in_group, p_c, 0.0), axis=0, keepdims=True)
        s_b = _block_scores(jnp.broadcast_to(imp, (N_HEADS, n_ch)), ssel_ref[...], t_col, ns)
        s_b = jnp.broadcast_to(s_b[0:1], (nsp, nsp))
        s_t = s_b.T
        beats = (s_t > s_b) | ((s_t == s_b) & (ii < jj))
        rank = jnp.sum(jnp.where(beats, 1.0, 0.0), axis=0, keepdims=True)
        sel = (rank < N_SELECT) & (jl != cur)
        sel_f = jnp.where(sel, 1.0, 0.0)
        before = _dot(jnp.broadcast_to(sel_f, (N_HEADS, nsp)).astype(BF16),
                      jnp.where(ii < jj, 1.0, 0.0).astype(BF16))[0:1]
        slot = lax.broadcasted_iota(jnp.int32, (N_SELECT, nsp), 0)
        pick = jnp.broadcast_to(sel, (N_SELECT, nsp)) & (jnp.broadcast_to(before, (N_SELECT, nsp)) == slot.astype(F32))
        blk = jnp.sum(jnp.where(pick, lax.broadcasted_iota(jnp.int32, (N_SELECT, nsp), 1), 0), axis=1, keepdims=True)
        idx_ref[0, g] = blk
    oc_ref[0] = o_c


def _cmp_sample(page_table, q8, cache_pages, wkv, pos_term, w2, ssel, *, past, ns):
    bsz, n_pages = page_table.shape
    n_ch = n_pages * (PAGE_SIZE // CMP_STRIDE)
    nsp = ssel.shape[1]
    grid_spec = pltpu.PrefetchScalarGridSpec(
        num_scalar_prefetch=1, grid=(bsz,),
        in_specs=[pl.BlockSpec((1, N_HEADS, HEAD_DIM), lambda b, pt: (b, 0, 0)),
                  pl.BlockSpec(memory_space=pl.ANY),
                  _const_spec(wkv.shape), _const_spec(pos_term.shape), _const_spec(w2.shape), _const_spec(ssel.shape)],
        out_specs=(pl.BlockSpec((1, N_HEADS, HEAD_DIM), lambda b, pt: (b, 0, 0)),
                   pl.BlockSpec((1, KV_HEADS, N_SELECT, 1), lambda b, pt: (b, 0, 0, 0))),
        scratch_shapes=[pltpu.VMEM((n_pages, KV_COLS, PAGE_SIZE), F32), pltpu.VMEM((CMP_STRIDE, n_ch, KV_COLS), F32),
                        pltpu.VMEM((n_ch + 8, 2 * KV_COLS), F32), pltpu.SemaphoreType.DMA(())],
    )
    return pl.pallas_call(
        functools.partial(_cmp_sample_kernel, n_pages=n_pages, past=past, ns=ns, nsp=nsp),
        grid_spec=grid_spec,
        out_shape=(jax.ShapeDtypeStruct((bsz, N_HEADS, HEAD_DIM), F32),
                   jax.ShapeDtypeStruct((bsz, KV_HEADS, N_SELECT, 1), jnp.int32)),
        compiler_params=_params("arbitrary"), name="cmp_sample",
    )(page_table.reshape(-1), q8, cache_pages, wkv, pos_term, w2, ssel)


def _slab_copy(cache_ref, dst_ref, sem, src, buf, g, k):
    return pltpu.make_async_copy(cache_ref.at[src], dst_ref.at[buf, g, :, pl.ds(k * PAGE_SIZE, PAGE_SIZE)],
                                 sem.at[buf])


def _mix_sample_kernel(idx_ref, pt_ref, q_ref, gate_ref, oc_ref, kvs_ref, kvw_ref, win_ref, cst_ref, a_ref,
                       cw_ref, cb_ref, lg_ref, lb_ref, cache_ref, att_ref, conv_ref, kt_ref, vt_ref, sem,
                       *, n_pages, n_gather, past, w_buf):
    b = pl.program_id(0)
    per_page = PAGE_SIZE // SEL_BLOCK
    buf = b % 2

    def block_of(sample, g, k):
        return idx_ref[(sample * KV_HEADS + g) * N_SELECT + k]

    def gather(sample, into):
        for g in range(KV_HEADS):
            for k in range(n_gather):
                blk = block_of(sample, g, k)
                slab = pt_ref[sample * n_pages + blk // per_page] * (2 * KV_HEADS) + g
                _slab_copy(cache_ref, kt_ref, sem, slab, into, g, k).start()
                _slab_copy(cache_ref, vt_ref, sem, slab + KV_HEADS, into, g, k).start()

    @pl.when(b == 0)
    def _():
        gather(0, 0)

    @pl.when(b + 1 < pl.num_programs(0))
    def _():
        gather(b + 1, 1 - buf)

    y = (jnp.sum(cw_ref[0:CONV_WIDTH - 1, :] * cst_ref[0], axis=0, keepdims=True)
         + cw_ref[CONV_WIDTH - 1:CONV_WIDTH, :] * a_ref[0])
    conv_ref[0] = _conv_ln_silu(y, cb_ref[...], lg_ref[...], lb_ref[...])

    def wait(s, carry):
        _slab_copy(cache_ref, kt_ref, sem, 0, buf, 0, 0).wait()
        return carry

    lax.fori_loop(0, 2 * KV_HEADS * n_gather, wait, 0)
    halves = [block_of(b, g, k) % per_page for g in range(KV_HEADS) for k in range(n_gather)]

    q8 = q_ref[0]
    q8f = q8.astype(F32)
    head = lax.broadcasted_iota(jnp.int32, (N_HEADS, 1), 0)
    gates = gate_ref[0]
    kvs_new, kvw_new = kvs_ref[0], kvw_ref[0]
    jw = lax.broadcasted_iota(jnp.int32, (1, w_buf), 1)
    wdiff = w_buf - jw
    wmask = jnp.broadcast_to((wdiff < WINDOW) & (past - wdiff >= 0), (N_HEADS, w_buf))
    half_of_lane = lax.broadcasted_iota(jnp.int32, (1, PAGE_SIZE), 1) // SEL_BLOCK
    out = jnp.zeros((N_HEADS, HEAD_DIM), F32)

    def with_new_row(kt, vt, mask, k_new, v_new):
        s = jnp.where(mask, _dot(q8, kt.astype(BF16)), NEG)
        s_new = jnp.sum(q8f * k_new.astype(BF16).astype(F32), axis=-1, keepdims=True)
        m = jnp.maximum(jnp.max(s, axis=-1, keepdims=True), s_new)
        e = jnp.where(mask, jnp.exp(s - m), 0.0)
        e_new = jnp.exp(s_new - m)
        num = _dot_nt(e.astype(BF16), vt.astype(BF16)) + e_new * v_new
        return num / (jnp.sum(e, axis=-1, keepdims=True) + e_new)

    for g in range(KV_HEADS):
        kc, vc = slice(g * HEAD_DIM, (g + 1) * HEAD_DIM), slice((KV_HEADS + g) * HEAD_DIM, (KV_HEADS + g + 1) * HEAD_DIM)
        smask = jnp.concatenate([half_of_lane == halves[g * n_gather + k] for k in range(n_gather)], axis=1)
        smask = jnp.broadcast_to(smask, (N_HEADS, n_gather * PAGE_SIZE))
        o_s = with_new_row(kt_ref[buf, g], vt_ref[buf, g], smask, kvs_new[:, kc], kvs_new[:, vc])
        o_w = with_new_row(win_ref[0, g], win_ref[0, KV_HEADS + g], wmask, kvw_new[:, kc], kvw_new[:, vc])
        mixed = gates[:, 0:1] * oc_ref[0] + gates[:, 1:2] * o_s + gates[:, 2:3] * o_w
        out = jnp.where(head // GROUP == g, mixed, out)
    att_ref[0] = out


def _mix_sample(idx, page_table, q8, gates8, o_c, kvs, kvw, win, conv_state, a, cw, cb, lg, lb, cache_slabs,
                *, past, n_gather):
    bsz, n_pages = page_table.shape
    w_buf = win.shape[-1]
    one = lambda *shape: pl.BlockSpec((1,) + shape, lambda b, *_: (b,) + (0,) * len(shape))
    const = lambda shape: pl.BlockSpec(shape, lambda b, *_: (0,) * len(shape))
    gathered = pltpu.VMEM((2, KV_HEADS, HEAD_DIM, n_gather * PAGE_SIZE), F32)
    grid_spec = pltpu.PrefetchScalarGridSpec(
        num_scalar_prefetch=2, grid=(bsz,),
        in_specs=[one(N_HEADS, HEAD_DIM), one(N_HEADS, N_BRANCH), one(N_HEADS, HEAD_DIM), one(1, KV_COLS),
                  one(1, KV_COLS), one(2 * KV_HEADS, HEAD_DIM, w_buf), one(CONV_WIDTH - 1, CONV_CH), one(1, CONV_CH),
                  const(cw.shape), const(cb.shape), const(lg.shape), const(lb.shape),
                  pl.BlockSpec(memory_space=pl.ANY)],
        out_specs=(one(N_HEADS, HEAD_DIM), one(1, CONV_CH)),
        scratch_shapes=[gathered, gathered, pltpu.SemaphoreType.DMA((2,))],
    )
    return pl.pallas_call(
        functools.partial(_mix_sample_kernel, n_pages=n_pages, n_gather=n_gather, past=past, w_buf=w_buf),
        grid_spec=grid_spec,
        out_shape=(jax.ShapeDtypeStruct((bsz, N_HEADS, HEAD_DIM), F32),
                   jax.ShapeDtypeStruct((bsz, 1, CONV_CH), F32)),
        compiler_params=_params("arbitrary"), name="mix_sample",
    )(idx.reshape(-1), page_table.reshape(-1), q8, gates8, o_c, kvs, kvw, win, conv_state, a, cw, cb, lg, lb,
      cache_slabs)


def _inproj_weight(w_in):
    off = ATTN_WIDTH + N_BRANCH * KV_COLS
    n_gate = N_HEADS * N_BRANCH
    gate = jnp.pad(w_in[:, off:off + n_gate], ((0, 0), (0, GATE_PAD - n_gate)))
    return jnp.concatenate([w_in[:, :off], gate, w_in[:, off + n_gate:]], axis=1).astype(BF16)


def _compress_weights(pos_k, w1_k, w2_k, pos_v, w1_v, w2_v):
    n_slot = 2 * KV_HEADS
    eye = jnp.eye(n_slot, dtype=F32)

    def place(per_slot):
        w = jnp.stack(per_slot, axis=0)
        full = jnp.einsum('jclde,jk->ljdcke', w, eye)
        return full.reshape(CHUNK_COLS, N_SUB * KV_COLS)

    w1k = w1_k.reshape(N_SUB, CMP_STRIDE, HEAD_DIM, HEAD_DIM)
    w1v = w1_v.reshape(N_SUB, CMP_STRIDE, HEAD_DIM, HEAD_DIM)
    wc = place([w1k, w1k, w1v, w1v]).astype(BF16)
    pk = pos_k.reshape(N_SUB, CMP_STRIDE, 1, HEAD_DIM)
    pv = pos_v.reshape(N_SUB, CMP_STRIDE, 1, HEAD_DIM)
    posx = jnp.concatenate([pk, pk, pv, pv], axis=2).reshape(N_SUB, CHUNK_COLS)
    posx = jnp.pad(posx, ((0, 8 - N_SUB), (0, 0))).astype(BF16)
    w2 = jnp.einsum('jef,jk->jekf', jnp.stack([w2_k, w2_k, w2_v, w2_v]), eye).reshape(KV_COLS, KV_COLS).astype(BF16)
    return wc, posx, w2


def _split_kv(wc):
    half = KV_COLS // 2
    w = wc.reshape(CMP_STRIDE, 2, half, N_SUB, 2, half)
    return jnp.stack([w[:, kv, :, :, kv, :].reshape(CMP_STRIDE * half, N_SUB * half) for kv in range(2)])


def _selection_matrix(n_rows, n_cols):
    n = jnp.arange(n_rows)[:, None]
    j = jnp.arange(n_cols)[None, :]
    cnt = jnp.zeros((n_rows, n_cols), F32)
    for m in range(SEL_RATIO):
        for sub in range(N_SUB):
            cnt = cnt + (SEL_RATIO * j + m - sub == n).astype(F32)
    return cnt.astype(BF16)


def kernel(x_prompt, x_sample, cache_cmp_kv, cache_slc_kv, state_win_kv, state_conv, state_pool, page_table,
           norm_mix, norm_ffn, norm_final, w_in_a, w_out_a, cmp_pos_k, cmp_w1_k, cmp_w2_k, cmp_pos_v, cmp_w1_v,
           cmp_w2_v, conv_w, conv_b, conv_ln_g, conv_ln_b, pool_w, pool_scale, w_ffn_gate, w_ffn_up, w_ffn_down):
    bp, tp, _ = x_prompt.shape
    bs, s_new, _ = x_sample.shape
    n_pages = page_table.shape[1]
    past = n_pages * PAGE_SIZE
    w_buf = state_win_kv.shape[2]
    n_pool = cache_cmp_kv.shape[1]
    ns_p = tp // SEL_BLOCK
    ns_s = -(-(past + s_new) // SEL_BLOCK)
    assert s_new == 1 and tp % SEL_CHUNK == 0 and tp >= WIN_KEYS and N_SELECT <= ns_p <= HEAD_DIM
    assert ns_s > N_SELECT and norm_mix.shape[0] == 2 and ns_p % SUBLANE == 0

    row = lambda v: v.reshape(1, -1)
    w_in = _inproj_weight(w_in_a[0])
    wc, posx, w2 = _compress_weights(cmp_pos_k[0], cmp_w1_k[0], cmp_w2_k[0], cmp_pos_v[0], cmp_w1_v[0], cmp_w2_v[0])
    woa, woc = w_out_a[0, :ATTN_WIDTH].astype(BF16), w_out_a[0, ATTN_WIDTH:].astype(BF16)
    wg, wu, wd = w_ffn_gate.astype(BF16), w_ffn_up.astype(BF16), w_ffn_down.astype(BF16)
    cw, cb, lg, lb = conv_w[0], row(conv_b[0]), row(conv_ln_g[0]), row(conv_ln_b[0])
    pw, ps = pool_w[0].astype(BF16), row(pool_scale[0])

    m = bp * tp
    xp = x_prompt.reshape(m, D_MODEL)
    q, kvc, kvs, kvw, gates, a, ksa, vs, kw, vw, kvct, kvst, kvwt = _inproj(
        xp, row(norm_mix[0]), w_in, tm=512, seq_len=tp, transposed_kv=True)
    cw_rows = jnp.broadcast_to(cw[:, None, :], (CONV_WIDTH, SUBLANE, CONV_CH))
    c_out = _conv_prompt(a.reshape(bp, tp, CONV_CH), cw_rows, cb, lg, lb, tc=512)
    kc, vc, pos_term = _compress_prompt(kvc.reshape(bp, tp // CMP_STRIDE, CHUNK_COLS), wc, posx, w2)
    a_out = _nsa_prompt(q, gates, ksa, vs, kw, vw, kc, vc, _selection_matrix(tp // CMP_STRIDE, LANE),
                        bsz=bp, seq_len=tp)
    xp = _ffn0(xp, a_out, c_out.reshape(m, CONV_CH), woa, woc, row(norm_ffn[0]), wg[0], wu[0], wd[0], tm=512)
    y_prompt, pool_tail = _layer1_prompt(xp.reshape(bp, tp, D_MODEL), row(norm_mix[1]), pw, ps, row(norm_ffn[1]),
                                         wg[1], wu[1], wd[1], row(norm_final), tm=512)
    kv6 = lambda z, b: z.reshape(1, b, -1, 2, KV_HEADS, HEAD_DIM)
    kv6_t = lambda zt: jnp.transpose(zt.reshape(bp, 2, KV_HEADS, HEAD_DIM, tp), (0, 4, 1, 2, 3))[None]
    new_cmp_p, new_slc_p = kv6_t(kvct), kv6_t(kvst)
    new_win_p = kv6_t(kvwt)[:, :, -min(WINDOW, tp):]
    new_conv_p = a.reshape(1, bp, tp, CONV_CH)[:, :, -(CONV_WIDTH - 1):]
    new_pool_p = pool_tail[None, :, -POOL_BUF:]

    xs = x_sample.reshape(bs, D_MODEL)
    q, kvc, kvs, kvw, gates, a = _inproj(xs, row(norm_mix[0]), w_in, tm=bs, seq_len=1, transposed_kv=False)[:6]
    q8 = q.reshape(bs, N_HEADS, HEAD_DIM)
    nsp = -(-ns_s // LANE) * LANE
    rows_last = lambda c: jnp.transpose(c, (0, 2, 3, 4, 1))
    o_c, idx = _cmp_sample(page_table, q8, rows_last(cache_cmp_kv[0]).reshape(n_pool, KV_COLS, PAGE_SIZE),
                           _split_kv(wc), pos_term, w2,
                           _selection_matrix(n_pages * (PAGE_SIZE // CMP_STRIDE), nsp), past=past, ns=ns_s)
    gates8 = gates[:, :N_HEADS * N_BRANCH].reshape(bs, N_HEADS, N_BRANCH)
    att, c_s = _mix_sample(idx, page_table, q8, gates8, o_c, kvs.reshape(bs, 1, KV_COLS), kvw.reshape(bs, 1, KV_COLS),
                           rows_last(state_win_kv[0]).reshape(bs, 2 * KV_HEADS, HEAD_DIM, w_buf), state_conv[0],
                           a.reshape(bs, 1, CONV_CH), cw, cb, lg, lb,
                           rows_last(cache_slc_kv[0]).reshape(n_pool * 2 * KV_HEADS, HEAD_DIM, PAGE_SIZE),
                           past=past, n_gather=N_SELECT - 1)
    xs = _ffn0(xs, att.reshape(bs, ATTN_WIDTH).astype(BF16), c_s.reshape(bs, CONV_CH).astype(BF16), woa, woc,
               row(norm_ffn[0]), wg[0], wu[0], wd[0], tm=bs)
    y_sample, h_s = _layer1_sample(xs, jnp.swapaxes(state_pool[0], 0, 1), row(norm_mix[1]), pw, ps, row(norm_ffn[1]),
                                   wg[1], wu[1], wd[1], row(norm_final), first_pos=past)
    new_win_s = jnp.concatenate([state_win_kv[0], kv6(kvw, bs)[0]], axis=1)[None, :, -w_buf:]
    new_conv_s = jnp.concatenate([state_conv[0], a.reshape(bs, 1, CONV_CH)], axis=1)[None, :, -(CONV_WIDTH - 1):]
    new_pool_s = jnp.concatenate([state_pool[0], h_s[:, None, :]], axis=1)[None, :, -POOL_BUF:]

    return (y_prompt, y_sample.reshape(bs, s_new, D_MODEL), new_cmp_p, kv6(kvc, bs), new_slc_p, kv6(kvs, bs),
            new_win_p, new_win_s, new_conv_p, new_conv_s, new_pool_p, new_pool_s)
```

```python
import functools

import jax
import jax.numpy as jnp
from jax import lax
from jax.experimental import pallas as pl
from jax.experimental.pallas import tpu as pltpu

F32 = jnp.float32
BF16 = jnp.bfloat16

D_MODEL = 1024
N_HEADS = 8
HEAD_DIM = 64
KV_HEADS = 2
GROUP = N_HEADS // KV_HEADS
ATTN_WIDTH = N_HEADS * HEAD_DIM
KV_COLS = 2 * KV_HEADS * HEAD_DIM
CMP_LEN = 32
CMP_STRIDE = 16
N_SUB = CMP_LEN // CMP_STRIDE
SEL_BLOCK = 64
SEL_RATIO = SEL_BLOCK // CMP_STRIDE
N_SELECT = 16
WINDOW = 512
PAGE_SIZE = 128
N_BRANCH = 3
SCALE = HEAD_DIM ** -0.5
CONV_CH = D_MODEL // 2
CONV_WIDTH = 31
POOL_WINDOWS = (2, 4, 8, 16)
POOL_GROUP = D_MODEL // len(POOL_WINDOWS)
POOL_BUF = max(POOL_WINDOWS) - 1
EPS = 1e-6
BIG = 1e9
NEG = -1e30
TINY = float(jnp.finfo(jnp.float32).tiny)

LANE = 128
SUBLANE = 8
CHUNK_COLS = CMP_STRIDE * KV_COLS
GATE_PAD = LANE
IN_COLS = ATTN_WIDTH + N_BRANCH * KV_COLS + GATE_PAD + 2 * CONV_CH
VMEM_LIMIT = 56 * 1024 * 1024

Q_TILE = 128
SEL_CHUNK = 512
WIN_KEYS = WINDOW + Q_TILE
ATT_ROWS = 64
CONV_HALO = 32
CONV_ROWS = 32
POOL_HALO = 16


def _dot(a, b):
    return jnp.dot(a, b, preferred_element_type=F32)


def _dot_nt(a, b):
    return lax.dot_general(a, b, (((1,), (1,)), ((), ())), preferred_element_type=F32)


def _dot_exact_lhs(a, b):
    hi = a.astype(BF16)
    r1 = a - hi.astype(F32)
    mid = r1.astype(BF16)
    lo = (r1 - mid.astype(F32)).astype(BF16)
    return _dot(hi, b) + _dot(mid, b) + _dot(lo, b)


def _rms(x, g):
    return x * lax.rsqrt(jnp.mean(x * x, axis=-1, keepdims=True) + EPS) * g


def _softmax_rows(s, mask):
    m = jnp.max(s, axis=-1, keepdims=True)
    e = jnp.where(mask, jnp.exp(s - m), 0.0)
    return e / jnp.maximum(jnp.sum(e, axis=-1, keepdims=True), TINY)


def _softmax_parts(s, mask):
    s = jnp.where(mask, s, NEG)
    e = jnp.where(mask, jnp.exp(s - jnp.max(s, axis=-1, keepdims=True)), 0.0)
    return e, 1.0 / jnp.maximum(jnp.sum(e, axis=-1, keepdims=True), TINY)


def _params(*sem):
    return pltpu.CompilerParams(dimension_semantics=sem, vmem_limit_bytes=VMEM_LIMIT)


def _const_spec(shape):
    nd = len(shape)
    return pl.BlockSpec(shape, lambda *_: (0,) * nd, pipeline_mode=pl.Buffered(1))


def _inproj_kernel(x_ref, g_ref, w_ref, q_ref, kvc_ref, kvs_ref, kvw_ref, gate_ref, a_ref,
                   ksa_ref, vs_ref, kw_ref, vw_ref, *kvt_refs, tm, seq_len):
    h = _rms(x_ref[...], g_ref[...])
    z = _dot(h.astype(BF16), w_ref[...])
    off = ATTN_WIDTH
    for br, kvt_ref in enumerate(kvt_refs):
        kvt_ref[0] = z[:, off + br * KV_COLS:off + (br + 1) * KV_COLS].T
    q_ref[...] = (z[:, :off] * SCALE).astype(BF16)
    kvc_ref[...] = z[:, off:off + KV_COLS]
    kvs = z[:, off + KV_COLS:off + 2 * KV_COLS]
    kvw = z[:, off + 2 * KV_COLS:off + 3 * KV_COLS]
    kvs_ref[...] = kvs
    kvw_ref[...] = kvw
    off += 3 * KV_COLS
    gate_ref[...] = jax.nn.sigmoid(z[:, off:off + GATE_PAD])
    off += GATE_PAD
    a_ref[...] = z[:, off:off + CONV_CH] * jax.nn.sigmoid(z[:, off + CONV_CH:])
    pos = (pl.program_id(0) * tm + lax.broadcasted_iota(jnp.int32, (tm, HEAD_DIM), 0)) % seq_len
    lane = lax.broadcasted_iota(jnp.int32, (tm, HEAD_DIM), 1)
    onehot = (pos // SEL_BLOCK == lane).astype(BF16)
    ones_col = (lane == 0).astype(BF16)
    for g in range(KV_HEADS):
        k0, v0 = g * HEAD_DIM, (KV_HEADS + g) * HEAD_DIM
        ksa_ref[g] = jnp.concatenate([kvs[:, k0:k0 + HEAD_DIM].astype(BF16), onehot], axis=1)
        vs_ref[g] = jnp.concatenate([kvs[:, v0:v0 + HEAD_DIM].astype(BF16), ones_col], axis=1)
        kw_ref[g] = kvw[:, k0:k0 + HEAD_DIM].astype(BF16)
        vw_ref[g] = jnp.concatenate([kvw[:, v0:v0 + HEAD_DIM].astype(BF16), ones_col], axis=1)


def _inproj(x, g, w, *, tm, seq_len, transposed_kv):
    m = x.shape[0]
    per_row = seq_len // tm if transposed_kv else 1
    row = lambda c: pl.BlockSpec((tm, c), lambda i: (i, 0))
    grp = lambda c: pl.BlockSpec((KV_HEADS, tm, c), lambda i: (0, i, 0))
    tr = pl.BlockSpec((1, KV_COLS, tm), lambda i: (i // per_row, 0, i % per_row))
    kvt = jax.ShapeDtypeStruct((m // seq_len, KV_COLS, seq_len), F32)
    n_t = N_BRANCH if transposed_kv else 0
    out_shape = (
        jax.ShapeDtypeStruct((m, ATTN_WIDTH), BF16),
        jax.ShapeDtypeStruct((m, KV_COLS), F32), jax.ShapeDtypeStruct((m, KV_COLS), F32),
        jax.ShapeDtypeStruct((m, KV_COLS), F32),
        jax.ShapeDtypeStruct((m, GATE_PAD), F32), jax.ShapeDtypeStruct((m, CONV_CH), F32),
        jax.ShapeDtypeStruct((KV_HEADS, m, 2 * HEAD_DIM), BF16),
        jax.ShapeDtypeStruct((KV_HEADS, m, 2 * HEAD_DIM), BF16),
        jax.ShapeDtypeStruct((KV_HEADS, m, HEAD_DIM), BF16),
        jax.ShapeDtypeStruct((KV_HEADS, m, 2 * HEAD_DIM), BF16),
    ) + (kvt,) * n_t
    out_specs = (row(ATTN_WIDTH), row(KV_COLS), row(KV_COLS), row(KV_COLS), row(GATE_PAD), row(CONV_CH),
                 grp(2 * HEAD_DIM), grp(2 * HEAD_DIM), grp(HEAD_DIM), grp(2 * HEAD_DIM)) + (tr,) * n_t
    return pl.pallas_call(
        functools.partial(_inproj_kernel, tm=tm, seq_len=seq_len),
        grid=(m // tm,),
        in_specs=[row(D_MODEL), _const_spec((1, D_MODEL)), _const_spec((D_MODEL, IN_COLS))],
        out_specs=out_specs, out_shape=out_shape,
        compiler_params=_params("parallel"), name="inproj",
    )(x, g, w)


def _conv_ln_silu(y, b, lg, lb):
    y = y + b
    mu = jnp.mean(y, axis=-1, keepdims=True)
    var = jnp.mean(jnp.square(y - mu), axis=-1, keepdims=True)
    return jax.nn.silu((y - mu) * lax.rsqrt(var + EPS) * lg + lb)


def _conv_kernel(prev_ref, a_ref, w_ref, b_ref, lg_ref, lb_ref, o_ref, ext_ref, sh_ref, *, tc):
    first = pl.program_id(1) == 0
    ext_ref[0:CONV_HALO, :] = jnp.where(first, 0.0, prev_ref[0])
    ext_ref[CONV_HALO:CONV_HALO + tc, :] = a_ref[0]
    lead = CONV_HALO - (CONV_WIDTH - 1)
    span = sh_ref.shape[1]
    for r in range(1, SUBLANE):
        sh_ref[r - 1] = ext_ref[r:r + span, :]

    for r0 in range(0, tc, CONV_ROWS):
        acc = jnp.zeros((CONV_ROWS, CONV_CH), F32)
        for k in range(CONV_WIDTH):
            r = (lead + k) % SUBLANE
            i0 = r0 + lead + k - r
            rows = ext_ref[i0:i0 + CONV_ROWS, :] if r == 0 else sh_ref[r - 1, i0:i0 + CONV_ROWS, :]
            acc = acc + jnp.concatenate([w_ref[k]] * (CONV_ROWS // SUBLANE), axis=0) * rows
        o_ref[0, r0:r0 + CONV_ROWS, :] = _conv_ln_silu(acc, b_ref[...], lg_ref[...], lb_ref[...]).astype(BF16)


def _conv_prompt(a, w, b, lg, lb, *, tc):
    bsz, t, _ = a.shape
    hb = tc // CONV_HALO
    return pl.pallas_call(
        functools.partial(_conv_kernel, tc=tc),
        grid=(bsz, t // tc),
        in_specs=[pl.BlockSpec((1, CONV_HALO, CONV_CH), lambda bi, i: (bi, jnp.maximum(i * hb - 1, 0), 0)),
                  pl.BlockSpec((1, tc, CONV_CH), lambda bi, i: (bi, i, 0)),
                  _const_spec((CONV_WIDTH, SUBLANE, CONV_CH)), _const_spec((1, CONV_CH)),
                  _const_spec((1, CONV_CH)), _const_spec((1, CONV_CH))],
        out_specs=pl.BlockSpec((1, tc, CONV_CH), lambda bi, i: (bi, i, 0)),
        out_shape=jax.ShapeDtypeStruct((bsz, t, CONV_CH), BF16),
        scratch_shapes=[pltpu.VMEM((CONV_HALO + tc, CONV_CH), F32),
                        pltpu.VMEM((SUBLANE - 1, CONV_HALO + tc - SUBLANE, CONV_CH), F32)],
        compiler_params=_params("parallel", "parallel"), name="conv_prompt",
    )(a, a, w, b, lg, lb)


def _compress_rows(h0, h1_next, pos_term, w2):
    hid = pos_term + h0 + h1_next
    return _dot(jax.nn.gelu(hid).astype(BF16), w2)


def _pos_term(posx_ref, wc_ref):
    hp = _dot(posx_ref[...], wc_ref[...])
    return hp[0:1, :KV_COLS] + hp[1:2, KV_COLS:]


def _chunk_row_perm():
    per_page = PAGE_SIZE // CMP_STRIDE
    r_out = lax.broadcasted_iota(jnp.int32, (PAGE_SIZE, PAGE_SIZE), 0)
    r_in = lax.broadcasted_iota(jnp.int32, (PAGE_SIZE, PAGE_SIZE), 1)
    return (r_in == (r_out % per_page) * CMP_STRIDE + r_out // per_page).astype(BF16)


def _compress_kernel(xt_ref, wc_ref, posx_ref, w2_ref, kc_ref, vc_ref, pos_ref, xl_ref):
    per_page = PAGE_SIZE // CMP_STRIDE
    n_groups = xt_ref.shape[2] // PAGE_SIZE
    batch = min(8, n_groups)
    perm = _chunk_row_perm()
    for i in range(n_groups // batch):
        groups = jnp.concatenate(
            [xt_ref[0, :, (i * batch + k) * PAGE_SIZE:(i * batch + k + 1) * PAGE_SIZE] for k in range(batch)], axis=0)
        xp = _dot_nt(perm, groups.astype(BF16))
        for k in range(batch):
            c0 = (i * batch + k) * per_page
            for l in range(CMP_STRIDE):
                xl_ref[l, c0:c0 + per_page, :] = xp[l * per_page:(l + 1) * per_page, k * KV_COLS:(k + 1) * KV_COLS]
    x = jnp.concatenate([xl_ref[l].astype(BF16) for l in range(CMP_STRIDE)], axis=1)
    hh = _dot(x, wc_ref[...])
    h1 = hh[:, KV_COLS:]
    h1_next = jnp.concatenate([h1[1:], jnp.zeros((1, KV_COLS), F32)], axis=0)
    pos_term = _pos_term(posx_ref, wc_ref)
    pos_ref[...] = jnp.broadcast_to(pos_term, pos_ref.shape)
    cmp = _compress_rows(hh[:, :KV_COLS], h1_next, pos_term, w2_ref[...])
    for g in range(KV_HEADS):
        kc_ref[0, g] = cmp[:, g * HEAD_DIM:(g + 1) * HEAD_DIM].astype(BF16)
        vc_ref[0, g] = cmp[:, (KV_HEADS + g) * HEAD_DIM:(KV_HEADS + g + 1) * HEAD_DIM].astype(BF16)


def _compress_prompt(kvct, wc, posx, w2):
    bsz, _, t = kvct.shape
    n_ch = t // CMP_STRIDE
    tok = jax.ShapeDtypeStruct((bsz, KV_HEADS, n_ch, HEAD_DIM), BF16)
    tok_spec = pl.BlockSpec((1, KV_HEADS, n_ch, HEAD_DIM), lambda bi: (bi, 0, 0, 0))
    return pl.pallas_call(
        _compress_kernel,
        grid=(bsz,),
        in_specs=[pl.BlockSpec((1, KV_COLS, t), lambda bi: (bi, 0, 0)),
                  _const_spec(wc.shape), _const_spec(posx.shape), _const_spec(w2.shape)],
        out_specs=(tok_spec, tok_spec, pl.BlockSpec((8, KV_COLS), lambda bi: (0, 0))),
        out_shape=(tok, tok, jax.ShapeDtypeStruct((8, KV_COLS), F32)),
        scratch_shapes=[pltpu.VMEM((CMP_STRIDE, n_ch, KV_COLS), F32)],
        compiler_params=_params("arbitrary"), name="compress_prompt",
    )(kvct, wc, posx, w2)


def _block_scores(imp, ssel, t_col, ns):
    s = _dot_exact_lhs(imp, ssel)
    j = lax.broadcasted_iota(jnp.int32, s.shape, 1)
    cur = t_col // SEL_BLOCK
    valid = j * SEL_BLOCK <= t_col
    forced = valid & ((j == 0) | (j == cur) | (j == cur - 1))
    s = jnp.where(forced, BIG, jnp.where(valid, s, -BIG))
    return jnp.where(j < ns, s, -3.0 * BIG)


def _nsa_prompt_kernel(q_ref, gate_ref, ksa_ref, vs_ref, kw_ref, vw_ref, kc_ref, vc_ref, ssel_ref, o_ref,
                       qa_ref, s_ref, p_ref, part_ref, m_ref, acc_ref, bias_ref, *, tq, ns):
    i = pl.program_id(1)
    q0 = i * tq
    t_col = q0 + lax.broadcasted_iota(jnp.int32, (tq, 1), 0)
    t4 = jnp.concatenate([t_col] * GROUP, axis=0)
    gates = gate_ref[...]
    n_blk = kc_ref.shape[2]
    rows = GROUP * tq
    end = lax.broadcasted_iota(jnp.int32, (1, n_blk), 1) * CMP_STRIDE + (CMP_LEN - 1)
    cmask = end <= t4

    sub = lax.broadcasted_iota(jnp.int32, (SUBLANE, tq), 0)
    o_cs, q4s = [], []
    for g in range(KV_HEADS):
        q4 = jnp.concatenate(
            [q_ref[:, (GROUP * g + h) * HEAD_DIM:(GROUP * g + h + 1) * HEAD_DIM] for h in range(GROUP)], axis=0)
        q4s.append(q4)

        e_c, r_c = _softmax_parts(_dot_nt(q4, kc_ref[0, g]), cmask)
        o_cs.append(_dot(e_c.astype(BF16), vc_ref[0, g]) * r_c)
        p_c = e_c * r_c
        imp = p_c[0:tq] + p_c[tq:2 * tq] + p_c[2 * tq:3 * tq] + p_c[3 * tq:4 * tq]

        s_t = _block_scores(imp, ssel_ref[...], t_col, ns).T
        blocks = [s_t[v * SUBLANE:(v + 1) * SUBLANE] for v in range(ns // SUBLANE)]
        ranks = [jnp.zeros((SUBLANE, tq), F32) for _ in blocks]
        for i2 in range(ns):
            row = s_t[i2:i2 + 1, :]
            for v, blk in enumerate(blocks):
                if v > i2 // SUBLANE:
                    beats = row >= blk
                elif v < i2 // SUBLANE:
                    beats = row > blk
                else:
                    beats = (row > blk) | ((row == blk) & (sub > i2 % SUBLANE))
                ranks[v] = ranks[v] + jnp.where(beats, 1.0, 0.0)
        sel_t = jnp.concatenate([jnp.where(r < N_SELECT, 0.0, NEG) for r in ranks]
                                + [jnp.full((LANE - ns, tq), NEG, F32)], axis=0)
        sel = sel_t.T
        selbias = jnp.concatenate([sel[:, :HEAD_DIM].astype(BF16)] * GROUP, axis=0)
        qa_ref[g] = jnp.concatenate([q4, selbias], axis=1)

    def set_bias(b, k0, width, key_ok):
        t = t_col
        key = k0 + lax.broadcasted_iota(jnp.int32, (1, width), 1)
        bias_ref[b, :, 0:width] = jnp.where(key_ok(t, key), 0.0, NEG)

    def scores(slot, q, k_ref, g, k0, width, bias):
        s_ref[slot, :, 0:width] = _dot_nt(q, k_ref[g, pl.ds(k0, width), :])

        for r0 in range(0, rows, ATT_ROWS):
            sb = s_ref[slot, r0:r0 + ATT_ROWS, 0:width]
            if bias is not None:
                sb = sb + bias_ref[bias, r0 % tq:r0 % tq + ATT_ROWS, 0:width]
                s_ref[slot, r0:r0 + ATT_ROWS, 0:width] = sb
            mx = sb[:, 0:LANE]
            for j in range(1, width // LANE):
                mx = jnp.maximum(mx, sb[:, j * LANE:(j + 1) * LANE])
            part_ref[slot, r0:r0 + ATT_ROWS, :] = mx
        return jnp.max(part_ref[slot], axis=-1, keepdims=True)

    def weights(slot, width):
        for r0 in range(0, rows, ATT_ROWS):
            m = m_ref[slot, r0:r0 + ATT_ROWS, :]
            sb = s_ref[slot, r0:r0 + ATT_ROWS, 0:width]
            p_ref[slot, r0:r0 + ATT_ROWS, 0:width] = jnp.exp(sb - jnp.concatenate([m] * (width // LANE), axis=1)).astype(BF16)

    w0 = pl.multiple_of(jnp.maximum(q0 - WINDOW, 0), Q_TILE)
    set_bias(1, w0, WIN_KEYS, lambda t, key: (t - key >= 0) & (t - key < WINDOW))

    def window(g):
        slot = KV_HEADS + g
        m_ref[slot] = jnp.broadcast_to(scores(slot, q4s[g], kw_ref, g, w0, WIN_KEYS, 1), (rows, LANE))
        weights(slot, WIN_KEYS)
        acc_w = _dot(p_ref[slot, :, 0:WIN_KEYS], vw_ref[g, pl.ds(w0, WIN_KEYS), :])
        return acc_w[:, :HEAD_DIM] * (1.0 / acc_w[:, HEAD_DIM:HEAD_DIM + 1])

    o_w = [window(0)]

    for g in range(KV_HEADS):
        m_ref[g] = jnp.full((rows, LANE), NEG, F32)
        acc_ref[g] = jnp.zeros((rows, LANE), F32)

    def chunk(k0, bias):
        blk_max = [scores(g, qa_ref[g], ksa_ref, g, k0, SEL_CHUNK, bias) for g in range(KV_HEADS)]
        for g in range(KV_HEADS):
            m_old = m_ref[g]
            m_new = jnp.maximum(m_old, blk_max[g])
            m_ref[g] = m_new
            weights(g, SEL_CHUNK)
            acc_ref[g] = (jnp.exp(m_old - m_new) * acc_ref[g]
                          + _dot(p_ref[g, :, 0:SEL_CHUNK], vs_ref[g, pl.ds(k0, SEL_CHUNK), :]))

    n_full = q0 // SEL_CHUNK

    def full_chunk(c, carry):
        chunk(pl.multiple_of(c * SEL_CHUNK, SEL_CHUNK), None)
        return carry

    lax.fori_loop(0, n_full, full_chunk, 0)

    tail0 = pl.multiple_of(jnp.maximum(q0 + tq - SEL_CHUNK, 0), Q_TILE)
    set_bias(0, tail0, SEL_CHUNK, lambda t, key: (key <= t) & (key >= n_full * SEL_CHUNK))
    chunk(tail0, 0)
    o_w.append(window(1))

    for g in range(KV_HEADS):
        o_c = o_cs[g]
        acc_s = acc_ref[g]
        o_s = acc_s[:, :HEAD_DIM] * (1.0 / acc_s[:, HEAD_DIM:HEAD_DIM + 1])
        for h in range(GROUP):
            hh = GROUP * g + h
            r = slice(h * tq, (h + 1) * tq)
            c = N_BRANCH * hh
            o = gates[:, c:c + 1] * o_c[r] + gates[:, c + 1:c + 2] * o_s[r] + gates[:, c + 2:c + 3] * o_w[g][r]
            o_ref[:, hh * HEAD_DIM:(hh + 1) * HEAD_DIM] = o.astype(BF16)


def _nsa_prompt(q, gates, ksa, vs, kw, vw, kc, vc, ssel, *, bsz, seq_len):
    tq = Q_TILE
    nt = seq_len // tq
    ns = seq_len // SEL_BLOCK
    n_blk = kc.shape[2]
    rows = GROUP * tq
    row = lambda c: pl.BlockSpec((tq, c), lambda bi, i: (bi * nt + i, 0))
    seq = lambda c: pl.BlockSpec((KV_HEADS, seq_len, c), lambda bi, i: (0, bi, 0))
    tok = pl.BlockSpec((1, KV_HEADS, n_blk, HEAD_DIM), lambda bi, i: (bi, 0, 0, 0))
    return pl.pallas_call(
        functools.partial(_nsa_prompt_kernel, tq=tq, ns=ns),
        grid=(bsz, nt),
        in_specs=[row(ATTN_WIDTH), row(GATE_PAD), seq(2 * HEAD_DIM), seq(2 * HEAD_DIM), seq(HEAD_DIM),
                  seq(2 * HEAD_DIM), tok, tok, _const_spec(ssel.shape)],
        out_specs=row(ATTN_WIDTH),
        out_shape=jax.ShapeDtypeStruct((bsz * seq_len, ATTN_WIDTH), BF16),
        scratch_shapes=[pltpu.VMEM((KV_HEADS, rows, 2 * HEAD_DIM), BF16),
                        pltpu.VMEM((2 * KV_HEADS, rows, WIN_KEYS), F32),
                        pltpu.VMEM((2 * KV_HEADS, rows, WIN_KEYS), BF16), pltpu.VMEM((2 * KV_HEADS, rows, LANE), F32),
                        pltpu.VMEM((2 * KV_HEADS, rows, LANE), F32), pltpu.VMEM((KV_HEADS, rows, LANE), F32),
                        pltpu.VMEM((2, tq, WIN_KEYS), F32)],
        compiler_params=_params("parallel", "parallel"), name="nsa_prompt",
    )(q, gates, ksa, vs, kw, vw, kc, vc, ssel)


def _swiglu_residual(x1, gf, wg_ref, wu_ref, wd_ref):
    h = _rms(x1, gf).astype(BF16)
    act = jax.nn.silu(_dot(h, wg_ref[...])) * _dot(h, wu_ref[...])
    return x1 + _dot(act.astype(BF16), wd_ref[...])


def _ffn0_kernel(x_ref, a_ref, c_ref, woa_ref, woc_ref, gf_ref, wg_ref, wu_ref, wd_ref, o_ref):
    x1 = x_ref[...] + (_dot(a_ref[...], woa_ref[...]) + _dot(c_ref[...], woc_ref[...]))
    o_ref[...] = _swiglu_residual(x1, gf_ref[...], wg_ref, wu_ref, wd_ref)


def _ffn0(x, a, c, woa, woc, gf, wg, wu, wd, *, tm):
    m = x.shape[0]
    row = lambda cols: pl.BlockSpec((tm, cols), lambda i: (i, 0))
    return pl.pallas_call(
        _ffn0_kernel,
        grid=(m // tm,),
        in_specs=[row(D_MODEL), row(ATTN_WIDTH), row(CONV_CH), _const_spec(woa.shape), _const_spec(woc.shape),
                  _const_spec(gf.shape), _const_spec(wg.shape), _const_spec(wu.shape), _const_spec(wd.shape)],
        out_specs=row(D_MODEL), out_shape=jax.ShapeDtypeStruct((m, D_MODEL), F32),
        compiler_params=_params("parallel"), name="outproj_ffn",
    )(x, a, c, woa, woc, gf, wg, wu, wd)


def _pool_mix(x, h, win_sums, cnts, pw_ref, ps):
    ys = []
    for g in range(len(POOL_WINDOWS)):
        z = win_sums[g] / cnts[g] - h[:, g * POOL_GROUP:(g + 1) * POOL_GROUP]
        ys.append(_dot(z.astype(BF16), pw_ref[g]))
    return x + jnp.concatenate(ys, axis=1) * ps


def _layer1_prompt_kernel(xprev_ref, x_ref, gm_ref, pw_ref, ps_ref, gf_ref, wg_ref, wu_ref, wd_ref, gfin_ref,
                          y_ref, hst_ref, *, tm):
    i = pl.program_id(1)
    x = x_ref[0]
    h = _rms(x, gm_ref[...])
    hprev = jnp.where(i == 0, 0.0, _rms(xprev_ref[0], gm_ref[...]))
    hst_ref[0] = h[tm - POOL_HALO:, :]
    pos = i * tm + lax.broadcasted_iota(jnp.int32, (tm, 1), 0)
    sums, cnts = [], []
    for g, w in enumerate(POOL_WINDOWS):
        e = jnp.concatenate([hprev[:, g * POOL_GROUP:(g + 1) * POOL_GROUP],
                             h[:, g * POOL_GROUP:(g + 1) * POOL_GROUP]], axis=0)
        span = 1
        while span < w:
            e = e[span:] + e[:-span]
            span *= 2
        first = POOL_HALO - (w - 1)
        sums.append(e[first:first + tm])
        cnts.append(jnp.minimum(w, pos + 1).astype(F32))
    x1 = _pool_mix(x, h, sums, cnts, pw_ref, ps_ref[...])
    x2 = _swiglu_residual(x1, gf_ref[...], wg_ref, wu_ref, wd_ref)
    y_ref[0] = _rms(x2, gfin_ref[...])


def _layer1_prompt(x, gm, pw, ps, gf, wg, wu, wd, gfin, *, tm):
    bsz, t, _ = x.shape
    hb = tm // POOL_HALO
    return pl.pallas_call(
        functools.partial(_layer1_prompt_kernel, tm=tm),
        grid=(bsz, t // tm),
        in_specs=[pl.BlockSpec((1, POOL_HALO, D_MODEL), lambda bi, i: (bi, jnp.maximum(i * hb - 1, 0), 0)),
                  pl.BlockSpec((1, tm, D_MODEL), lambda bi, i: (bi, i, 0)),
                  _const_spec(gm.shape), _const_spec(pw.shape), _const_spec(ps.shape), _const_spec(gf.shape),
                  _const_spec(wg.shape), _const_spec(wu.shape), _const_spec(wd.shape), _const_spec(gfin.shape)],
        out_specs=(pl.BlockSpec((1, tm, D_MODEL), lambda bi, i: (bi, i, 0)),
                   pl.BlockSpec((1, POOL_HALO, D_MODEL), lambda bi, i: (bi, 0, 0))),
        out_shape=(jax.ShapeDtypeStruct((bsz, t, D_MODEL), F32),
                   jax.ShapeDtypeStruct((bsz, POOL_HALO, D_MODEL), F32)),
        compiler_params=_params("parallel", "arbitrary"), name="layer1_prompt",
    )(x, x, gm, pw, ps, gf, wg, wu, wd, gfin)


def _layer1_sample_kernel(x_ref, hist_ref, gm_ref, pw_ref, ps_ref, gf_ref, wg_ref, wu_ref, wd_ref, gfin_ref,
                          y_ref, h_ref, *, first_pos):
    x = x_ref[...]
    h = _rms(x, gm_ref[...])
    h_ref[...] = h
    sums, cnts = [], []
    for g, w in enumerate(POOL_WINDOWS):
        c = slice(g * POOL_GROUP, (g + 1) * POOL_GROUP)
        s = h[:, c]
        for k in range(1, w):
            s = s + hist_ref[POOL_BUF - k][:, c]
        sums.append(s)
        cnts.append(float(min(w, first_pos + 1)))
    x1 = _pool_mix(x, h, sums, cnts, pw_ref, ps_ref[...])
    x2 = _swiglu_residual(x1, gf_ref[...], wg_ref, wu_ref, wd_ref)
    y_ref[...] = _rms(x2, gfin_ref[...])


def _layer1_sample(x, hist, gm, pw, ps, gf, wg, wu, wd, gfin, *, first_pos):
    m = x.shape[0]
    args = (x, hist, gm, pw, ps, gf, wg, wu, wd, gfin)
    out = jax.ShapeDtypeStruct((m, D_MODEL), F32)
    return pl.pallas_call(
        functools.partial(_layer1_sample_kernel, first_pos=first_pos),
        grid=(1,),
        in_specs=[_const_spec(a.shape) for a in args],
        out_specs=(_const_spec((m, D_MODEL)), _const_spec((m, D_MODEL))), out_shape=(out, out),
        compiler_params=_params("arbitrary"), name="layer1_sample",
    )(*args)


def _page_copy(cache_ref, xt_ref, sem, phys, p):
    return pltpu.make_async_copy(cache_ref.at[phys], xt_ref.at[p], sem)


def _cmp_sample_kernel(pt_ref, q_ref, cache_ref, wkv_ref, pos_ref, w2_ref, ssel_ref, oc_ref, idx_ref,
                       xt_ref, xl_ref, hbuf_ref, sem, *, n_pages, past, ns, nsp):
    b = pl.program_id(0)
    n_ch = n_pages * (PAGE_SIZE // CMP_STRIDE)

    def gather(sample):
        def start(p, carry):
            _page_copy(cache_ref, xt_ref, sem, pt_ref[sample * n_pages + p], p).start()
            return carry

        lax.fori_loop(0, n_pages, start, 0)

    def wait(p, carry):
        _page_copy(cache_ref, xt_ref, sem, 0, p).wait()
        return carry

    per_page = PAGE_SIZE // CMP_STRIDE
    perm = _chunk_row_perm()

    batch = 8

    def to_rows(i, carry):
        p0 = pl.multiple_of(i * batch, batch)
        pages = xt_ref[pl.ds(p0, batch)].reshape(batch * KV_COLS, PAGE_SIZE)
        xp = _dot_nt(perm, pages.astype(BF16))
        for k in range(batch):
            c0 = pl.multiple_of((p0 + k) * per_page, per_page)
            for l in range(CMP_STRIDE):
                xl_ref[l, pl.ds(c0, per_page), :] = xp[l * per_page:(l + 1) * per_page, k * KV_COLS:(k + 1) * KV_COLS]
        return carry

    @pl.when(b == 0)
    def _():
        gather(0)

    lax.fori_loop(0, n_pages, wait, 0)
    lax.fori_loop(0, n_pages // batch, to_rows, 0)

    @pl.when(b + 1 < pl.num_programs(0))
    def _():
        gather(b + 1)

    rows = min(256, n_ch)
    half = KV_COLS // 2
    for r0 in range(0, n_ch, rows):
        for kv in range(2):
            chunk_rows = jnp.concatenate(
                [xl_ref[l, r0:r0 + rows, kv * half:(kv + 1) * half].astype(BF16) for l in range(CMP_STRIDE)], axis=1)
            hbuf_ref[r0:r0 + rows, kv * KV_COLS:(kv + 1) * KV_COLS] = _dot(chunk_rows, wkv_ref[kv])
    hbuf_ref[n_ch:n_ch + 8, :] = jnp.zeros((8, 2 * KV_COLS), F32)
    first = lambda r: jnp.concatenate([hbuf_ref[r, 0:half], hbuf_ref[r, KV_COLS:KV_COLS + half]], axis=1)
    second = lambda r: jnp.concatenate([hbuf_ref[r, half:KV_COLS], hbuf_ref[r, KV_COLS + half:]], axis=1)
    cmp = _compress_rows(first(pl.ds(0, n_ch)), second(pl.ds(1, n_ch)), pos_ref[0:1, :], w2_ref[...])

    q8 = q_ref[0]
    head = lax.broadcasted_iota(jnp.int32, (N_HEADS, 1), 0)
    end = lax.broadcasted_iota(jnp.int32, (1, n_ch), 1) * CMP_STRIDE + (CMP_LEN - 1)
    cmask = jnp.broadcast_to(end <= past, (N_HEADS, n_ch))
    o_c = jnp.zeros((N_HEADS, HEAD_DIM), F32)
    jl = lax.broadcasted_iota(jnp.int32, (1, nsp), 1)
    ii = lax.broadcasted_iota(jnp.int32, (nsp, nsp), 0)
    jj = lax.broadcasted_iota(jnp.int32, (nsp, nsp), 1)
    cur = past // SEL_BLOCK
    t_col = jnp.full((N_HEADS, 1), past, jnp.int32)
    for g in range(KV_HEADS):
        in_group = head // GROUP == g
        kc = cmp[:, g * HEAD_DIM:(g + 1) * HEAD_DIM].astype(BF16)
        vc = cmp[:, (KV_HEADS + g) * HEAD_DIM:(KV_HEADS + g + 1) * HEAD_DIM].astype(BF16)
        sc = _dot_nt(q8, kc)
        p_c = _softmax_rows(jnp.where(cmask, sc, NEG), cmask)
        o_c = jnp.where(in_group, _dot(p_c.astype(BF16), vc), o_c)
        imp = jnp.sum(jnp.where(in_group, p_c, 0.0), axis=0, keepdims=True)
        s_b = _block_scores(jnp.broadcast_to(imp, (N_HEADS, n_ch)), ssel_ref[...], t_col, ns)
        s_b = jnp.broadcast_to(s_b[0:1], (nsp, nsp))
        s_t = s_b.T
        beats = (s_t > s_b) | ((s_t == s_b) & (ii < jj))
        rank = jnp.sum(jnp.where(beats, 1.0, 0.0), axis=0, keepdims=True)
        sel = (rank < N_SELECT) & (jl != cur)
        sel_f = jnp.where(sel, 1.0, 0.0)
        before = _dot(jnp.broadcast_to(sel_f, (N_HEADS, nsp)).astype(BF16),
                      jnp.where(ii < jj, 1.0, 0.0).astype(BF16))[0:1]
        slot = lax.broadcasted_iota(jnp.int32, (N_SELECT, nsp), 0)
        pick = jnp.broadcast_to(sel, (N_SELECT, nsp)) & (jnp.broadcast_to(before, (N_SELECT, nsp)) == slot.astype(F32))
        blk = jnp.sum(jnp.where(pick, lax.broadcasted_iota(jnp.int32, (N_SELECT, nsp), 1), 0), axis=1, keepdims=True)
        idx_ref[0, g] = blk
    oc_ref[0] = o_c


def _cmp_sample(page_table, q8, cache_pages, wkv, pos_term, w2, ssel, *, past, ns):
    bsz, n_pages = page_table.shape
    n_ch = n_pages * (PAGE_SIZE // CMP_STRIDE)
    nsp = ssel.shape[1]
    grid_spec = pltpu.PrefetchScalarGridSpec(
        num_scalar_prefetch=1, grid=(bsz,),
        in_specs=[pl.BlockSpec((1, N_HEADS, HEAD_DIM), lambda b, pt: (b, 0, 0)),
                  pl.BlockSpec(memory_space=pl.ANY),
                  _const_spec(wkv.shape), _const_spec(pos_term.shape), _const_spec(w2.shape), _const_spec(ssel.shape)],
        out_specs=(pl.BlockSpec((1, N_HEADS, HEAD_DIM), lambda b, pt: (b, 0, 0)),
                   pl.BlockSpec((1, KV_HEADS, N_SELECT, 1), lambda b, pt: (b, 0, 0, 0))),
        scratch_shapes=[pltpu.VMEM((n_pages, KV_COLS, PAGE_SIZE), F32), pltpu.VMEM((CMP_STRIDE, n_ch, KV_COLS), F32),
                        pltpu.VMEM((n_ch + 8, 2 * KV_COLS), F32), pltpu.SemaphoreType.DMA(())],
    )
    return pl.pallas_call(
        functools.partial(_cmp_sample_kernel, n_pages=n_pages, past=past, ns=ns, nsp=nsp),
        grid_spec=grid_spec,
        out_shape=(jax.ShapeDtypeStruct((bsz, N_HEADS, HEAD_DIM), F32),
                   jax.ShapeDtypeStruct((bsz, KV_HEADS, N_SELECT, 1), jnp.int32)),
        compiler_params=_params("arbitrary"), name="cmp_sample",
    )(page_table.reshape(-1), q8, cache_pages, wkv, pos_term, w2, ssel)


def _slab_copy(cache_ref, dst_ref, sem, src, buf, g, k):
    return pltpu.make_async_copy(cache_ref.at[src], dst_ref.at[buf, g, :, pl.ds(k * PAGE_SIZE, PAGE_SIZE)],
                                 sem.at[buf])


def _mix_sample_kernel(idx_ref, pt_ref, q_ref, gate_ref, oc_ref, kvs_ref, kvw_ref, win_ref, cst_ref, a_ref,
                       cw_ref, cb_ref, lg_ref, lb_ref, cache_ref, att_ref, conv_ref, kt_ref, vt_ref, sem,
                       *, n_pages, n_gather, past, w_buf):
    b = pl.program_id(0)
    per_page = PAGE_SIZE // SEL_BLOCK
    buf = b % 2

    def block_of(sample, g, k):
        return idx_ref[(sample * KV_HEADS + g) * N_SELECT + k]

    def gather(sample, into):
        for g in range(KV_HEADS):
            for k in range(n_gather):
                blk = block_of(sample, g, k)
                slab = pt_ref[sample * n_pages + blk // per_page] * (2 * KV_HEADS) + g
                _slab_copy(cache_ref, kt_ref, sem, slab, into, g, k).start()
                _slab_copy(cache_ref, vt_ref, sem, slab + KV_HEADS, into, g, k).start()

    @pl.when(b == 0)
    def _():
        gather(0, 0)

    @pl.when(b + 1 < pl.num_programs(0))
    def _():
        gather(b + 1, 1 - buf)

    y = (jnp.sum(cw_ref[0:CONV_WIDTH - 1, :] * cst_ref[0], axis=0, keepdims=True)
         + cw_ref[CONV_WIDTH - 1:CONV_WIDTH, :] * a_ref[0])
    conv_ref[0] = _conv_ln_silu(y, cb_ref[...], lg_ref[...], lb_ref[...])

    def wait(s, carry):
        _slab_copy(cache_ref, kt_ref, sem, 0, buf, 0, 0).wait()
        return carry

    lax.fori_loop(0, 2 * KV_HEADS * n_gather, wait, 0)
    halves = [block_of(b, g, k) % per_page for g in range(KV_HEADS) for k in range(n_gather)]

    q8 = q_ref[0]
    q8f = q8.astype(F32)
    head = lax.broadcasted_iota(jnp.int32, (N_HEADS, 1), 0)
    gates = gate_ref[0]
    kvs_new, kvw_new = kvs_ref[0], kvw_ref[0]
    jw = lax.broadcasted_iota(jnp.int32, (1, w_buf), 1)
    wdiff = w_buf - jw
    wmask = jnp.broadcast_to((wdiff < WINDOW) & (past - wdiff >= 0), (N_HEADS, w_buf))
    half_of_lane = lax.broadcasted_iota(jnp.int32, (1, PAGE_SIZE), 1) // SEL_BLOCK
    out = jnp.zeros((N_HEADS, HEAD_DIM), F32)

    def with_new_row(kt, vt, mask, k_new, v_new):
        s = jnp.where(mask, _dot(q8, kt.astype(BF16)), NEG)
        s_new = jnp.sum(q8f * k_new.astype(BF16).astype(F32), axis=-1, keepdims=True)
        m = jnp.maximum(jnp.max(s, axis=-1, keepdims=True), s_new)
        e = jnp.where(mask, jnp.exp(s - m), 0.0)
        e_new = jnp.exp(s_new - m)
        num = _dot_nt(e.astype(BF16), vt.astype(BF16)) + e_new * v_new
        return num / (jnp.sum(e, axis=-1, keepdims=True) + e_new)

    for g in range(KV_HEADS):
        kc, vc = slice(g * HEAD_DIM, (g + 1) * HEAD_DIM), slice((KV_HEADS + g) * HEAD_DIM, (KV_HEADS + g + 1) * HEAD_DIM)
        smask = jnp.concatenate([half_of_lane == halves[g * n_gather + k] for k in range(n_gather)], axis=1)
        smask = jnp.broadcast_to(smask, (N_HEADS, n_gather * PAGE_SIZE))
        o_s = with_new_row(kt_ref[buf, g], vt_ref[buf, g], smask, kvs_new[:, kc], kvs_new[:, vc])
        o_w = with_new_row(win_ref[0, g], win_ref[0, KV_HEADS + g], wmask, kvw_new[:, kc], kvw_new[:, vc])
        mixed = gates[:, 0:1] * oc_ref[0] + gates[:, 1:2] * o_s + gates[:, 2:3] * o_w
        out = jnp.where(head // GROUP == g, mixed, out)
    att_ref[0] = out


def _mix_sample(idx, page_table, q8, gates8, o_c, kvs, kvw, win, conv_state, a, cw, cb, lg, lb, cache_slabs,
                *, past, n_gather):
    bsz, n_pages = page_table.shape
    w_buf = win.shape[-1]
    one = lambda *shape: pl.BlockSpec((1,) + shape, lambda b, *_: (b,) + (0,) * len(shape))
    const = lambda shape: pl.BlockSpec(shape, lambda b, *_: (0,) * len(shape))
    gathered = pltpu.VMEM((2, KV_HEADS, HEAD_DIM, n_gather * PAGE_SIZE), F32)
    grid_spec = pltpu.PrefetchScalarGridSpec(
        num_scalar_prefetch=2, grid=(bsz,),
        in_specs=[one(N_HEADS, HEAD_DIM), one(N_HEADS, N_BRANCH), one(N_HEADS, HEAD_DIM), one(1, KV_COLS),
                  one(1, KV_COLS), one(2 * KV_HEADS, HEAD_DIM, w_buf), one(CONV_WIDTH - 1, CONV_CH), one(1, CONV_CH),
                  const(cw.shape), const(cb.shape), const(lg.shape), const(lb.shape),
                  pl.BlockSpec(memory_space=pl.ANY)],
        out_specs=(one(N_HEADS, HEAD_DIM), one(1, CONV_CH)),
        scratch_shapes=[gathered, gathered, pltpu.SemaphoreType.DMA((2,))],
    )
    return pl.pallas_call(
        functools.partial(_mix_sample_kernel, n_pages=n_pages, n_gather=n_gather, past=past, w_buf=w_buf),
        grid_spec=grid_spec,
        out_shape=(jax.ShapeDtypeStruct((bsz, N_HEADS, HEAD_DIM), F32),
                   jax.ShapeDtypeStruct((bsz, 1, CONV_CH), F32)),
        compiler_params=_params("arbitrary"), name="mix_sample",
    )(idx.reshape(-1), page_table.reshape(-1), q8, gates8, o_c, kvs, kvw, win, conv_state, a, cw, cb, lg, lb,
      cache_slabs)


def _inproj_weight(w_in):
    off = ATTN_WIDTH + N_BRANCH * KV_COLS
    n_gate = N_HEADS * N_BRANCH
    gate = jnp.pad(w_in[:, off:off + n_gate], ((0, 0), (0, GATE_PAD - n_gate)))
    return jnp.concatenate([w_in[:, :off], gate, w_in[:, off + n_gate:]], axis=1).astype(BF16)


def _compress_weights(pos_k, w1_k, w2_k, pos_v, w1_v, w2_v):
    n_slot = 2 * KV_HEADS
    eye = jnp.eye(n_slot, dtype=F32)

    def place(per_slot):
        w = jnp.stack(per_slot, axis=0)
        full = jnp.einsum('jclde,jk->ljdcke', w, eye)
        return full.reshape(CHUNK_COLS, N_SUB * KV_COLS)

    w1k = w1_k.reshape(N_SUB, CMP_STRIDE, HEAD_DIM, HEAD_DIM)
    w1v = w1_v.reshape(N_SUB, CMP_STRIDE, HEAD_DIM, HEAD_DIM)
    wc = place([w1k, w1k, w1v, w1v]).astype(BF16)
    pk = pos_k.reshape(N_SUB, CMP_STRIDE, 1, HEAD_DIM)
    pv = pos_v.reshape(N_SUB, CMP_STRIDE, 1, HEAD_DIM)
    posx = jnp.concatenate([pk, pk, pv, pv], axis=2).reshape(N_SUB, CHUNK_COLS)
    posx = jnp.pad(posx, ((0, 8 - N_SUB), (0, 0))).astype(BF16)
    w2 = jnp.einsum('jef,jk->jekf', jnp.stack([w2_k, w2_k, w2_v, w2_v]), eye).reshape(KV_COLS, KV_COLS).astype(BF16)
    return wc, posx, w2


def _split_kv(wc):
    half = KV_COLS // 2
    w = wc.reshape(CMP_STRIDE, 2, half, N_SUB, 2, half)
    return jnp.stack([w[:, kv, :, :, kv, :].reshape(CMP_STRIDE * half, N_SUB * half) for kv in range(2)])


def _selection_matrix(n_rows, n_cols):
    n = jnp.arange(n_rows)[:, None]
    j = jnp.arange(n_cols)[None, :]
    cnt = jnp.zeros((n_rows, n_cols), F32)
    for m in range(SEL_RATIO):
        for sub in range(N_SUB):
            cnt = cnt + (SEL_RATIO * j + m - sub == n).astype(F32)
    return cnt.astype(BF16)


def kernel(x_prompt, x_sample, cache_cmp_kv, cache_slc_kv, state_win_kv, state_conv, state_pool, page_table,
           norm_mix, norm_ffn, norm_final, w_in_a, w_out_a, cmp_pos_k, cmp_w1_k, cmp_w2_k, cmp_pos_v, cmp_w1_v,
           cmp_w2_v, conv_w, conv_b, conv_ln_g, conv_ln_b, pool_w, pool_scale, w_ffn_gate, w_ffn_up, w_ffn_down):
    bp, tp, _ = x_prompt.shape
    bs, s_new, _ = x_sample.shape
    n_pages = page_table.shape[1]
    past = n_pages * PAGE_SIZE
    w_buf = state_win_kv.shape[2]
    n_pool = cache_cmp_kv.shape[1]
    ns_p = tp // SEL_BLOCK
    ns_s = -(-(past + s_new) // SEL_BLOCK)
    assert s_new == 1 and tp % SEL_CHUNK == 0 and tp >= WIN_KEYS and N_SELECT <= ns_p <= HEAD_DIM
    assert ns_s > N_SELECT and norm_mix.shape[0] == 2 and ns_p % SUBLANE == 0

    row = lambda v: v.reshape(1, -1)
    w_in = _inproj_weight(w_in_a[0])
    wc, posx, w2 = _compress_weights(cmp_pos_k[0], cmp_w1_k[0], cmp_w2_k[0], cmp_pos_v[0], cmp_w1_v[0], cmp_w2_v[0])
    woa, woc = w_out_a[0, :ATTN_WIDTH].astype(BF16), w_out_a[0, ATTN_WIDTH:].astype(BF16)
    wg, wu, wd = w_ffn_gate.astype(BF16), w_ffn_up.astype(BF16), w_ffn_down.astype(BF16)
    cw, cb, lg, lb = conv_w[0], row(conv_b[0]), row(conv_ln_g[0]), row(conv_ln_b[0])
    pw, ps = pool_w[0].astype(BF16), row(pool_scale[0])

    m = bp * tp
    xp = x_prompt.reshape(m, D_MODEL)
    q, kvc, kvs, kvw, gates, a, ksa, vs, kw, vw, kvct, kvst, kvwt = _inproj(
        xp, row(norm_mix[0]), w_in, tm=512, seq_len=tp, transposed_kv=True)
    cw_rows = jnp.broadcast_to(cw[:, None, :], (CONV_WIDTH, SUBLANE, CONV_CH))
    c_out = _conv_prompt(a.reshape(bp, tp, CONV_CH), cw_rows, cb, lg, lb, tc=512)
    kc, vc, pos_term = _compress_prompt(kvct, wc, posx, w2)
    a_out = _nsa_prompt(q, gates, ksa, vs, kw, vw, kc, vc, _selection_matrix(tp // CMP_STRIDE, LANE),
                        bsz=bp, seq_len=tp)
    xp = _ffn0(xp, a_out, c_out.reshape(m, CONV_CH), woa, woc, row(norm_ffn[0]), wg[0], wu[0], wd[0], tm=512)
    y_prompt, pool_tail = _layer1_prompt(xp.reshape(bp, tp, D_MODEL), row(norm_mix[1]), pw, ps, row(norm_ffn[1]),
                                         wg[1], wu[1], wd[1], row(norm_final), tm=512)
    kv6 = lambda z, b: z.reshape(1, b, -1, 2, KV_HEADS, HEAD_DIM)
    kv6_t = lambda zt: jnp.transpose(zt.reshape(bp, 2, KV_HEADS, HEAD_DIM, tp), (0, 4, 1, 2, 3))[None]
    new_cmp_p, new_slc_p = kv6_t(kvct), kv6_t(kvst)
    new_win_p = kv6_t(kvwt)[:, :, -min(WINDOW, tp):]
    new_conv_p = a.reshape(1, bp, tp, CONV_CH)[:, :, -(CONV_WIDTH - 1):]
    new_pool_p = pool_tail[None, :, -POOL_BUF:]

    xs = x_sample.reshape(bs, D_MODEL)
    q, kvc, kvs, kvw, gates, a = _inproj(xs, row(norm_mix[0]), w_in, tm=bs, seq_len=1, transposed_kv=False)[:6]
    q8 = q.reshape(bs, N_HEADS, HEAD_DIM)
    nsp = -(-ns_s // LANE) * LANE
    rows_last = lambda c: jnp.transpose(c, (0, 2, 3, 4, 1))
    o_c, idx = _cmp_sample(page_table, q8, rows_last(cache_cmp_kv[0]).reshape(n_pool, KV_COLS, PAGE_SIZE),
                           _split_kv(wc), pos_term, w2,
                           _selection_matrix(n_pages * (PAGE_SIZE // CMP_STRIDE), nsp), past=past, ns=ns_s)
    gates8 = gates[:, :N_HEADS * N_BRANCH].reshape(bs, N_HEADS, N_BRANCH)
    att, c_s = _mix_sample(idx, page_table, q8, gates8, o_c, kvs.reshape(bs, 1, KV_COLS), kvw.reshape(bs, 1, KV_COLS),
                           rows_last(state_win_kv[0]).reshape(bs, 2 * KV_HEADS, HEAD_DIM, w_buf), state_conv[0],
                           a.reshape(bs, 1, CONV_CH), cw, cb, lg, lb,
                           rows_last(cache_slc_kv[0]).reshape(n_pool * 2 * KV_HEADS, HEAD_DIM, PAGE_SIZE),
                           past=past, n_gather=N_SELECT - 1)
    xs = _ffn0(xs, att.reshape(bs, ATTN_WIDTH).astype(BF16), c_s.reshape(bs, CONV_CH).astype(BF16), woa, woc,
               row(norm_ffn[0]), wg[0], wu[0], wd[0], tm=bs)
    y_sample, h_s = _layer1_sample(xs, jnp.swapaxes(state_pool[0], 0, 1), row(norm_mix[1]), pw, ps, row(norm_ffn[1]),
                                   wg[1], wu[1], wd[1], row(norm_final), first_pos=past)
    new_win_s = jnp.concatenate([state_win_kv[0], kv6(kvw, bs)[0]], axis=1)[None, :, -w_buf:]
    new_conv_s = jnp.concatenate([state_conv[0], a.reshape(bs, 1, CONV_CH)], axis=1)[None, :, -(CONV_WIDTH - 1):]
    new_pool_s = jnp.concatenate([state_pool[0], h_s[:, None, :]], axis=1)[None, :, -POOL_BUF:]

    return (y_prompt, y_sample.reshape(bs, s_new, D_MODEL), new_cmp_p, kv6(kvc, bs), new_slc_p, kv6(kvs, bs),
            new_win_p, new_win_s, new_conv_p, new_conv_s, new_pool_p, new_pool_s)
```
